```python
import jax, jax.numpy as jnp
from jax import lax
import numpy as np

D_MODEL = 1024
BATCH = 8
SEQ = 2048
DEPTH = 2
DEC_BATCH = 128
DEC_SEQ = 4
PAST_LEN = 16384
PAGE_SIZE = 128

N_MIXERS = 2
N_A_LAYERS = (DEPTH + 1) // 2
N_B_LAYERS = DEPTH // 2
MIX_WIDTH = D_MODEL // 2
N_MEM = 256
N_XHEADS = 4
XHEAD_DIM = (D_MODEL // 2) // N_XHEADS
XATTN_WIDTH = N_XHEADS * XHEAD_DIM
CONV_A_WIDTH = 31
CONV_B_WIDTH = 3
N_EXPERTS = 64
TOP_K = 8
N_GROUPS = 8
TOPK_GROUPS = 4
EXPERT_FF = D_MODEL // 4
SHARED_FF = EXPERT_FF
ROUTED_SCALE = 2.5
EXPERT_BLOCK = 128
LN_EPS = 1e-5
DEEPNORM_ALPHA = (2 * DEPTH) ** 0.25
DEEPNORM_BETA = (8 * DEPTH) ** -0.25

kernel_name = "interleaved_conv_memxattn_moe_step"


def layer_norm(x, g, b):
    xf = x.astype(jnp.float32)
    mu = jnp.mean(xf, axis=-1, keepdims=True)
    var = jnp.mean(jnp.square(xf - mu), axis=-1, keepdims=True)
    return ((xf - mu) * lax.rsqrt(var + LN_EPS) * g.astype(jnp.float32) + b.astype(jnp.float32)).astype(x.dtype)


def causal_depthwise_conv(u, hist, w):
    width, ch = w.shape
    full = jnp.concatenate([hist.astype(u.dtype), u], axis=1)
    y = lax.conv_general_dilated(full, w.astype(u.dtype)[:, None, :], window_strides=(1,), padding='VALID',
                                 dimension_numbers=('NWC', 'WIO', 'NWC'), feature_group_count=ch)
    return y, full[:, full.shape[1] - (width - 1):]


def memory_attention(q, mem_k, mem_v):
    q = q.reshape(q.shape[0], q.shape[1], N_XHEADS, XHEAD_DIM)
    s = jnp.einsum('blhd,bmhd->bhlm', q, mem_k.astype(q.dtype)).astype(jnp.float32) * (XHEAD_DIM ** -0.5)
    p = jax.nn.softmax(s, axis=-1).astype(q.dtype)
    o = jnp.einsum('bhlm,bmhd->blhd', p, mem_v.astype(q.dtype))
    return o.reshape(o.shape[0], o.shape[1], XATTN_WIDTH)


def route(xt, w_router, router_bias):
    t = xt.shape[0]
    scores = jax.nn.sigmoid((xt @ w_router).astype(jnp.float32))
    biased = scores + router_bias.astype(jnp.float32)
    grp = biased.reshape(t, N_GROUPS, N_EXPERTS // N_GROUPS)
    grp_score = jnp.sum(lax.top_k(grp, 2)[0], axis=-1)
    _, grp_idx = lax.top_k(grp_score, TOPK_GROUPS)
    grp_mask = jnp.any(grp_idx[..., None] == jnp.arange(N_GROUPS), axis=-2)
    expert_mask = jnp.repeat(grp_mask, N_EXPERTS // N_GROUPS, axis=-1)
    _, idx = lax.top_k(jnp.where(expert_mask, biased, -jnp.inf), TOP_K)
    w = jnp.take_along_axis(scores, idx, axis=-1)
    w = w / jnp.sum(w, axis=-1, keepdims=True) * ROUTED_SCALE
    return idx, w


def routed_experts(xt, idx, gate, w_gate, w_up, w_down):
    t, d = xt.shape
    n_slots = t * TOP_K
    flat_e = idx.reshape(n_slots)
    order = jnp.argsort(flat_e)
    sorted_e = flat_e[order]
    sorted_tok = (order // TOP_K).astype(jnp.int32)
    sorted_gate = gate.reshape(n_slots)[order]
    counts = jnp.zeros((N_EXPERTS,), jnp.int32).at[flat_e].add(1)
    padded = (counts + EXPERT_BLOCK - 1) // EXPERT_BLOCK * EXPERT_BLOCK
    pad_end = jnp.cumsum(padded)
    pad_start = pad_end - padded
    start = jnp.cumsum(counts) - counts
    dest = pad_start[sorted_e] + jnp.arange(n_slots, dtype=jnp.int32) - start[sorted_e]
    n_blocks = -(-n_slots // EXPERT_BLOCK) + N_EXPERTS
    slot_tok = jnp.full((n_blocks * EXPERT_BLOCK,), t, jnp.int32).at[dest].set(sorted_tok)
    slot_gate = jnp.zeros((n_blocks * EXPERT_BLOCK,), xt.dtype).at[dest].set(sorted_gate.astype(xt.dtype))
    block_expert = jnp.minimum(jnp.searchsorted(pad_end, jnp.arange(n_blocks) * EXPERT_BLOCK, side='right'),
                               N_EXPERTS - 1).astype(jnp.int32)
    x_pad = jnp.concatenate([xt, jnp.zeros((1, d), xt.dtype)], axis=0)

    def body(y, blk):
        toks, gts, e = blk
        xb = x_pad[toks]
        h = jax.nn.silu(xb @ w_gate[e]) * (xb @ w_up[e])
        return y.at[toks].add((h @ w_down[e]) * gts[:, None]), None

    y, _ = lax.scan(body, jnp.zeros((t + 1, d), xt.dtype),
                    (slot_tok.reshape(n_blocks, EXPERT_BLOCK), slot_gate.reshape(n_blocks, EXPERT_BLOCK), block_expert))
    return y[:t]


def setup_inputs(seed: int = 0) -> dict:
    key = jax.random.key(seed)
    ks = iter(jax.random.split(key, 40))

    def nrm(shape, scale):
        return jax.random.normal(next(ks), shape, jnp.float32) * scale

    d, c, f, e = D_MODEL, MIX_WIDTH, EXPERT_FF, N_EXPERTS
    beta = DEEPNORM_BETA
    win_a = 2 * c + XATTN_WIDTH
    win_b = 3 * c + XATTN_WIDTH
    w_kv = jnp.concatenate([nrm((DEPTH, d, XATTN_WIDTH), d ** -0.5),
                            nrm((DEPTH, d, XATTN_WIDTH), d ** -0.5 * beta)], axis=-1)
    return {
        "x_prompt": nrm((BATCH, SEQ, d), 1.0),
        "x_sample": nrm((DEC_BATCH, DEC_SEQ, d), 1.0),
        "mem_prompt": nrm((BATCH, N_MEM, d), 1.0),
        "cache_mem_k": nrm((DEPTH, DEC_BATCH, N_MEM, N_XHEADS, XHEAD_DIM), 1.0),
        "cache_mem_v": nrm((DEPTH, DEC_BATCH, N_MEM, N_XHEADS, XHEAD_DIM), beta),
        "state_conv_a": nrm((N_A_LAYERS, DEC_BATCH, CONV_A_WIDTH - 1, c), 0.5),
        "state_conv_b": nrm((N_B_LAYERS, DEC_BATCH, CONV_B_WIDTH - 1, c), 0.5),
        "w_in_a": nrm((N_A_LAYERS, d, win_a), d ** -0.5),
        "conv_a_w": nrm((N_A_LAYERS, CONV_A_WIDTH, c), CONV_A_WIDTH ** -0.5),
        "conv_a_b": nrm((N_A_LAYERS, c), 0.02),
        "norm_a_g": 1.0 + nrm((N_A_LAYERS, c), 0.02),
        "norm_a_b": nrm((N_A_LAYERS, c), 0.02),
        "w_in_b": nrm((N_B_LAYERS, d, win_b), d ** -0.5),
        "conv_b_w": nrm((N_B_LAYERS, CONV_B_WIDTH, c), CONV_B_WIDTH ** -0.5),
        "w_kv": w_kv,
        "w_out": nrm((DEPTH, c + XATTN_WIDTH, d), (c + XATTN_WIDTH) ** -0.5 * beta),
        "ln1_g": 1.0 + nrm((DEPTH, d), 0.02),
        "ln1_b": nrm((DEPTH, d), 0.02),
        "w_router": nrm((DEPTH, d, e), d ** -0.5),
        "router_bias": nrm((DEPTH, e), 0.01),
        "w_gate": nrm((DEPTH, e, d, f), d ** -0.5),
        "w_up": nrm((DEPTH, e, d, f), d ** -0.5),
        "w_down": nrm((DEPTH, e, f, d), f ** -0.5 * beta),
        "ws_gate": nrm((DEPTH, d, SHARED_FF), d ** -0.5),
        "ws_up": nrm((DEPTH, d, SHARED_FF), d ** -0.5),
        "ws_down": nrm((DEPTH, SHARED_FF, d), SHARED_FF ** -0.5 * beta),
        "ln2_g": 1.0 + nrm((DEPTH, d), 0.02),
        "ln2_b": nrm((DEPTH, d), 0.02),
    }


def reference(x_prompt, x_sample, mem_prompt, cache_mem_k, cache_mem_v, state_conv_a, state_conv_b,
              w_in_a, conv_a_w, conv_a_b, norm_a_g, norm_a_b, w_in_b, conv_b_w, w_kv, w_out, ln1_g, ln1_b,
              w_router, router_bias, w_gate, w_up, w_down, ws_gate, ws_up, ws_down, ln2_g, ln2_b):
    c = MIX_WIDTH

    def token_sublayer(x, hist, mem_k, mem_v, i):
        j = i // N_MIXERS
        if i % N_MIXERS == 0:
            u = x @ w_in_a[j]
            a, g, q = jnp.split(u, [c, 2 * c], axis=-1)
            conv, new_hist = causal_depthwise_conv(a * jax.nn.sigmoid(g), hist, conv_a_w[j])
            mix = jax.nn.silu(layer_norm(conv + conv_a_b[j], norm_a_g[j], norm_a_b[j]))
        else:
            u = x @ w_in_b[j]
            b_gate, c_gate, h, q = jnp.split(u, [c, 2 * c, 3 * c], axis=-1)
            conv, new_hist = causal_depthwise_conv(c_gate * h, hist, conv_b_w[j])
            mix = b_gate * conv
        attn = memory_attention(q, mem_k, mem_v)
        out = jnp.concatenate([mix, attn], axis=-1) @ w_out[i]
        return layer_norm(DEEPNORM_ALPHA * x + out, ln1_g[i], ln1_b[i]), new_hist

    def channel_sublayer(x, i):
        xt = x.reshape(-1, x.shape[-1])
        idx, gate = route(xt, w_router[i], router_bias[i])
        shared = (jax.nn.silu(xt @ ws_gate[i]) * (xt @ ws_up[i])) @ ws_down[i]
        y = routed_experts(xt, idx, gate, w_gate[i], w_up[i], w_down[i]) + shared
        return layer_norm(DEEPNORM_ALPHA * x + y.reshape(x.shape), ln2_g[i], ln2_b[i])

    hp, hs = x_prompt, x_sample
    new_k, new_v = [], []
    conv_a_p, conv_b_p, conv_a_s, conv_b_s = [], [], [], []
    for i in range(DEPTH):
        j = i // N_MIXERS
        kv = mem_prompt @ w_kv[i]
        k_p = kv[..., :XATTN_WIDTH].reshape(kv.shape[0], kv.shape[1], N_XHEADS, XHEAD_DIM)
        v_p = kv[..., XATTN_WIDTH:].reshape(kv.shape[0], kv.shape[1], N_XHEADS, XHEAD_DIM)
        new_k.append(k_p)
        new_v.append(v_p)
        is_a = i % N_MIXERS == 0
        width = CONV_A_WIDTH if is_a else CONV_B_WIDTH
        hist_s = state_conv_a[j] if is_a else state_conv_b[j]
        zero_hist = jnp.zeros((hp.shape[0], width - 1, c), hp.dtype)
        hp, hist_p_new = token_sublayer(hp, zero_hist, k_p, v_p, i)
        hs, hist_s_new = token_sublayer(hs, hist_s, cache_mem_k[i], cache_mem_v[i], i)
        if is_a:
            conv_a_p.append(hist_p_new)
            conv_a_s.append(hist_s_new)
        else:
            conv_b_p.append(hist_p_new)
            conv_b_s.append(hist_s_new)
        hp = channel_sublayer(hp, i)
        hs = channel_sublayer(hs, i)

    return (hp, hs, jnp.stack(new_k), jnp.stack(new_v), jnp.stack(conv_a_p), jnp.stack(conv_b_p),
            jnp.stack(conv_a_s), jnp.stack(conv_b_s))
```

```python
import functools

import jax
import jax.numpy as jnp
from jax import lax
from jax.experimental import pallas as pl
from jax.experimental.pallas import tpu as pltpu

D_MODEL = 1024
DEPTH = 2
N_MIXERS = 2
MIX_WIDTH = D_MODEL // 2
N_MEM = 256
N_XHEADS = 4
XHEAD_DIM = MIX_WIDTH // N_XHEADS
XATTN_WIDTH = N_XHEADS * XHEAD_DIM
CONV_A_WIDTH = 31
CONV_B_WIDTH = 3
N_EXPERTS = 64
TOP_K = 8
N_GROUPS = 8
GROUP_SIZE = N_EXPERTS // N_GROUPS
TOPK_GROUPS = 4
EXPERT_FF = D_MODEL // 4
ROUTED_SCALE = 2.5
LN_EPS = 1e-5
DEEPNORM_ALPHA = (2 * DEPTH) ** 0.25

V7X_LANES = 128
V7X_SUBLANES = 8
V7X_VMEM_BYTES = 64 * 1024 * 1024

HIST_PAD = 32
PROMPT_SEQ_TILE = 512
SAMPLE_BATCH_BLOCK = 8
ROUTE_TILE = 512
MOE_TILE_PROMPT = 1024
MOE_TILE_SAMPLE = 512
GATE_LANES = 128
NEG_INF = float("-inf")


def _vmem_limit(nbytes):
    return int(min(max(nbytes, 16 * 1024 * 1024), V7X_VMEM_BYTES - 8 * 1024 * 1024))


def _bf(x):
    return x.astype(jnp.bfloat16)


def _dot(a, b):
    return jnp.dot(a, b, preferred_element_type=jnp.float32)


def _dot_nt(a, b):
    return lax.dot_general(a, b, (((1,), (1,)), ((), ())), preferred_element_type=jnp.float32)


def _sigmoid(x):
    return 1.0 / (1.0 + jnp.exp(-x))


def _silu(x):
    return x * _sigmoid(x)


def _layer_norm(x, g, b):
    mu = jnp.mean(x, axis=-1, keepdims=True)
    xc = x - mu
    var = jnp.mean(xc * xc, axis=-1, keepdims=True)
    return xc * lax.rsqrt(var + LN_EPS) * g + b


def _memory_attention(q, k, v, same_seq=None):
    outs = []
    for h in range(N_XHEADS):
        sl = slice(h * XHEAD_DIM, (h + 1) * XHEAD_DIM)
        s = _dot_nt(_bf(q[:, sl]), _bf(k[:, sl])) * (XHEAD_DIM ** -0.5)
        if same_seq is not None:
            s = jnp.where(same_seq, s, NEG_INF)
        e = jnp.exp(s - jnp.max(s, axis=-1, keepdims=True))
        p = e / jnp.sum(e, axis=-1, keepdims=True)
        outs.append(_dot(_bf(p), _bf(v[:, sl])))
    return jnp.concatenate(outs, axis=-1)


def _kv_body(mem_ref, w_ref, k_ref, v_ref):
    kv = _dot(_bf(mem_ref[...]), _bf(w_ref[0]))
    k_ref[0] = kv[:, :XATTN_WIDTH]
    v_ref[0] = kv[:, XATTN_WIDTH:]


def _kv_projection(mem2d, w_kv):
    rows = mem2d.shape[0]
    tm = 512
    out = jax.ShapeDtypeStruct((DEPTH, rows, XATTN_WIDTH), jnp.float32)
    return pl.pallas_call(
        _kv_body,
        grid=(DEPTH, rows // tm),
        in_specs=[pl.BlockSpec((tm, D_MODEL), lambda i, m: (m, 0)),
                  pl.BlockSpec((1, D_MODEL, 2 * XATTN_WIDTH), lambda i, m: (i, 0, 0))],
        out_specs=[pl.BlockSpec((1, tm, XATTN_WIDTH), lambda i, m: (i, m, 0)),
                   pl.BlockSpec((1, tm, XATTN_WIDTH), lambda i, m: (i, m, 0))],
        out_shape=[out, out],
        compiler_params=pltpu.CompilerParams(dimension_semantics=("arbitrary", "arbitrary"),
                                             vmem_limit_bytes=_vmem_limit(32 * 1024 * 1024)),
        name="kv_projection",
    )(mem2d, w_kv)


def _prompt_token_body(is_a, x_ref, w_in_ref, cw_ref, cb_ref, ng_ref, nb_ref, k_ref, v_ref, w_out_ref,
                       g1_ref, b1_ref, y_ref, hist_ref, buf_ref):
    c = MIX_WIDTH
    tl = x_ref.shape[0]
    width = CONV_A_WIDTH if is_a else CONV_B_WIDTH
    step = pl.program_id(1)

    @pl.when(step == 0)
    def _():
        buf_ref[pl.ds(0, HIST_PAD), :] = jnp.zeros((HIST_PAD, c), jnp.float32)

    x = x_ref[...]
    u = _dot(_bf(x), w_in_ref[...])
    if is_a:
        conv_in = u[:, :c] * _sigmoid(u[:, c:2 * c])
        q = u[:, 2 * c:]
    else:
        conv_in = u[:, c:2 * c] * u[:, 2 * c:3 * c]
        q = u[:, 3 * c:]
    buf_ref[pl.ds(HIST_PAD, tl), :] = conv_in

    base = HIST_PAD - (width - 1)
    conv = cw_ref[0:1, :] * buf_ref[pl.ds(base, tl), :]
    for t in range(1, width):
        conv = conv + cw_ref[t:t + 1, :] * buf_ref[pl.ds(base + t, tl), :]

    if is_a:
        mix = _silu(_layer_norm(conv + cb_ref[...], ng_ref[...], nb_ref[...]))
    else:
        mix = u[:, :c] * conv

    attn = _memory_attention(q, k_ref[0], v_ref[0])
    out = _dot(_bf(jnp.concatenate([mix, attn], axis=-1)), w_out_ref[...])
    y_ref[...] = _layer_norm(DEEPNORM_ALPHA * x + out, g1_ref[...], b1_ref[...])

    @pl.when(step == pl.num_programs(1) - 1)
    def _():
        hist_ref[0] = buf_ref[pl.ds(HIST_PAD + tl - (width - 1), width - 1), :]

    buf_ref[pl.ds(0, HIST_PAD), :] = buf_ref[pl.ds(tl, HIST_PAD), :]


def _prompt_token_sublayer(is_a, x, batch, w_in, cw, cb, ng, nb, k, v, w_out, g1, b1):
    rows = x.shape[0]
    seq = rows // batch
    tl = PROMPT_SEQ_TILE
    nl = seq // tl
    c = MIX_WIDTH
    width = CONV_A_WIDTH if is_a else CONV_B_WIDTH
    n_in = w_in.shape[1]
    full = lambda shape: pl.BlockSpec(shape, lambda b, l: tuple(0 for _ in shape))
    return pl.pallas_call(
        functools.partial(_prompt_token_body, is_a),
        grid=(batch, nl),
        in_specs=[pl.BlockSpec((tl, D_MODEL), lambda b, l: (b * nl + l, 0)),
                  full((D_MODEL, n_in)), full((width, c)), full((1, c)), full((1, c)), full((1, c)),
                  pl.BlockSpec((1, N_MEM, XATTN_WIDTH), lambda b, l: (b, 0, 0)),
                  pl.BlockSpec((1, N_MEM, XATTN_WIDTH), lambda b, l: (b, 0, 0)),
                  full((c + XATTN_WIDTH, D_MODEL)), full((1, D_MODEL)), full((1, D_MODEL))],
        out_specs=[pl.BlockSpec((tl, D_MODEL), lambda b, l: (b * nl + l, 0)),
                   pl.BlockSpec((1, width - 1, c), lambda b, l: (b, 0, 0))],
        out_shape=[jax.ShapeDtypeStruct((rows, D_MODEL), jnp.float32),
                   jax.ShapeDtypeStruct((batch, width - 1, c), jnp.float32)],
        scratch_shapes=[pltpu.VMEM((HIST_PAD + tl, c), jnp.float32)],
        compiler_params=pltpu.CompilerParams(dimension_semantics=("arbitrary", "arbitrary"),
                                             vmem_limit_bytes=_vmem_limit(48 * 1024 * 1024)),
        name="prompt_token_a" if is_a else "prompt_token_b",
    )(x, w_in, cw, cb, ng, nb, k, v, w_out, g1, b1)


def _sample_mix_body(is_a, n_seq, x_ref, w_in_ref, hist_ref, cw_ref, cb_ref, ng_ref, nb_ref,
                     mix_ref, q_ref, new_hist_ref):
    c = MIX_WIDTH
    width = CONV_A_WIDTH if is_a else CONV_B_WIDTH
    n_hist = width - 1
    n_pos = x_ref.shape[0] // n_seq
    u = _dot(_bf(x_ref[...]), w_in_ref[...])
    if is_a:
        conv_in = u[:, :c] * _sigmoid(u[:, c:2 * c])
        q_ref[...] = u[:, 2 * c:]
    else:
        conv_in = u[:, c:2 * c] * u[:, 2 * c:3 * c]
        q_ref[...] = u[:, 3 * c:]

    def full_row(j):
        if j < n_hist:
            return hist_ref[j]
        return conv_in[(j - n_hist) * n_seq:(j - n_hist + 1) * n_seq, :]

    for l in range(n_pos):
        conv = cw_ref[0:1, :] * full_row(l)
        for t in range(1, width):
            conv = conv + cw_ref[t:t + 1, :] * full_row(l + t)
        rows = slice(l * n_seq, (l + 1) * n_seq)
        if is_a:
            mix_ref[rows, :] = _silu(_layer_norm(conv + cb_ref[...], ng_ref[...], nb_ref[...]))
        else:
            mix_ref[rows, :] = u[rows, :c] * conv
    for j in range(n_hist):
        new_hist_ref[j] = full_row(j + n_pos)


def _sample_mix(is_a, n_seq, x, w_in, hist, cw, cb, ng, nb):
    rows = x.shape[0]
    c = MIX_WIDTH
    full = lambda shape: pl.BlockSpec(shape, lambda i: tuple(0 for _ in shape))
    args = (x, w_in, hist, cw, cb, ng, nb)
    return pl.pallas_call(
        functools.partial(_sample_mix_body, is_a, n_seq),
        grid=(1,),
        in_specs=[full(a.shape) for a in args],
        out_specs=[full((rows, c)), full((rows, XATTN_WIDTH)), full(hist.shape)],
        out_shape=[jax.ShapeDtypeStruct((rows, c), jnp.float32),
                   jax.ShapeDtypeStruct((rows, XATTN_WIDTH), jnp.float32),
                   jax.ShapeDtypeStruct(hist.shape, jnp.float32)],
        compiler_params=pltpu.CompilerParams(dimension_semantics=("arbitrary",),
                                             vmem_limit_bytes=_vmem_limit(48 * 1024 * 1024)),
        name="sample_mix_a" if is_a else "sample_mix_b",
    )(*args)


def _sample_attn_body(n_seq, q_ref, k_ref, v_ref, o_ref):
    bb = k_ref.shape[0]
    n_pos = q_ref.shape[0] // n_seq
    first = pl.multiple_of(pl.program_id(0) * bb, bb)
    q = jnp.concatenate([q_ref[pl.ds(l * n_seq + first, bb), :] for l in range(n_pos)], axis=0)
    k = k_ref[...].reshape(bb * N_MEM, XATTN_WIDTH)
    v = v_ref[...].reshape(bb * N_MEM, XATTN_WIDTH)
    row_seq = lax.broadcasted_iota(jnp.int32, (n_pos * bb, bb * N_MEM), 0) % bb
    col_seq = lax.broadcasted_iota(jnp.int32, (n_pos * bb, bb * N_MEM), 1) // N_MEM
    o = _memory_attention(q, k, v, same_seq=row_seq == col_seq)
    for l in range(n_pos):
        o_ref[pl.ds(l * n_seq + first, bb), :] = o[l * bb:(l + 1) * bb, :]


def _sample_attention(n_seq, q, mem_k, mem_v):
    rows = q.shape[0]
    bb = SAMPLE_BATCH_BLOCK
    return pl.pallas_call(
        functools.partial(_sample_attn_body, n_seq),
        grid=(n_seq // bb,),
        in_specs=[pl.BlockSpec((rows, XATTN_WIDTH), lambda i: (0, 0)),
                  pl.BlockSpec((bb, N_MEM, XATTN_WIDTH), lambda i: (i, 0, 0)),
                  pl.BlockSpec((bb, N_MEM, XATTN_WIDTH), lambda i: (i, 0, 0))],
        out_specs=pl.BlockSpec((rows, XATTN_WIDTH), lambda i: (0, 0)),
        out_shape=jax.ShapeDtypeStruct((rows, XATTN_WIDTH), jnp.float32),
        compiler_params=pltpu.CompilerParams(dimension_semantics=("arbitrary",),
                                             vmem_limit_bytes=_vmem_limit(40 * 1024 * 1024)),
        name="sample_attention",
    )(q, mem_k, mem_v)


def _sample_out_body(x_ref, mix_ref, attn_ref, w_out_ref, g1_ref, b1_ref, y_ref):
    cat = jnp.concatenate([mix_ref[...], attn_ref[...]], axis=-1)
    out = _dot(_bf(cat), w_out_ref[...])
    y_ref[...] = _layer_norm(DEEPNORM_ALPHA * x_ref[...] + out, g1_ref[...], b1_ref[...])


def _sample_out(x, mix, attn, w_out, g1, b1):
    full = lambda shape: pl.BlockSpec(shape, lambda i: tuple(0 for _ in shape))
    args = (x, mix, attn, w_out, g1, b1)
    return pl.pallas_call(
        _sample_out_body,
        grid=(1,),
        in_specs=[full(a.shape) for a in args],
        out_specs=full(x.shape),
        out_shape=jax.ShapeDtypeStruct(x.shape, jnp.float32),
        compiler_params=pltpu.CompilerParams(dimension_semantics=("arbitrary",),
                                             vmem_limit_bytes=_vmem_limit(32 * 1024 * 1024)),
        name="sample_out",
    )(*args)


def _first_index_of(mask, index, n):
    cand = jnp.where(mask, index, float(n))
    while cand.ndim > 2:
        cand = jnp.min(cand, axis=0)
    return jnp.min(cand, axis=0, keepdims=True)


def _max_all(x):
    while x.ndim > 2:
        x = jnp.max(x, axis=0)
    return jnp.max(x, axis=0, keepdims=True)


def _route_body(x_ref, wr_ref, bias_ref, gate_ref):
    tm = x_ref.shape[0]
    logits = _dot_nt(wr_ref[...], _bf(x_ref[...]))
    scores = _sigmoid(logits)
    biased = scores + bias_ref[...]
    grp = biased.reshape(N_GROUPS, GROUP_SIZE, tm)

    within = lax.broadcasted_iota(jnp.int32, grp.shape, 1).astype(jnp.float32)
    top1 = jnp.max(grp, axis=1, keepdims=True)
    first = jnp.min(jnp.where(grp == top1, within, float(GROUP_SIZE)), axis=1, keepdims=True)
    top2 = jnp.max(jnp.where(within == first, NEG_INF, grp), axis=1, keepdims=True)
    grp_score = (top1 + top2).reshape(N_GROUPS, tm)

    gidx = lax.broadcasted_iota(jnp.int32, grp_score.shape, 0).astype(jnp.float32)
    grp_sel = jnp.zeros(grp_score.shape, jnp.float32)
    for _ in range(TOPK_GROUPS):
        best = jnp.max(grp_score, axis=0, keepdims=True)
        pick = gidx == _first_index_of(grp_score == best, gidx, N_GROUPS)
        grp_sel = jnp.where(pick, 1.0, grp_sel)
        grp_score = jnp.where(pick, NEG_INF, grp_score)

    eidx = (lax.broadcasted_iota(jnp.int32, grp.shape, 0) * GROUP_SIZE
            + lax.broadcasted_iota(jnp.int32, grp.shape, 1)).astype(jnp.float32)
    cand = jnp.where(grp_sel.reshape(N_GROUPS, 1, tm) > 0.0, grp, NEG_INF)
    chosen = jnp.zeros(grp.shape, jnp.float32)
    for _ in range(TOP_K):
        best = _max_all(cand).reshape(1, 1, tm)
        pick = eidx == _first_index_of(cand == best, eidx, N_EXPERTS).reshape(1, 1, tm)
        chosen = jnp.where(pick, 1.0, chosen)
        cand = jnp.where(pick, NEG_INF, cand)

    w = jnp.where(chosen > 0.0, scores.reshape(grp.shape), 0.0)
    total = jnp.sum(jnp.sum(w, axis=0), axis=0, keepdims=True).reshape(1, 1, tm)
    gates = (w / total * ROUTED_SCALE).reshape(N_EXPERTS, tm)
    padded = jnp.concatenate([gates, jnp.zeros((GATE_LANES - N_EXPERTS, tm), jnp.float32)], axis=0)
    gate_ref[...] = padded.T


def _route(x, w_router_t, bias_col):
    rows = x.shape[0]
    tm = ROUTE_TILE
    return pl.pallas_call(
        _route_body,
        grid=(rows // tm,),
        in_specs=[pl.BlockSpec((tm, D_MODEL), lambda i: (i, 0)),
                  pl.BlockSpec((N_EXPERTS, D_MODEL), lambda i: (0, 0)),
                  pl.BlockSpec((N_EXPERTS, 1), lambda i: (0, 0))],
        out_specs=pl.BlockSpec((tm, GATE_LANES), lambda i: (i, 0)),
        out_shape=jax.ShapeDtypeStruct((rows, GATE_LANES), jnp.float32),
        compiler_params=pltpu.CompilerParams(dimension_semantics=("arbitrary",),
                                             vmem_limit_bytes=_vmem_limit(24 * 1024 * 1024)),
        name="route",
    )(x, w_router_t, bias_col)


def _moe_body(x_ref, gate_ref, wg_ref, wu_ref, wd_ref, g2_ref, b2_ref, y_ref, xb_ref, acc_ref):
    e = pl.program_id(1)

    @pl.when(e == 0)
    def _():
        xb_ref[...] = _bf(x_ref[...])
        acc_ref[...] = jnp.zeros(acc_ref.shape, jnp.float32)

    xb = xb_ref[...]
    h = _silu(_dot(xb, wg_ref[0])) * _dot(xb, wu_ref[0])
    down = _dot(_bf(h), wd_ref[0])
    lane = lax.broadcasted_iota(jnp.int32, gate_ref.shape, 1)
    gate = jnp.sum(jnp.where(lane == e, gate_ref[...], 0.0), axis=-1, keepdims=True)
    gate = jnp.where(e == N_EXPERTS, 1.0, gate)
    acc_ref[...] += down * gate

    @pl.when(e == pl.num_programs(1) - 1)
    def _():
        y_ref[...] = _layer_norm(DEEPNORM_ALPHA * x_ref[...] + acc_ref[...], g2_ref[...], b2_ref[...])


def _moe(x, gates, wg, wu, wd, g2, b2, tm):
    rows = x.shape[0]
    n_e = wg.shape[0]
    return pl.pallas_call(
        _moe_body,
        grid=(rows // tm, n_e),
        in_specs=[pl.BlockSpec((tm, D_MODEL), lambda i, e: (i, 0)),
                  pl.BlockSpec((tm, GATE_LANES), lambda i, e: (i, 0)),
                  pl.BlockSpec((1, D_MODEL, EXPERT_FF), lambda i, e: (e, 0, 0)),
                  pl.BlockSpec((1, D_MODEL, EXPERT_FF), lambda i, e: (e, 0, 0)),
                  pl.BlockSpec((1, EXPERT_FF, D_MODEL), lambda i, e: (e, 0, 0)),
                  pl.BlockSpec((1, D_MODEL), lambda i, e: (0, 0)),
                  pl.BlockSpec((1, D_MODEL), lambda i, e: (0, 0))],
        out_specs=pl.BlockSpec((tm, D_MODEL), lambda i, e: (i, 0)),
        out_shape=jax.ShapeDtypeStruct((rows, D_MODEL), jnp.float32),
        scratch_shapes=[pltpu.VMEM((tm, D_MODEL), jnp.bfloat16), pltpu.VMEM((tm, D_MODEL), jnp.float32)],
        compiler_params=pltpu.CompilerParams(dimension_semantics=("arbitrary", "arbitrary"),
                                             vmem_limit_bytes=_vmem_limit(48 * 1024 * 1024)),
        name="moe_dense",
    )(x, gates, wg, wu, wd, g2, b2)


def _channel_sublayer(x, w_router_t, bias_col, wg, wu, wd, g2, b2, tm):
    gates = _route(x, w_router_t, bias_col)
    return _moe(x, gates, wg, wu, wd, g2, b2, tm)


def kernel(x_prompt, x_sample, mem_prompt, cache_mem_k, cache_mem_v, state_conv_a, state_conv_b, w_in_a, conv_a_w, conv_a_b, norm_a_g, norm_a_b, w_in_b, conv_b_w, w_kv, w_out, ln1_g, ln1_b, w_router, router_bias, w_gate, w_up, w_down, ws_gate, ws_up, ws_down, ln2_g, ln2_b):
    batch, seq, d = x_prompt.shape
    n_seq, n_pos, _ = x_sample.shape
    c = MIX_WIDTH
    row = lambda a: a.reshape(1, -1)

    hp = x_prompt.reshape(batch * seq, d)
    hs = x_sample.transpose(1, 0, 2).reshape(n_pos * n_seq, d)
    k_all, v_all = _kv_projection(mem_prompt.reshape(batch * N_MEM, d), w_kv)

    conv_a_p, conv_b_p, conv_a_s, conv_b_s = [], [], [], []
    for i in range(DEPTH):
        j = i // N_MIXERS
        is_a = i % N_MIXERS == 0
        if is_a:
            w_in, cw = _bf(w_in_a[j]), conv_a_w[j]
            cb, ng, nb = row(conv_a_b[j]), row(norm_a_g[j]), row(norm_a_b[j])
            hist_s = state_conv_a[j]
        else:
            w_in, cw = _bf(w_in_b[j]), conv_b_w[j]
            cb = ng = nb = jnp.zeros((1, c), jnp.float32)
            hist_s = state_conv_b[j]
        w_o = _bf(w_out[i])
        g1, b1 = row(ln1_g[i]), row(ln1_b[i])

        k_p = k_all[i].reshape(batch, N_MEM, XATTN_WIDTH)
        v_p = v_all[i].reshape(batch, N_MEM, XATTN_WIDTH)
        hp, hist_p_new = _prompt_token_sublayer(is_a, hp, batch, w_in, cw, cb, ng, nb, k_p, v_p, w_o, g1, b1)

        mix, q, hist_s_new = _sample_mix(is_a, n_seq, hs, w_in, hist_s.transpose(1, 0, 2), cw, cb, ng, nb)
        attn = _sample_attention(n_seq, q, cache_mem_k[i].reshape(n_seq, N_MEM, XATTN_WIDTH),
                                 cache_mem_v[i].reshape(n_seq, N_MEM, XATTN_WIDTH))
        hs = _sample_out(hs, mix, attn, w_o, g1, b1)
        hist_s_new = hist_s_new.transpose(1, 0, 2)
        if is_a:
            conv_a_p.append(hist_p_new)
            conv_a_s.append(hist_s_new)
        else:
            conv_b_p.append(hist_p_new)
            conv_b_s.append(hist_s_new)

        w_router_t = _bf(w_router[i].T)
        bias_col = router_bias[i].reshape(N_EXPERTS, 1)
        wg = _bf(jnp.concatenate([w_gate[i], ws_gate[i][None]], axis=0))
        wu = _bf(jnp.concatenate([w_up[i], ws_up[i][None]], axis=0))
        wd = _bf(jnp.concatenate([w_down[i], ws_down[i][None]], axis=0))
        g2, b2 = row(ln2_g[i]), row(ln2_b[i])
        hp = _channel_sublayer(hp, w_router_t, bias_col, wg, wu, wd, g2, b2, MOE_TILE_PROMPT)
        hs = _channel_sublayer(hs, w_router_t, bias_col, wg, wu, wd, g2, b2, MOE_TILE_SAMPLE)

    new_k = k_all.reshape(DEPTH, batch, N_MEM, N_XHEADS, XHEAD_DIM)
    new_v = v_all.reshape(DEPTH, batch, N_MEM, N_XHEADS, XHEAD_DIM)
    return (hp.reshape(batch, seq, d), hs.reshape(n_pos, n_seq, d).transpose(1, 0, 2), new_k, new_v,
            jnp.stack(conv_a_p), jnp.stack(conv_b_p), jnp.stack(conv_a_s), jnp.stack(conv_b_s))
```

```python
import functools

import jax
import jax.numpy as jnp
from jax import lax
from jax.experimental import pallas as pl
from jax.experimental.pallas import tpu as pltpu

D_MODEL = 1024
DEPTH = 2
N_MIXERS = 2
MIX_WIDTH = D_MODEL // 2
N_MEM = 256
N_XHEADS = 4
XHEAD_DIM = MIX_WIDTH // N_XHEADS
XATTN_WIDTH = N_XHEADS * XHEAD_DIM
CONV_A_WIDTH = 31
CONV_B_WIDTH = 3
N_EXPERTS = 64
TOP_K = 8
N_GROUPS = 8
GROUP_SIZE = N_EXPERTS // N_GROUPS
TOPK_GROUPS = 4
EXPERT_FF = D_MODEL // 4
ROUTED_SCALE = 2.5
LN_EPS = 1e-5
DEEPNORM_ALPHA = (2 * DEPTH) ** 0.25

V7X_LANES = 128
V7X_VMEM_BYTES = 64 * 1024 * 1024

HIST_PAD = 32
PROMPT_SEQ_TILE = 512
SAMPLE_BATCH_BLOCK = 8
TOKEN_TILE = 256
SLOT_WINDOW = 64
EXPERT_CHUNK = 8
GATE_LANES = V7X_LANES
SLOT_WIDTH = D_MODEL + GATE_LANES
DENSE_TILE = 512
NEG_INF = float("-inf")


def _vmem_limit(nbytes):
    return int(min(max(nbytes, 16 * 1024 * 1024), V7X_VMEM_BYTES - 8 * 1024 * 1024))


def _bf(x):
    return x.astype(jnp.bfloat16)


def _dot(a, b):
    return jnp.dot(a, b, preferred_element_type=jnp.float32)


def _dot_nt(a, b):
    return lax.dot_general(a, b, (((1,), (1,)), ((), ())), preferred_element_type=jnp.float32)


def _dot_tn(a, b):
    return lax.dot_general(a, b, (((0,), (0,)), ((), ())), preferred_element_type=jnp.float32)


def _sigmoid(x):
    return 1.0 / (1.0 + jnp.exp(-x))


def _silu(x):
    return x * _sigmoid(x)


def _layer_norm(x, g, b):
    mu = jnp.mean(x, axis=-1, keepdims=True)
    xc = x - mu
    var = jnp.mean(xc * xc, axis=-1, keepdims=True)
    return xc * lax.rsqrt(var + LN_EPS) * g + b


def _memory_attention(q, k, v, same_seq=None):
    outs = []
    for h in range(N_XHEADS):
        sl = slice(h * XHEAD_DIM, (h + 1) * XHEAD_DIM)
        s = _dot_nt(_bf(q[:, sl]), _bf(k[:, sl])) * (XHEAD_DIM ** -0.5)
        if same_seq is not None:
            s = jnp.where(same_seq, s, NEG_INF)
        e = jnp.exp(s - jnp.max(s, axis=-1, keepdims=True))
        p = e / jnp.sum(e, axis=-1, keepdims=True)
        outs.append(_dot(_bf(p), _bf(v[:, sl])))
    return jnp.concatenate(outs, axis=-1)


def _full(shape):
    return pl.BlockSpec(shape, lambda *_: tuple(0 for _ in shape))


def _kv_body(mem_ref, w_ref, k_ref, v_ref):
    kv = _dot(_bf(mem_ref[...]), _bf(w_ref[0]))
    k_ref[0] = kv[:, :XATTN_WIDTH]
    v_ref[0] = kv[:, XATTN_WIDTH:]


def _kv_projection(mem2d, w_kv):
    rows = mem2d.shape[0]
    tm = 512
    out = jax.ShapeDtypeStruct((DEPTH, rows, XATTN_WIDTH), jnp.float32)
    return pl.pallas_call(
        _kv_body,
        grid=(DEPTH, rows // tm),
        in_specs=[pl.BlockSpec((tm, D_MODEL), lambda i, m: (m, 0)),
                  pl.BlockSpec((1, D_MODEL, 2 * XATTN_WIDTH), lambda i, m: (i, 0, 0))],
        out_specs=[pl.BlockSpec((1, tm, XATTN_WIDTH), lambda i, m: (i, m, 0)),
                   pl.BlockSpec((1, tm, XATTN_WIDTH), lambda i, m: (i, m, 0))],
        out_shape=[out, out],
        compiler_params=pltpu.CompilerParams(dimension_semantics=("arbitrary", "arbitrary"),
                                             vmem_limit_bytes=_vmem_limit(32 * 1024 * 1024)),
        name="kv_projection",
    )(mem2d, w_kv)


def _prompt_token_body(is_a, x_ref, w_in_ref, cw_ref, cb_ref, ng_ref, nb_ref, k_ref, v_ref, w_out_ref,
                       g1_ref, b1_ref, y_ref, hist_ref, buf_ref):
    c = MIX_WIDTH
    tl = x_ref.shape[0]
    width = CONV_A_WIDTH if is_a else CONV_B_WIDTH
    step = pl.program_id(1)

    @pl.when(step == 0)
    def _():
        buf_ref[pl.ds(0, HIST_PAD), :] = jnp.zeros((HIST_PAD, c), jnp.float32)

    x = x_ref[...]
    u = _dot(_bf(x), w_in_ref[...])
    if is_a:
        conv_in = u[:, :c] * _sigmoid(u[:, c:2 * c])
        q = u[:, 2 * c:]
    else:
        conv_in = u[:, c:2 * c] * u[:, 2 * c:3 * c]
        q = u[:, 3 * c:]
    buf_ref[pl.ds(HIST_PAD, tl), :] = conv_in

    base = HIST_PAD - (width - 1)
    conv = cw_ref[0:1, :] * buf_ref[pl.ds(base, tl), :]
    for t in range(1, width):
        conv = conv + cw_ref[t:t + 1, :] * buf_ref[pl.ds(base + t, tl), :]

    if is_a:
        mix = _silu(_layer_norm(conv + cb_ref[...], ng_ref[...], nb_ref[...]))
    else:
        mix = u[:, :c] * conv

    attn = _memory_attention(q, k_ref[0, 0], v_ref[0, 0])
    out = _dot(_bf(jnp.concatenate([mix, attn], axis=-1)), w_out_ref[...])
    y_ref[...] = _layer_norm(DEEPNORM_ALPHA * x + out, g1_ref[...], b1_ref[...])

    @pl.when(step == pl.num_programs(1) - 1)
    def _():
        hist_ref[0] = buf_ref[pl.ds(HIST_PAD + tl - (width - 1), width - 1), :]

    buf_ref[pl.ds(0, HIST_PAD), :] = buf_ref[pl.ds(tl, HIST_PAD), :]


def _prompt_token_sublayer(is_a, layer, x, total_rows, batch, seq, w_in, cw, cb, ng, nb, k, v, w_out, g1, b1):
    tl = PROMPT_SEQ_TILE
    nl = seq // tl
    c = MIX_WIDTH
    width = CONV_A_WIDTH if is_a else CONV_B_WIDTH
    n_in = w_in.shape[1]
    return pl.pallas_call(
        functools.partial(_prompt_token_body, is_a),
        grid=(batch, nl),
        in_specs=[pl.BlockSpec((tl, D_MODEL), lambda b, l: (b * nl + l, 0)),
                  _full((D_MODEL, n_in)), _full((width, c)), _full((1, c)), _full((1, c)), _full((1, c)),
                  pl.BlockSpec((1, 1, N_MEM, XATTN_WIDTH), lambda b, l: (layer, b, 0, 0)),
                  pl.BlockSpec((1, 1, N_MEM, XATTN_WIDTH), lambda b, l: (layer, b, 0, 0)),
                  _full((c + XATTN_WIDTH, D_MODEL)), _full((1, D_MODEL)), _full((1, D_MODEL))],
        out_specs=[pl.BlockSpec((tl, D_MODEL), lambda b, l: (b * nl + l, 0)),
                   pl.BlockSpec((1, width - 1, c), lambda b, l: (b, 0, 0))],
        out_shape=[jax.ShapeDtypeStruct((total_rows, D_MODEL), jnp.float32),
                   jax.ShapeDtypeStruct((batch, width - 1, c), jnp.float32)],
        scratch_shapes=[pltpu.VMEM((HIST_PAD + tl, c), jnp.float32)],
        compiler_params=pltpu.CompilerParams(dimension_semantics=("arbitrary", "arbitrary"),
                                             vmem_limit_bytes=_vmem_limit(48 * 1024 * 1024)),
        name="prompt_token_a" if is_a else "prompt_token_b",
    )(x, w_in, cw, cb, ng, nb, k, v, w_out, g1, b1)


def _sample_mix_body(is_a, n_seq, x_ref, w_in_ref, hist_ref, cw_ref, cb_ref, ng_ref, nb_ref,
                     mix_ref, q_ref, new_hist_ref):
    c = MIX_WIDTH
    width = CONV_A_WIDTH if is_a else CONV_B_WIDTH
    n_hist = width - 1
    n_pos = x_ref.shape[0] // n_seq
    u = _dot(_bf(x_ref[...]), w_in_ref[...])
    if is_a:
        conv_in = u[:, :c] * _sigmoid(u[:, c:2 * c])
        q_ref[...] = u[:, 2 * c:]
    else:
        conv_in = u[:, c:2 * c] * u[:, 2 * c:3 * c]
        q_ref[...] = u[:, 3 * c:]

    def full_row(j):
        if j < n_hist:
            return hist_ref[j]
        return conv_in[(j - n_hist) * n_seq:(j - n_hist + 1) * n_seq, :]

    for l in range(n_pos):
        conv = cw_ref[0:1, :] * full_row(l)
        for t in range(1, width):
            conv = conv + cw_ref[t:t + 1, :] * full_row(l + t)
        rows = slice(l * n_seq, (l + 1) * n_seq)
        if is_a:
            mix_ref[rows, :] = _silu(_layer_norm(conv + cb_ref[...], ng_ref[...], nb_ref[...]))
        else:
            mix_ref[rows, :] = u[rows, :c] * conv
    for j in range(n_hist):
        new_hist_ref[j] = full_row(j + n_pos)


def _sample_mix(is_a, n_seq, rows, x, x_block, w_in, hist, cw, cb, ng, nb):
    c = MIX_WIDTH
    small = (w_in, hist, cw, cb, ng, nb)
    return pl.pallas_call(
        functools.partial(_sample_mix_body, is_a, n_seq),
        grid=(1,),
        in_specs=[pl.BlockSpec((rows, D_MODEL), lambda i: (x_block, 0))] + [_full(a.shape) for a in small],
        out_specs=[_full((rows, c)), _full((rows, XATTN_WIDTH)), _full(hist.shape)],
        out_shape=[jax.ShapeDtypeStruct((rows, c), jnp.float32),
                   jax.ShapeDtypeStruct((rows, XATTN_WIDTH), jnp.float32),
                   jax.ShapeDtypeStruct(hist.shape, jnp.float32)],
        compiler_params=pltpu.CompilerParams(dimension_semantics=("arbitrary",),
                                             vmem_limit_bytes=_vmem_limit(48 * 1024 * 1024)),
        name="sample_mix_a" if is_a else "sample_mix_b",
    )(x, *small)


def _sample_attn_body(n_seq, q_ref, k_ref, v_ref, o_ref):
    bb = k_ref.shape[1]
    n_pos = q_ref.shape[0] // n_seq
    first = pl.multiple_of(pl.program_id(0) * bb, bb)
    q = jnp.concatenate([q_ref[pl.ds(l * n_seq + first, bb), :] for l in range(n_pos)], axis=0)
    k = k_ref[0].reshape(bb * N_MEM, XATTN_WIDTH)
    v = v_ref[0].reshape(bb * N_MEM, XATTN_WIDTH)
    row_seq = lax.broadcasted_iota(jnp.int32, (n_pos * bb, bb * N_MEM), 0) % bb
    col_seq = lax.broadcasted_iota(jnp.int32, (n_pos * bb, bb * N_MEM), 1) // N_MEM
    o = _memory_attention(q, k, v, same_seq=row_seq == col_seq)
    for l in range(n_pos):
        o_ref[pl.ds(l * n_seq + first, bb), :] = o[l * bb:(l + 1) * bb, :]


def _sample_attention(layer, n_seq, q, mem_k, mem_v):
    rows = q.shape[0]
    bb = SAMPLE_BATCH_BLOCK
    return pl.pallas_call(
        functools.partial(_sample_attn_body, n_seq),
        grid=(n_seq // bb,),
        in_specs=[pl.BlockSpec((rows, XATTN_WIDTH), lambda i: (0, 0)),
                  pl.BlockSpec((1, bb, N_MEM, XATTN_WIDTH), lambda i: (layer, i, 0, 0)),
                  pl.BlockSpec((1, bb, N_MEM, XATTN_WIDTH), lambda i: (layer, i, 0, 0))],
        out_specs=pl.BlockSpec((rows, XATTN_WIDTH), lambda i: (0, 0)),
        out_shape=jax.ShapeDtypeStruct((rows, XATTN_WIDTH), jnp.float32),
        compiler_params=pltpu.CompilerParams(dimension_semantics=("arbitrary",),
                                             vmem_limit_bytes=_vmem_limit(40 * 1024 * 1024)),
        name="sample_attention",
    )(q, mem_k, mem_v)


def _sample_out_body(x_ref, mix_ref, attn_ref, w_out_ref, g1_ref, b1_ref, buf_ref, y_ref):
    del buf_ref
    cat = jnp.concatenate([mix_ref[...], attn_ref[...]], axis=-1)
    out = _dot(_bf(cat), w_out_ref[...])
    y_ref[...] = _layer_norm(DEEPNORM_ALPHA * x_ref[...] + out, g1_ref[...], b1_ref[...])


def _sample_out(rows, x, x_block, mix, attn, w_out, g1, b1, token_buf, out_block):
    small = (mix, attn, w_out, g1, b1)
    return pl.pallas_call(
        _sample_out_body,
        grid=(1,),
        in_specs=[pl.BlockSpec((rows, D_MODEL), lambda i: (x_block, 0))] + [_full(a.shape) for a in small]
                 + [pl.BlockSpec(memory_space=pl.ANY)],
        out_specs=pl.BlockSpec((rows, D_MODEL), lambda i: (out_block, 0)),
        out_shape=jax.ShapeDtypeStruct(token_buf.shape, jnp.float32),
        input_output_aliases={6: 0},
        compiler_params=pltpu.CompilerParams(dimension_semantics=("arbitrary",),
                                             vmem_limit_bytes=_vmem_limit(32 * 1024 * 1024)),
        name="sample_out",
    )(x, *small, token_buf)


def _first_index_of(mask, index, n):
    cand = jnp.where(mask, index, float(n))
    while cand.ndim > 2:
        cand = jnp.min(cand, axis=0)
    return jnp.min(cand, axis=0, keepdims=True)


def _max_all(x):
    while x.ndim > 2:
        x = jnp.max(x, axis=0)
    return jnp.max(x, axis=0, keepdims=True)


def _route_gates(logits, bias):
    tm = logits.shape[1]
    scores = _sigmoid(logits)
    biased = scores + bias
    grp = biased.reshape(N_GROUPS, GROUP_SIZE, tm)

    within = lax.broadcasted_iota(jnp.int32, grp.shape, 1).astype(jnp.float32)
    top1 = jnp.max(grp, axis=1, keepdims=True)
    first = jnp.min(jnp.where(grp == top1, within, float(GROUP_SIZE)), axis=1, keepdims=True)
    top2 = jnp.max(jnp.where(within == first, NEG_INF, grp), axis=1, keepdims=True)
    grp_score = (top1 + top2).reshape(N_GROUPS, tm)

    gidx = lax.broadcasted_iota(jnp.int32, grp_score.shape, 0).astype(jnp.float32)
    grp_sel = jnp.zeros(grp_score.shape, jnp.float32)
    for _ in range(TOPK_GROUPS):
        best = jnp.max(grp_score, axis=0, keepdims=True)
        pick = gidx == _first_index_of(grp_score == best, gidx, N_GROUPS)
        grp_sel = jnp.where(pick, 1.0, grp_sel)
        grp_score = jnp.where(pick, NEG_INF, grp_score)

    eidx = (lax.broadcasted_iota(jnp.int32, grp.shape, 0) * GROUP_SIZE
            + lax.broadcasted_iota(jnp.int32, grp.shape, 1)).astype(jnp.float32)
    cand = jnp.where(grp_sel.reshape(N_GROUPS, 1, tm) > 0.0, grp, NEG_INF)
    chosen = jnp.zeros(grp.shape, jnp.float32)
    for _ in range(TOP_K):
        best = _max_all(cand).reshape(1, 1, tm)
        pick = eidx == _first_index_of(cand == best, eidx, N_EXPERTS).reshape(1, 1, tm)
        chosen = jnp.where(pick, 1.0, chosen)
        cand = jnp.where(pick, NEG_INF, cand)

    w = jnp.where(chosen > 0.0, scores.reshape(grp.shape), 0.0)
    total = jnp.sum(jnp.sum(w, axis=0), axis=0, keepdims=True).reshape(1, 1, tm)
    gates = (w / total * ROUTED_SCALE).reshape(N_EXPERTS, tm)
    return chosen.reshape(N_EXPERTS, tm), gates


def _slot_onehot(sel_ref, first_expert, tm):
    slot = lax.broadcasted_iota(jnp.int32, (SLOT_WINDOW, tm), 0).astype(jnp.float32)
    rows = [jnp.broadcast_to(sel_ref[e:e + 1, :], (SLOT_WINDOW, tm)) == slot
            for e in range(first_expert, first_expert + EXPERT_CHUNK)]
    return jnp.where(jnp.concatenate(rows, axis=0), 1.0, 0.0).astype(jnp.bfloat16)


def _to_token_major(a):
    pad = jnp.zeros((GATE_LANES - a.shape[0], a.shape[1]), jnp.float32)
    return jnp.concatenate([a, pad], axis=0).T


def _route_body(x_ref, wr_ref, bias_ref, xs_ref, sel_ref, over_ref, flag_ref):
    tm = x_ref.shape[0]
    xb = _bf(x_ref[...])
    chosen, gates = _route_gates(_dot_nt(wr_ref[...], xb), bias_ref[...])

    earlier = (lax.broadcasted_iota(jnp.int32, (tm, tm), 0) < lax.broadcasted_iota(jnp.int32, (tm, tm), 1))
    pos = _dot(_bf(chosen), jnp.where(earlier, 1.0, 0.0).astype(jnp.bfloat16))
    routed = chosen > 0.0
    in_window = routed & (pos < float(SLOT_WINDOW))
    sel_ref[...] = jnp.where(in_window, pos, -1.0)

    over = jnp.where(routed & (pos >= float(SLOT_WINDOW)), gates, 0.0)
    over_ref[...] = _to_token_major(over)
    flag_ref[...] = jnp.broadcast_to(jnp.max(jnp.max(over, axis=0, keepdims=True), axis=1, keepdims=True),
                                     flag_ref.shape[1:])[None]

    g_in = jnp.where(in_window, gates, 0.0)
    g_hi = _bf(g_in).astype(jnp.float32)
    g_cols = jnp.concatenate([g_hi, g_in - g_hi], axis=0).T
    x_aug = jnp.concatenate([xb, _bf(g_cols)], axis=1)
    for first in range(0, N_EXPERTS, EXPERT_CHUNK):
        slots = _dot(_slot_onehot(sel_ref, first, tm), x_aug)
        xs_ref[first:first + EXPERT_CHUNK] = _bf(slots).reshape(EXPERT_CHUNK, SLOT_WINDOW, SLOT_WIDTH)


def _route_dispatch(x, w_router_t, bias_col):
    rows = x.shape[0]
    tm = TOKEN_TILE
    n_tiles = rows // tm
    return pl.pallas_call(
        _route_body,
        grid=(n_tiles,),
        in_specs=[pl.BlockSpec((tm, D_MODEL), lambda i: (i, 0)),
                  _full((N_EXPERTS, D_MODEL)), _full((N_EXPERTS, 1))],
        out_specs=[pl.BlockSpec((N_EXPERTS, SLOT_WINDOW, SLOT_WIDTH), lambda i: (0, i, 0)),
                   pl.BlockSpec((N_EXPERTS, tm), lambda i: (0, i)),
                   pl.BlockSpec((tm, GATE_LANES), lambda i: (i, 0)),
                   pl.BlockSpec((1, 8, V7X_LANES), lambda i: (i, 0, 0))],
        out_shape=[jax.ShapeDtypeStruct((N_EXPERTS, n_tiles * SLOT_WINDOW, SLOT_WIDTH), jnp.bfloat16),
                   jax.ShapeDtypeStruct((N_EXPERTS, rows), jnp.float32),
                   jax.ShapeDtypeStruct((rows, GATE_LANES), jnp.float32),
                   jax.ShapeDtypeStruct((n_tiles, 8, V7X_LANES), jnp.float32)],
        compiler_params=pltpu.CompilerParams(dimension_semantics=("arbitrary",),
                                             vmem_limit_bytes=_vmem_limit(48 * 1024 * 1024)),
        name="route_dispatch",
    )(x, w_router_t, bias_col)


def _expert_body(xs_ref, wg_ref, wu_ref, wd_ref, ys_ref, wg_bf, wu_bf, wd_bf):
    e = pl.program_id(0)

    @pl.when(pl.program_id(1) == 0)
    def _():
        wg_bf[...] = _bf(wg_ref[0, 0])
        wu_bf[...] = _bf(wu_ref[0, 0])
        wd_bf[...] = _bf(wd_ref[0, 0])

    xs = xs_ref[0]
    x = xs[:, :D_MODEL]
    g_cols = xs[:, D_MODEL:].astype(jnp.float32)
    lane = lax.broadcasted_iota(jnp.int32, g_cols.shape, 1)
    gate = jnp.sum(jnp.where((lane == e) | (lane == e + N_EXPERTS), g_cols, 0.0), axis=-1, keepdims=True)
    h = _silu(_dot(x, wg_bf[...])) * _dot(x, wu_bf[...])
    ys_ref[0] = _bf(_dot(_bf(h), wd_bf[...]) * gate)


def _experts(layer, xs, w_gate, w_up, w_down):
    n_slots = xs.shape[1]
    n_chunks = 4
    rows = n_slots // n_chunks
    w_in_spec = pl.BlockSpec((1, 1, D_MODEL, EXPERT_FF), lambda e, c: (layer, e, 0, 0))
    return pl.pallas_call(
        _expert_body,
        grid=(N_EXPERTS, n_chunks),
        in_specs=[pl.BlockSpec((1, rows, SLOT_WIDTH), lambda e, c: (e, c, 0)),
                  w_in_spec, w_in_spec,
                  pl.BlockSpec((1, 1, EXPERT_FF, D_MODEL), lambda e, c: (layer, e, 0, 0))],
        out_specs=pl.BlockSpec((1, rows, D_MODEL), lambda e, c: (e, c, 0)),
        out_shape=jax.ShapeDtypeStruct((N_EXPERTS, n_slots, D_MODEL), jnp.bfloat16),
        scratch_shapes=[pltpu.VMEM((D_MODEL, EXPERT_FF), jnp.bfloat16), pltpu.VMEM((D_MODEL, EXPERT_FF), jnp.bfloat16),
                        pltpu.VMEM((EXPERT_FF, D_MODEL), jnp.bfloat16)],
        compiler_params=pltpu.CompilerParams(dimension_semantics=("arbitrary", "arbitrary"),
                                             vmem_limit_bytes=_vmem_limit(40 * 1024 * 1024)),
        name="experts",
    )(xs, w_gate, w_up, w_down)


def _combine_body(has_extra, x_ref, sel_ref, ys_ref, wsg_ref, wsu_ref, wsd_ref, g2_ref, b2_ref, *rest):
    if has_extra:
        extra_ref, y_ref, wsg_bf, wsu_bf, wsd_bf = rest
    else:
        y_ref, wsg_bf, wsu_bf, wsd_bf = rest
    tm = x_ref.shape[0]

    @pl.when(pl.program_id(0) == 0)
    def _():
        wsg_bf[...] = _bf(wsg_ref[0])
        wsu_bf[...] = _bf(wsu_ref[0])
        wsd_bf[...] = _bf(wsd_ref[0])

    x = x_ref[...]
    xb = _bf(x)
    y = _dot(_bf(_silu(_dot(xb, wsg_bf[...])) * _dot(xb, wsu_bf[...])), wsd_bf[...])
    for first in range(0, N_EXPERTS, EXPERT_CHUNK):
        ys = ys_ref[first:first + EXPERT_CHUNK].reshape(EXPERT_CHUNK * SLOT_WINDOW, D_MODEL)
        y = y + _dot_tn(_slot_onehot(sel_ref, first, tm), ys)
    if has_extra:
        y = y + extra_ref[...]
    y_ref[...] = _layer_norm(DEEPNORM_ALPHA * x + y, g2_ref[...], b2_ref[...])


def _combine(layer, x, sel, ys, ws_gate, ws_up, ws_down, g2, b2, extra=None):
    rows = x.shape[0]
    tm = TOKEN_TILE
    has_extra = extra is not None
    in_specs = [pl.BlockSpec((tm, D_MODEL), lambda i: (i, 0)),
                pl.BlockSpec((N_EXPERTS, tm), lambda i: (0, i)),
                pl.BlockSpec((N_EXPERTS, SLOT_WINDOW, D_MODEL), lambda i: (0, i, 0)),
                pl.BlockSpec((1, D_MODEL, EXPERT_FF), lambda i: (layer, 0, 0)),
                pl.BlockSpec((1, D_MODEL, EXPERT_FF), lambda i: (layer, 0, 0)),
                pl.BlockSpec((1, EXPERT_FF, D_MODEL), lambda i: (layer, 0, 0)),
                _full((1, D_MODEL)), _full((1, D_MODEL))]
    args = [x, sel, ys, ws_gate, ws_up, ws_down, g2, b2]
    if has_extra:
        in_specs.append(pl.BlockSpec((tm, D_MODEL), lambda i: (i, 0)))
        args.append(extra)
    return pl.pallas_call(
        functools.partial(_combine_body, has_extra),
        grid=(rows // tm,),
        in_specs=in_specs,
        out_specs=pl.BlockSpec((tm, D_MODEL), lambda i: (i, 0)),
        out_shape=jax.ShapeDtypeStruct((rows, D_MODEL), jnp.float32),
        scratch_shapes=[pltpu.VMEM((D_MODEL, EXPERT_FF), jnp.bfloat16), pltpu.VMEM((D_MODEL, EXPERT_FF), jnp.bfloat16),
                        pltpu.VMEM((EXPERT_FF, D_MODEL), jnp.bfloat16)],
        compiler_params=pltpu.CompilerParams(dimension_semantics=("arbitrary",),
                                             vmem_limit_bytes=_vmem_limit(48 * 1024 * 1024)),
        name="combine_extra" if has_extra else "combine",
    )(*args)


def _dense_body(x_ref, gate_ref, wg_ref, wu_ref, wd_ref, y_ref, xb_ref):
    e = pl.program_id(1)

    @pl.when(e == 0)
    def _():
        xb_ref[...] = _bf(x_ref[...])
        y_ref[...] = jnp.zeros(y_ref.shape, jnp.float32)

    xb = xb_ref[...]
    h = _silu(_dot(xb, _bf(wg_ref[0, 0]))) * _dot(xb, _bf(wu_ref[0, 0]))
    down = _dot(_bf(h), _bf(wd_ref[0, 0]))
    lane = lax.broadcasted_iota(jnp.int32, gate_ref.shape, 1)
    gate = jnp.sum(jnp.where(lane == e, gate_ref[...], 0.0), axis=-1, keepdims=True)
    y_ref[...] += down * gate


def _dense_experts(layer, x, gates, w_gate, w_up, w_down):
    rows = x.shape[0]
    tm = DENSE_TILE
    w_in_spec = pl.BlockSpec((1, 1, D_MODEL, EXPERT_FF), lambda i, e: (layer, e, 0, 0))
    return pl.pallas_call(
        _dense_body,
        grid=(rows // tm, N_EXPERTS),
        in_specs=[pl.BlockSpec((tm, D_MODEL), lambda i, e: (i, 0)),
                  pl.BlockSpec((tm, GATE_LANES), lambda i, e: (i, 0)),
                  w_in_spec, w_in_spec,
                  pl.BlockSpec((1, 1, EXPERT_FF, D_MODEL), lambda i, e: (layer, e, 0, 0))],
        out_specs=pl.BlockSpec((tm, D_MODEL), lambda i, e: (i, 0)),
        out_shape=jax.ShapeDtypeStruct((rows, D_MODEL), jnp.float32),
        scratch_shapes=[pltpu.VMEM((tm, D_MODEL), jnp.bfloat16)],
        compiler_params=pltpu.CompilerParams(dimension_semantics=("arbitrary", "arbitrary"),
                                             vmem_limit_bytes=_vmem_limit(40 * 1024 * 1024)),
        name="dense_overflow",
    )(x, gates, w_gate, w_up, w_down)


def _channel_sublayer(layer, x, w_router_t, bias_col, w_gate, w_up, w_down, ws_gate, ws_up, ws_down, g2, b2):
    xs, sel, over, flags = _route_dispatch(x, w_router_t, bias_col)
    ys = _experts(layer, xs, w_gate, w_up, w_down)
    shared = (ws_gate, ws_up, ws_down, g2, b2)

    def with_overflow():
        extra = _dense_experts(layer, x, over, w_gate, w_up, w_down)
        return _combine(layer, x, sel, ys, *shared, extra=extra)

    def without_overflow():
        return _combine(layer, x, sel, ys, *shared)

    return lax.cond(jnp.max(flags) > 0.0, with_overflow, without_overflow)


def kernel(x_prompt, x_sample, mem_prompt, cache_mem_k, cache_mem_v, state_conv_a, state_conv_b, w_in_a, conv_a_w, conv_a_b, norm_a_g, norm_a_b, w_in_b, conv_b_w, w_kv, w_out, ln1_g, ln1_b, w_router, router_bias, w_gate, w_up, w_down, ws_gate, ws_up, ws_down, ln2_g, ln2_b):
    batch, seq, d = x_prompt.shape
    n_seq, n_pos, _ = x_sample.shape
    c = MIX_WIDTH
    p_rows, s_rows = batch * seq, n_pos * n_seq
    total_rows = p_rows + s_rows
    s_block = p_rows // s_rows
    row = lambda a: a.reshape(1, -1)

    x_p = x_prompt.reshape(p_rows, d)
    x_s, x_s_block = x_sample.transpose(1, 0, 2).reshape(s_rows, d), 0
    k_all, v_all = _kv_projection(mem_prompt.reshape(batch * N_MEM, d), w_kv)
    k_p = k_all.reshape(DEPTH, batch, N_MEM, XATTN_WIDTH)
    v_p = v_all.reshape(DEPTH, batch, N_MEM, XATTN_WIDTH)
    mem_k_s = cache_mem_k.reshape(DEPTH, n_seq, N_MEM, XATTN_WIDTH)
    mem_v_s = cache_mem_v.reshape(DEPTH, n_seq, N_MEM, XATTN_WIDTH)

    conv_a_p, conv_b_p, conv_a_s, conv_b_s = [], [], [], []
    for i in range(DEPTH):
        j = i // N_MIXERS
        is_a = i % N_MIXERS == 0
        if is_a:
            w_in, cw = _bf(w_in_a[j]), conv_a_w[j]
            cb, ng, nb = row(conv_a_b[j]), row(norm_a_g[j]), row(norm_a_b[j])
            hist_s = state_conv_a[j]
        else:
            w_in, cw = _bf(w_in_b[j]), conv_b_w[j]
            cb = ng = nb = jnp.zeros((1, c), jnp.float32)
            hist_s = state_conv_b[j]
        w_o = _bf(w_out[i])
        g1, b1 = row(ln1_g[i]), row(ln1_b[i])

        h, hist_p_new = _prompt_token_sublayer(is_a, i, x_p, total_rows, batch, seq, w_in, cw, cb, ng, nb,
                                               k_p, v_p, w_o, g1, b1)
        mix, q, hist_s_new = _sample_mix(is_a, n_seq, s_rows, x_s, x_s_block, w_in, hist_s.transpose(1, 0, 2),
                                         cw, cb, ng, nb)
        attn = _sample_attention(i, n_seq, q, mem_k_s, mem_v_s)
        h = _sample_out(s_rows, x_s, x_s_block, mix, attn, w_o, g1, b1, h, s_block)
        hist_s_new = hist_s_new.transpose(1, 0, 2)
        if is_a:
            conv_a_p.append(hist_p_new)
            conv_a_s.append(hist_s_new)
        else:
            conv_b_p.append(hist_p_new)
            conv_b_s.append(hist_s_new)

        h = _channel_sublayer(i, h, _bf(w_router[i].T), router_bias[i].reshape(N_EXPERTS, 1),
                              w_gate, w_up, w_down, ws_gate, ws_up, ws_down, row(ln2_g[i]), row(ln2_b[i]))
        x_p = h
        x_s, x_s_block = h, s_block

    new_k = k_all.reshape(DEPTH, batch, N_MEM, N_XHEADS, XHEAD_DIM)
    new_v = v_all.reshape(DEPTH, batch, N_MEM, N_XHEADS, XHEAD_DIM)
    y_p = h[:p_rows].reshape(batch, seq, d)
    y_s = h[p_rows:].reshape(n_pos, n_seq, d).transpose(1, 0, 2)
    return (y_p, y_s, new_k, new_v,
            jnp.stack(conv_a_p), jnp.stack(conv_b_p), jnp.stack(conv_a_s), jnp.stack(conv_b_s))
```

```python
import functools

import jax
import jax.numpy as jnp
from jax import lax
from jax.experimental import pallas as pl
from jax.experimental.pallas import tpu as pltpu

D_MODEL = 1024
DEPTH = 2
N_MIXERS = 2
MIX_WIDTH = D_MODEL // 2
N_MEM = 256
N_XHEADS = 4
XHEAD_DIM = MIX_WIDTH // N_XHEADS
XATTN_WIDTH = N_XHEADS * XHEAD_DIM
CONV_A_WIDTH = 31
CONV_B_WIDTH = 3
N_EXPERTS = 64
TOP_K = 8
N_GROUPS = 8
GROUP_SIZE = N_EXPERTS // N_GROUPS
TOPK_GROUPS = 4
EXPERT_FF = D_MODEL // 4
ROUTED_SCALE = 2.5
LN_EPS = 1e-5
DEEPNORM_ALPHA = (2 * DEPTH) ** 0.25

V7X_LANES = 128
V7X_SUBLANES = 8
V7X_VMEM_BYTES = 64 * 1024 * 1024

HIST_PAD = 32
PROMPT_SEQ_TILE = 512
SAMPLE_BATCH_BLOCK = 8
TOKEN_TILE = 256
SLOT_WINDOW = 64
EXPERT_CHUNK = 8
EXPERT_ROW_CHUNKS = 2
GATE_LANES = V7X_LANES
DENSE_TILE = 512
NEG_INF = float("-inf")


def _vmem_limit(nbytes):
    return int(min(max(nbytes, 16 * 1024 * 1024), V7X_VMEM_BYTES - 8 * 1024 * 1024))


def _bf(x):
    return x.astype(jnp.bfloat16)


def _dot(a, b):
    return jnp.dot(a, b, preferred_element_type=jnp.float32)


def _dot_nt(a, b):
    return lax.dot_general(a, b, (((1,), (1,)), ((), ())), preferred_element_type=jnp.float32)


def _dot_tn(a, b):
    return lax.dot_general(a, b, (((0,), (0,)), ((), ())), preferred_element_type=jnp.float32)


def _sigmoid(x):
    return 1.0 / (1.0 + jnp.exp(-x))


def _silu(x):
    return x * _sigmoid(x)


def _layer_norm(x, g, b):
    mu = jnp.mean(x, axis=-1, keepdims=True)
    xc = x - mu
    var = jnp.mean(xc * xc, axis=-1, keepdims=True)
    return xc * lax.rsqrt(var + LN_EPS) * g + b


def _memory_attention(q, k_head, v_head):
    outs = []
    for h in range(N_XHEADS):
        sl = slice(h * XHEAD_DIM, (h + 1) * XHEAD_DIM)
        s = _dot_nt(_bf(q[:, sl]), _bf(k_head(h))) * (XHEAD_DIM ** -0.5)
        e = jnp.exp(s - jnp.max(s, axis=-1, keepdims=True))
        p = e / jnp.sum(e, axis=-1, keepdims=True)
        outs.append(_dot(_bf(p), _bf(v_head(h))))
    return jnp.concatenate(outs, axis=-1)


def _full(shape):
    return pl.BlockSpec(shape, lambda *_: tuple(0 for _ in shape))


def _kv_body(mem_ref, w_ref, k_ref, v_ref):
    kv = _dot(_bf(mem_ref[...]), _bf(w_ref[0]))
    k_ref[0] = kv[:, :XATTN_WIDTH]
    v_ref[0] = kv[:, XATTN_WIDTH:]


def _kv_projection(mem2d, w_kv):
    rows = mem2d.shape[0]
    tm = 512
    out = jax.ShapeDtypeStruct((DEPTH, rows, XATTN_WIDTH), jnp.float32)
    return pl.pallas_call(
        _kv_body,
        grid=(DEPTH, rows // tm),
        in_specs=[pl.BlockSpec((tm, D_MODEL), lambda i, m: (m, 0)),
                  pl.BlockSpec((1, D_MODEL, 2 * XATTN_WIDTH), lambda i, m: (i, 0, 0))],
        out_specs=[pl.BlockSpec((1, tm, XATTN_WIDTH), lambda i, m: (i, m, 0)),
                   pl.BlockSpec((1, tm, XATTN_WIDTH), lambda i, m: (i, m, 0))],
        out_shape=[out, out],
        compiler_params=pltpu.CompilerParams(dimension_semantics=("arbitrary", "arbitrary"),
                                             vmem_limit_bytes=_vmem_limit(32 * 1024 * 1024)),
        name="kv_projection",
    )(mem2d, w_kv)


def _prompt_token_body(is_a, nl, n_tiles, x_ref, w_in_ref, cw_ref, cb_ref, ng_ref, nb_ref, k_ref, v_ref, w_out_ref,
                       g1_ref, b1_ref, sample_ref, y_ref, hist_ref, buf_ref):
    c = MIX_WIDTH
    tl = x_ref.shape[0]
    width = CONV_A_WIDTH if is_a else CONV_B_WIDTH
    step = pl.program_id(0)
    seq_step = step % nl

    @pl.when(step == n_tiles)
    def _():
        y_ref[...] = sample_ref[...]

    @pl.when((step < n_tiles) & (seq_step == 0))
    def _():
        buf_ref[pl.ds(0, HIST_PAD), :] = jnp.zeros((HIST_PAD, c), jnp.float32)

    @pl.when(step < n_tiles)
    def _():
        x = x_ref[...]
        u = _dot(_bf(x), w_in_ref[...])
        if is_a:
            conv_in = u[:, :c] * _sigmoid(u[:, c:2 * c])
            q = u[:, 2 * c:]
        else:
            conv_in = u[:, c:2 * c] * u[:, 2 * c:3 * c]
            q = u[:, 3 * c:]
        buf_ref[pl.ds(HIST_PAD, tl), :] = conv_in

        base = HIST_PAD - (width - 1)
        conv = None
        for phase in range(V7X_SUBLANES):
            taps = [t for t in range(width) if (base + t) % V7X_SUBLANES == phase]
            if not taps:
                continue
            n = tl if phase == 0 else tl + V7X_SUBLANES
            part = None
            for t in taps:
                term = cw_ref[t:t + 1, :] * buf_ref[pl.ds(base + t - phase, n), :]
                part = term if part is None else part + term
            part = part[phase:phase + tl, :]
            conv = part if conv is None else conv + part

        if is_a:
            mix = _silu(_layer_norm(conv + cb_ref[...], ng_ref[...], nb_ref[...]))
        else:
            mix = u[:, :c] * conv

        head = lambda ref: lambda h: ref[0, 0, :, h * XHEAD_DIM:(h + 1) * XHEAD_DIM]
        attn = _memory_attention(q, head(k_ref), head(v_ref))
        out = _dot(_bf(jnp.concatenate([mix, attn], axis=-1)), w_out_ref[...])
        y_ref[...] = _layer_norm(DEEPNORM_ALPHA * x + out, g1_ref[...], b1_ref[...])

    @pl.when((step < n_tiles) & (seq_step == nl - 1))
    def _():
        hist_ref[0] = buf_ref[pl.ds(HIST_PAD + tl - (width - 1), width - 1), :]

    @pl.when(step < n_tiles)
    def _():
        buf_ref[pl.ds(0, HIST_PAD), :] = buf_ref[pl.ds(tl, HIST_PAD), :]


def _prompt_token_sublayer(is_a, layer, x, sample_rows, batch, seq, w_in, cw, cb, ng, nb, k, v, w_out, g1, b1):
    tl = PROMPT_SEQ_TILE
    assert sample_rows.shape == (tl, D_MODEL)
    nl = seq // tl
    n_tiles = batch * nl
    c = MIX_WIDTH
    width = CONV_A_WIDTH if is_a else CONV_B_WIDTH
    n_in = w_in.shape[1]
    tile = lambda s: jnp.minimum(s, n_tiles - 1)
    mem_spec = pl.BlockSpec((1, 1, N_MEM, XATTN_WIDTH), lambda s: (layer, tile(s) // nl, 0, 0))
    return pl.pallas_call(
        functools.partial(_prompt_token_body, is_a, nl, n_tiles),
        grid=(n_tiles + 1,),
        in_specs=[pl.BlockSpec((tl, D_MODEL), lambda s: (tile(s), 0)),
                  _full((D_MODEL, n_in)), _full((width, c)), _full((1, c)), _full((1, c)), _full((1, c)),
                  mem_spec, mem_spec,
                  _full((c + XATTN_WIDTH, D_MODEL)), _full((1, D_MODEL)), _full((1, D_MODEL)),
                  _full((tl, D_MODEL))],
        out_specs=[pl.BlockSpec((tl, D_MODEL), lambda s: (s, 0)),
                   pl.BlockSpec((1, width - 1, c), lambda s: (tile(s) // nl, 0, 0))],
        out_shape=[jax.ShapeDtypeStruct(((n_tiles + 1) * tl, D_MODEL), jnp.float32),
                   jax.ShapeDtypeStruct((batch, width - 1, c), jnp.float32)],
        scratch_shapes=[pltpu.VMEM((HIST_PAD + tl, c), jnp.float32)],
        compiler_params=pltpu.CompilerParams(dimension_semantics=("arbitrary",),
                                             vmem_limit_bytes=_vmem_limit(48 * 1024 * 1024)),
        name="prompt_token_a" if is_a else "prompt_token_b",
    )(x, w_in, cw, cb, ng, nb, k, v, w_out, g1, b1, sample_rows)


def _sample_mix_body(is_a, n_seq, x_ref, w_in_ref, hist_ref, cw_ref, cb_ref, ng_ref, nb_ref,
                     mix_ref, q_ref, new_hist_ref):
    c = MIX_WIDTH
    width = CONV_A_WIDTH if is_a else CONV_B_WIDTH
    n_hist = width - 1
    n_pos = x_ref.shape[0] // n_seq
    u = _dot(_bf(x_ref[...]), w_in_ref[...])
    if is_a:
        conv_in = u[:, :c] * _sigmoid(u[:, c:2 * c])
        q_ref[...] = u[:, 2 * c:]
    else:
        conv_in = u[:, c:2 * c] * u[:, 2 * c:3 * c]
        q_ref[...] = u[:, 3 * c:]

    def full_row(j):
        if j < n_hist:
            return hist_ref[j]
        return conv_in[(j - n_hist) * n_seq:(j - n_hist + 1) * n_seq, :]

    for l in range(n_pos):
        conv = cw_ref[0:1, :] * full_row(l)
        for t in range(1, width):
            conv = conv + cw_ref[t:t + 1, :] * full_row(l + t)
        rows = slice(l * n_seq, (l + 1) * n_seq)
        if is_a:
            mix_ref[rows, :] = _silu(_layer_norm(conv + cb_ref[...], ng_ref[...], nb_ref[...]))
        else:
            mix_ref[rows, :] = u[rows, :c] * conv
    for j in range(n_hist):
        new_hist_ref[j] = full_row(j + n_pos)


def _sample_mix(is_a, n_seq, rows, x, x_block, w_in, hist, cw, cb, ng, nb):
    c = MIX_WIDTH
    small = (w_in, hist, cw, cb, ng, nb)
    return pl.pallas_call(
        functools.partial(_sample_mix_body, is_a, n_seq),
        grid=(1,),
        in_specs=[pl.BlockSpec((rows, D_MODEL), lambda i: (x_block, 0))] + [_full(a.shape) for a in small],
        out_specs=[_full((rows, c)), _full((rows, XATTN_WIDTH)), _full(hist.shape)],
        out_shape=[jax.ShapeDtypeStruct((rows, c), jnp.float32),
                   jax.ShapeDtypeStruct((rows, XATTN_WIDTH), jnp.float32),
                   jax.ShapeDtypeStruct(hist.shape, jnp.float32)],
        compiler_params=pltpu.CompilerParams(dimension_semantics=("arbitrary",),
                                             vmem_limit_bytes=_vmem_limit(48 * 1024 * 1024)),
        name="sample_mix_a" if is_a else "sample_mix_b",
    )(x, *small)


def _sample_attn_body(n_seq, q_ref, k_ref, v_ref, o_ref):
    bb = k_ref.shape[1]
    n_pos = q_ref.shape[0] // n_seq
    first = pl.multiple_of(pl.program_id(0) * bb, bb)
    q = jnp.concatenate([q_ref[pl.ds(l * n_seq + first, bb), :] for l in range(n_pos)], axis=0)
    nq = n_pos * bb
    n_rows = bb * N_MEM * N_XHEADS
    k_rows = k_ref[0].reshape(n_rows, XHEAD_DIM)
    v_rows = v_ref[0].reshape(n_rows, XHEAD_DIM)
    q_heads = jnp.concatenate([q[:, h * XHEAD_DIM:(h + 1) * XHEAD_DIM] for h in range(N_XHEADS)], axis=0)
    s = _dot_nt(_bf(k_rows), _bf(q_heads)) * (XHEAD_DIM ** -0.5)
    r = lax.broadcasted_iota(jnp.int32, s.shape, 0)
    col = lax.broadcasted_iota(jnp.int32, s.shape, 1)
    valid = ((r % N_XHEADS) == (col // nq)) & ((r // (N_MEM * N_XHEADS)) == (col % bb))
    s = jnp.where(valid, s, NEG_INF)
    e = jnp.exp(s - jnp.max(s, axis=0, keepdims=True))
    p = e / jnp.sum(e, axis=0, keepdims=True)
    o_heads = _dot_tn(_bf(p), _bf(v_rows))
    o = jnp.concatenate([o_heads[h * nq:(h + 1) * nq] for h in range(N_XHEADS)], axis=1)
    for l in range(n_pos):
        o_ref[pl.ds(l * n_seq + first, bb), :] = o[l * bb:(l + 1) * bb, :]


def _sample_attention(layer, n_seq, q, mem_k, mem_v):
    rows = q.shape[0]
    bb = SAMPLE_BATCH_BLOCK
    mem_spec = pl.BlockSpec((1, bb, N_MEM, N_XHEADS, XHEAD_DIM), lambda i: (layer, i, 0, 0, 0))
    return pl.pallas_call(
        functools.partial(_sample_attn_body, n_seq),
        grid=(n_seq // bb,),
        in_specs=[pl.BlockSpec((rows, XATTN_WIDTH), lambda i: (0, 0)), mem_spec, mem_spec],
        out_specs=pl.BlockSpec((rows, XATTN_WIDTH), lambda i: (0, 0)),
        out_shape=jax.ShapeDtypeStruct((rows, XATTN_WIDTH), jnp.float32),
        compiler_params=pltpu.CompilerParams(dimension_semantics=("arbitrary",),
                                             vmem_limit_bytes=_vmem_limit(40 * 1024 * 1024)),
        name="sample_attention",
    )(q, mem_k, mem_v)


def _sample_out_body(x_ref, mix_ref, attn_ref, w_out_ref, g1_ref, b1_ref, y_ref):
    cat = jnp.concatenate([mix_ref[...], attn_ref[...]], axis=-1)
    out = _dot(_bf(cat), w_out_ref[...])
    y_ref[...] = _layer_norm(DEEPNORM_ALPHA * x_ref[...] + out, g1_ref[...], b1_ref[...])


def _sample_out(rows, x, x_block, mix, attn, w_out, g1, b1):
    small = (mix, attn, w_out, g1, b1)
    return pl.pallas_call(
        _sample_out_body,
        grid=(1,),
        in_specs=[pl.BlockSpec((rows, D_MODEL), lambda i: (x_block, 0))] + [_full(a.shape) for a in small],
        out_specs=_full((rows, D_MODEL)),
        out_shape=jax.ShapeDtypeStruct((rows, D_MODEL), jnp.float32),
        compiler_params=pltpu.CompilerParams(dimension_semantics=("arbitrary",),
                                             vmem_limit_bytes=_vmem_limit(32 * 1024 * 1024)),
        name="sample_out",
    )(x, *small)


def _first_index_of(mask, index, n):
    cand = jnp.where(mask, index, float(n))
    while cand.ndim > 2:
        cand = jnp.min(cand, axis=0)
    return jnp.min(cand, axis=0, keepdims=True)


def _max_all(x):
    while x.ndim > 2:
        x = jnp.max(x, axis=0)
    return jnp.max(x, axis=0, keepdims=True)


def _route_gates(logits, bias):
    tm = logits.shape[1]
    scores = _sigmoid(logits)
    biased = scores + bias
    grp = biased.reshape(N_GROUPS, GROUP_SIZE, tm)

    within = lax.broadcasted_iota(jnp.int32, grp.shape, 1).astype(jnp.float32)
    top1 = jnp.max(grp, axis=1, keepdims=True)
    first = jnp.min(jnp.where(grp == top1, within, float(GROUP_SIZE)), axis=1, keepdims=True)
    top2 = jnp.max(jnp.where(within == first, NEG_INF, grp), axis=1, keepdims=True)
    grp_score = (top1 + top2).reshape(N_GROUPS, tm)

    gidx = lax.broadcasted_iota(jnp.int32, grp_score.shape, 0).astype(jnp.float32)
    grp_sel = jnp.zeros(grp_score.shape, jnp.float32)
    for _ in range(TOPK_GROUPS):
        best = jnp.max(grp_score, axis=0, keepdims=True)
        pick = gidx == _first_index_of(grp_score == best, gidx, N_GROUPS)
        grp_sel = jnp.where(pick, 1.0, grp_sel)
        grp_score = jnp.where(pick, NEG_INF, grp_score)

    eidx = (lax.broadcasted_iota(jnp.int32, grp.shape, 0) * GROUP_SIZE
            + lax.broadcasted_iota(jnp.int32, grp.shape, 1)).astype(jnp.float32)
    cand = jnp.where(grp_sel.reshape(N_GROUPS, 1, tm) > 0.0, grp, NEG_INF)
    chosen = jnp.zeros(grp.shape, jnp.float32)
    for _ in range(TOP_K):
        best = _max_all(cand).reshape(1, 1, tm)
        pick = eidx == _first_index_of(cand == best, eidx, N_EXPERTS).reshape(1, 1, tm)
        chosen = jnp.where(pick, 1.0, chosen)
        cand = jnp.where(pick, NEG_INF, cand)

    w = jnp.where(chosen > 0.0, scores.reshape(grp.shape), 0.0)
    total = jnp.sum(jnp.sum(w, axis=0), axis=0, keepdims=True).reshape(1, 1, tm)
    gates = (w / total * ROUTED_SCALE).reshape(N_EXPERTS, tm)
    return chosen.reshape(N_EXPERTS, tm), gates


def _slot_matches(sel_row):
    tm = sel_row.shape[1]
    slot = lax.broadcasted_iota(jnp.int32, (SLOT_WINDOW, tm), 0).astype(jnp.float32)
    return jnp.broadcast_to(sel_row, (SLOT_WINDOW, tm)) == slot


def _slot_onehot(sel_ref, first_expert):
    rows = [_slot_matches(sel_ref[e:e + 1, :]) for e in range(first_expert, first_expert + EXPERT_CHUNK)]
    return jnp.where(jnp.concatenate(rows, axis=0), 1.0, 0.0).astype(jnp.bfloat16)


def _route_body(x_ref, wr_ref, bias_ref, xs_ref, sel_ref, gate_ref, over_ref, flag_ref):
    tm = x_ref.shape[0]
    xb = _bf(x_ref[...])
    chosen, gates = _route_gates(_dot_nt(wr_ref[...], xb), bias_ref[...])

    earlier = (lax.broadcasted_iota(jnp.int32, (tm, tm), 0) < lax.broadcasted_iota(jnp.int32, (tm, tm), 1))
    pos = _dot(_bf(chosen), jnp.where(earlier, 1.0, 0.0).astype(jnp.bfloat16))
    routed = chosen > 0.0
    in_window = routed & (pos < float(SLOT_WINDOW))
    sel_ref[...] = jnp.where(in_window, pos, -1.0)
    gate_ref[...] = jnp.where(in_window, gates, 0.0)

    over = jnp.where(routed & (pos >= float(SLOT_WINDOW)), gates, 0.0)
    pad = jnp.zeros((GATE_LANES - N_EXPERTS, tm), jnp.float32)
    over_ref[...] = jnp.concatenate([over, pad], axis=0).T
    flag_ref[...] = jnp.broadcast_to(jnp.max(jnp.max(over, axis=0, keepdims=True), axis=1, keepdims=True),
                                     flag_ref.shape[1:])[None]

    for first in range(0, N_EXPERTS, EXPERT_CHUNK):
        slots = _dot(_slot_onehot(sel_ref, first), xb)
        xs_ref[first:first + EXPERT_CHUNK] = _bf(slots).reshape(EXPERT_CHUNK, SLOT_WINDOW, D_MODEL)


def _route_dispatch(x, w_router_t, bias_col):
    rows = x.shape[0]
    tm = TOKEN_TILE
    n_tiles = rows // tm
    return pl.pallas_call(
        _route_body,
        grid=(n_tiles,),
        in_specs=[pl.BlockSpec((tm, D_MODEL), lambda i: (i, 0)),
                  _full((N_EXPERTS, D_MODEL)), _full((N_EXPERTS, 1))],
        out_specs=[pl.BlockSpec((N_EXPERTS, SLOT_WINDOW, D_MODEL), lambda i: (0, i, 0)),
                   pl.BlockSpec((N_EXPERTS, tm), lambda i: (0, i)),
                   pl.BlockSpec((N_EXPERTS, tm), lambda i: (0, i)),
                   pl.BlockSpec((tm, GATE_LANES), lambda i: (i, 0)),
                   pl.BlockSpec((1, 8, V7X_LANES), lambda i: (i, 0, 0))],
        out_shape=[jax.ShapeDtypeStruct((N_EXPERTS, n_tiles * SLOT_WINDOW, D_MODEL), jnp.bfloat16),
                   jax.ShapeDtypeStruct((N_EXPERTS, rows), jnp.float32),
                   jax.ShapeDtypeStruct((N_EXPERTS, rows), jnp.float32),
                   jax.ShapeDtypeStruct((rows, GATE_LANES), jnp.float32),
                   jax.ShapeDtypeStruct((n_tiles, 8, V7X_LANES), jnp.float32)],
        compiler_params=pltpu.CompilerParams(dimension_semantics=("arbitrary",),
                                             vmem_limit_bytes=_vmem_limit(48 * 1024 * 1024)),
        name="route_dispatch",
    )(x, w_router_t, bias_col)


def _expert_body(n_tiles, xs_ref, sel_ref, gate_ref, wg_ref, wu_ref, wd_ref, ys_ref, wg_bf, wu_bf, wd_bf):
    @pl.when(pl.program_id(1) == 0)
    def _():
        wg_bf[...] = _bf(wg_ref[0, 0])
        wu_bf[...] = _bf(wu_ref[0, 0])
        wd_bf[...] = _bf(wd_ref[0, 0])

    row = pl.ds(pl.program_id(0) % V7X_SUBLANES, 1)
    tm = sel_ref.shape[1] // n_tiles
    slot_gates = []
    for t in range(n_tiles):
        cols = slice(t * tm, (t + 1) * tm)
        picked = jnp.where(_slot_matches(sel_ref[row, cols]), gate_ref[row, cols], 0.0)
        slot_gates.append(jnp.sum(picked, axis=-1, keepdims=True))
    gate = jnp.concatenate(slot_gates, axis=0)

    x = xs_ref[0]
    h = _silu(_dot(x, wg_bf[...])) * _dot(x, wu_bf[...])
    ys_ref[0] = _bf(_dot(_bf(h), wd_bf[...]) * gate)


def _experts(layer, xs, sel, gate, w_gate, w_up, w_down):
    n_slots = xs.shape[1]
    n_chunks = EXPERT_ROW_CHUNKS
    rows = n_slots // n_chunks
    n_tiles = rows // SLOT_WINDOW
    tokens = sel.shape[1] // n_chunks
    w_in_spec = pl.BlockSpec((1, 1, D_MODEL, EXPERT_FF), lambda e, c: (layer, e, 0, 0))
    route_spec = pl.BlockSpec((V7X_SUBLANES, tokens), lambda e, c: (e // V7X_SUBLANES, c))
    return pl.pallas_call(
        functools.partial(_expert_body, n_tiles),
        grid=(N_EXPERTS, n_chunks),
        in_specs=[pl.BlockSpec((1, rows, D_MODEL), lambda e, c: (e, c, 0)),
                  route_spec, route_spec, w_in_spec, w_in_spec,
                  pl.BlockSpec((1, 1, EXPERT_FF, D_MODEL), lambda e, c: (layer, e, 0, 0))],
        out_specs=pl.BlockSpec((1, rows, D_MODEL), lambda e, c: (e, c, 0)),
        out_shape=jax.ShapeDtypeStruct((N_EXPERTS, n_slots, D_MODEL), jnp.bfloat16),
        scratch_shapes=[pltpu.VMEM((D_MODEL, EXPERT_FF), jnp.bfloat16), pltpu.VMEM((D_MODEL, EXPERT_FF), jnp.bfloat16),
                        pltpu.VMEM((EXPERT_FF, D_MODEL), jnp.bfloat16)],
        compiler_params=pltpu.CompilerParams(dimension_semantics=("arbitrary", "arbitrary"),
                                             vmem_limit_bytes=_vmem_limit(40 * 1024 * 1024)),
        name="experts",
    )(xs, sel, gate, w_gate, w_up, w_down)


def _combine_body(has_extra, n_prompt_tiles, x_ref, sel_ref, ys_ref, wsg_ref, wsu_ref, wsd_ref, g2_ref, b2_ref, *rest):
    rest = list(rest)
    extra_ref = rest.pop(0) if has_extra else None
    out_refs = [rest.pop(0)] if n_prompt_tiles is None else [rest.pop(0), rest.pop(0)]
    wsg_bf, wsu_bf, wsd_bf = rest
    tile = pl.program_id(0)

    @pl.when(tile == 0)
    def _():
        wsg_bf[...] = _bf(wsg_ref[0])
        wsu_bf[...] = _bf(wsu_ref[0])
        wsd_bf[...] = _bf(wsd_ref[0])

    x = x_ref[...]
    xb = _bf(x)
    y = _dot(_bf(_silu(_dot(xb, wsg_bf[...])) * _dot(xb, wsu_bf[...])), wsd_bf[...])
    for first in range(0, N_EXPERTS, EXPERT_CHUNK):
        ys = ys_ref[first:first + EXPERT_CHUNK].reshape(EXPERT_CHUNK * SLOT_WINDOW, D_MODEL)
        y = y + _dot_tn(_slot_onehot(sel_ref, first), ys)
    if has_extra:
        y = y + extra_ref[...]
    y = _layer_norm(DEEPNORM_ALPHA * x + y, g2_ref[...], b2_ref[...])

    if n_prompt_tiles is None:
        out_refs[0][...] = y
    else:
        @pl.when(tile < n_prompt_tiles)
        def _():
            out_refs[0][...] = y

        @pl.when(tile >= n_prompt_tiles)
        def _():
            out_refs[1][...] = y


def _combine(layer, x, sel, ys, ws_gate, ws_up, ws_down, g2, b2, extra=None, prompt_rows=None):
    rows = x.shape[0]
    tm = TOKEN_TILE
    has_extra = extra is not None
    in_specs = [pl.BlockSpec((tm, D_MODEL), lambda i: (i, 0)),
                pl.BlockSpec((N_EXPERTS, tm), lambda i: (0, i)),
                pl.BlockSpec((N_EXPERTS, SLOT_WINDOW, D_MODEL), lambda i: (0, i, 0)),
                pl.BlockSpec((1, D_MODEL, EXPERT_FF), lambda i: (layer, 0, 0)),
                pl.BlockSpec((1, D_MODEL, EXPERT_FF), lambda i: (layer, 0, 0)),
                pl.BlockSpec((1, EXPERT_FF, D_MODEL), lambda i: (layer, 0, 0)),
                _full((1, D_MODEL)), _full((1, D_MODEL))]
    args = [x, sel, ys, ws_gate, ws_up, ws_down, g2, b2]
    if has_extra:
        in_specs.append(pl.BlockSpec((tm, D_MODEL), lambda i: (i, 0)))
        args.append(extra)
    if prompt_rows is None:
        n_prompt_tiles = None
        out_specs = pl.BlockSpec((tm, D_MODEL), lambda i: (i, 0))
        out_shape = jax.ShapeDtypeStruct((rows, D_MODEL), jnp.float32)
    else:
        n_prompt_tiles = prompt_rows // tm
        out_specs = [pl.BlockSpec((tm, D_MODEL), lambda i: (jnp.minimum(i, n_prompt_tiles - 1), 0)),
                     pl.BlockSpec((tm, D_MODEL), lambda i: (jnp.maximum(i - n_prompt_tiles, 0), 0))]
        out_shape = [jax.ShapeDtypeStruct((prompt_rows, D_MODEL), jnp.float32),
                     jax.ShapeDtypeStruct((rows - prompt_rows, D_MODEL), jnp.float32)]
    return pl.pallas_call(
        functools.partial(_combine_body, has_extra, n_prompt_tiles),
        grid=(rows // tm,),
        in_specs=in_specs,
        out_specs=out_specs,
        out_shape=out_shape,
        scratch_shapes=[pltpu.VMEM((D_MODEL, EXPERT_FF), jnp.bfloat16), pltpu.VMEM((D_MODEL, EXPERT_FF), jnp.bfloat16),
                        pltpu.VMEM((EXPERT_FF, D_MODEL), jnp.bfloat16)],
        compiler_params=pltpu.CompilerParams(dimension_semantics=("arbitrary",),
                                             vmem_limit_bytes=_vmem_limit(48 * 1024 * 1024)),
        name="combine_extra" if has_extra else "combine",
    )(*args)


def _dense_body(x_ref, gate_ref, wg_ref, wu_ref, wd_ref, y_ref, xb_ref):
    e = pl.program_id(1)

    @pl.when(e == 0)
    def _():
        xb_ref[...] = _bf(x_ref[...])
        y_ref[...] = jnp.zeros(y_ref.shape, jnp.float32)

    xb = xb_ref[...]
    h = _silu(_dot(xb, _bf(wg_ref[0, 0]))) * _dot(xb, _bf(wu_ref[0, 0]))
    down = _dot(_bf(h), _bf(wd_ref[0, 0]))
    lane = lax.broadcasted_iota(jnp.int32, gate_ref.shape, 1)
    gate = jnp.sum(jnp.where(lane == e, gate_ref[...], 0.0), axis=-1, keepdims=True)
    y_ref[...] += down * gate


def _dense_experts(layer, x, gates, w_gate, w_up, w_down):
    rows = x.shape[0]
    tm = DENSE_TILE
    w_in_spec = pl.BlockSpec((1, 1, D_MODEL, EXPERT_FF), lambda i, e: (layer, e, 0, 0))
    return pl.pallas_call(
        _dense_body,
        grid=(rows // tm, N_EXPERTS),
        in_specs=[pl.BlockSpec((tm, D_MODEL), lambda i, e: (i, 0)),
                  pl.BlockSpec((tm, GATE_LANES), lambda i, e: (i, 0)),
                  w_in_spec, w_in_spec,
                  pl.BlockSpec((1, 1, EXPERT_FF, D_MODEL), lambda i, e: (layer, e, 0, 0))],
        out_specs=pl.BlockSpec((tm, D_MODEL), lambda i, e: (i, 0)),
        out_shape=jax.ShapeDtypeStruct((rows, D_MODEL), jnp.float32),
        scratch_shapes=[pltpu.VMEM((tm, D_MODEL), jnp.bfloat16)],
        compiler_params=pltpu.CompilerParams(dimension_semantics=("arbitrary", "arbitrary"),
                                             vmem_limit_bytes=_vmem_limit(40 * 1024 * 1024)),
        name="dense_overflow",
    )(x, gates, w_gate, w_up, w_down)


def _channel_sublayer(layer, x, w_router_t, bias_col, w_gate, w_up, w_down, ws_gate, ws_up, ws_down, g2, b2,
                      prompt_rows=None):
    xs, sel, gate, over, flags = _route_dispatch(x, w_router_t, bias_col)
    ys = _experts(layer, xs, sel, gate, w_gate, w_up, w_down)
    rest = (x, sel, ys, ws_gate, ws_up, ws_down, g2, b2)

    def with_overflow():
        extra = _dense_experts(layer, x, over, w_gate, w_up, w_down)
        return _combine(layer, *rest, extra=extra, prompt_rows=prompt_rows)

    def without_overflow():
        return _combine(layer, *rest, prompt_rows=prompt_rows)

    return lax.cond(jnp.max(flags) > 0.0, with_overflow, without_overflow)


def kernel(x_prompt, x_sample, mem_prompt, cache_mem_k, cache_mem_v, state_conv_a, state_conv_b, w_in_a, conv_a_w, conv_a_b, norm_a_g, norm_a_b, w_in_b, conv_b_w, w_kv, w_out, ln1_g, ln1_b, w_router, router_bias, w_gate, w_up, w_down, ws_gate, ws_up, ws_down, ln2_g, ln2_b):
    batch, seq, d = x_prompt.shape
    n_seq, n_pos, _ = x_sample.shape
    c = MIX_WIDTH
    p_rows, s_rows = batch * seq, n_pos * n_seq
    s_block = p_rows // s_rows
    row = lambda a: a.reshape(1, -1)

    x_p = x_prompt.reshape(p_rows, d)
    x_s, x_s_block = x_sample.transpose(1, 0, 2).reshape(s_rows, d), 0
    k_all, v_all = _kv_projection(mem_prompt.reshape(batch * N_MEM, d), w_kv)
    k_p = k_all.reshape(DEPTH, batch, N_MEM, XATTN_WIDTH)
    v_p = v_all.reshape(DEPTH, batch, N_MEM, XATTN_WIDTH)

    conv_a_p, conv_b_p, conv_a_s, conv_b_s = [], [], [], []
    for i in range(DEPTH):
        j = i // N_MIXERS
        is_a = i % N_MIXERS == 0
        if is_a:
            w_in, cw = _bf(w_in_a[j]), conv_a_w[j]
            cb, ng, nb = row(conv_a_b[j]), row(norm_a_g[j]), row(norm_a_b[j])
            hist_s = state_conv_a[j]
        else:
            w_in, cw = _bf(w_in_b[j]), conv_b_w[j]
            cb = ng = nb = jnp.zeros((1, c), jnp.float32)
            hist_s = state_conv_b[j]
        w_o = _bf(w_out[i])
        g1, b1 = row(ln1_g[i]), row(ln1_b[i])

        mix, q, hist_s_new = _sample_mix(is_a, n_seq, s_rows, x_s, x_s_block, w_in, hist_s.transpose(1, 0, 2),
                                         cw, cb, ng, nb)
        attn = _sample_attention(i, n_seq, q, cache_mem_k, cache_mem_v)
        h_s = _sample_out(s_rows, x_s, x_s_block, mix, attn, w_o, g1, b1)
        h, hist_p_new = _prompt_token_sublayer(is_a, i, x_p, h_s, batch, seq, w_in, cw, cb, ng, nb,
                                               k_p, v_p, w_o, g1, b1)
        hist_s_new = hist_s_new.transpose(1, 0, 2)
        if is_a:
            conv_a_p.append(hist_p_new)
            conv_a_s.append(hist_s_new)
        else:
            conv_b_p.append(hist_p_new)
            conv_b_s.append(hist_s_new)

        last = i == DEPTH - 1
        h = _channel_sublayer(i, h, _bf(w_router[i].T), router_bias[i].reshape(N_EXPERTS, 1),
                              w_gate, w_up, w_down, ws_gate, ws_up, ws_down, row(ln2_g[i]), row(ln2_b[i]),
                              prompt_rows=p_rows if last else None)
        if not last:
            x_p = h
            x_s, x_s_block = h, s_block

    y_p, y_s = h
    new_k = k_all.reshape(DEPTH, batch, N_MEM, N_XHEADS, XHEAD_DIM)
    new_v = v_all.reshape(DEPTH, batch, N_MEM, N_XHEADS, XHEAD_DIM)
    return (y_p.reshape(batch, seq, d), y_s.reshape(n_pos, n_seq, d).transpose(1, 0, 2), new_k, new_v,
            jnp.stack(conv_a_p), jnp.stack(conv_b_p), jnp.stack(conv_a_s), jnp.stack(conv_b_s))
```

```python
import functools

import jax
import jax.numpy as jnp
from jax import lax
from jax.experimental import pallas as pl
from jax.experimental.pallas import tpu as pltpu

D_MODEL = 1024
DEPTH = 2
N_MIXERS = 2
MIX_WIDTH = D_MODEL // 2
N_MEM = 256
N_XHEADS = 4
XHEAD_DIM = MIX_WIDTH // N_XHEADS
XATTN_WIDTH = N_XHEADS * XHEAD_DIM
CONV_A_WIDTH = 31
CONV_B_WIDTH = 3
N_EXPERTS = 64
TOP_K = 8
N_GROUPS = 8
GROUP_SIZE = N_EXPERTS // N_GROUPS
TOPK_GROUPS = 4
EXPERT_FF = D_MODEL // 4
ROUTED_SCALE = 2.5
LN_EPS = 1e-5
DEEPNORM_ALPHA = (2 * DEPTH) ** 0.25

V7X_LANES = 128
V7X_SUBLANES = 8
V7X_VMEM_BYTES = 64 * 1024 * 1024

HIST_PAD = 32
PROMPT_SEQ_TILE = 512
PROMPT_ROW_GROUPS = 2
SAMPLE_BATCH_BLOCK = 8
TOKEN_TILE = 256
SLOT_WINDOW = 64
SLOT_GROUP = 16
SLOT_GROUPS_ALWAYS = 2
ROUTE_TILES_PER_STEP = 2
EXPERT_CHUNK = 8
EXPERT_ROW_CHUNKS = 2
GATE_LANES = V7X_LANES
DENSE_TILE = 512
NEG_INF = float("-inf")


def _vmem_limit(nbytes):
    return int(min(max(nbytes, 16 * 1024 * 1024), V7X_VMEM_BYTES - 8 * 1024 * 1024))


def _bf(x):
    return x.astype(jnp.bfloat16)


def _dot(a, b):
    return jnp.dot(a, b, preferred_element_type=jnp.float32)


def _dot_nt(a, b):
    return lax.dot_general(a, b, (((1,), (1,)), ((), ())), preferred_element_type=jnp.float32)


def _dot_tn(a, b):
    return lax.dot_general(a, b, (((0,), (0,)), ((), ())), preferred_element_type=jnp.float32)


def _sigmoid(x):
    return 1.0 / (1.0 + jnp.exp(-x))


def _silu(x):
    return x * _sigmoid(x)


def _layer_norm(x, g, b):
    mu = jnp.mean(x, axis=-1, keepdims=True)
    xc = x - mu
    var = jnp.mean(xc * xc, axis=-1, keepdims=True)
    return xc * lax.rsqrt(var + LN_EPS) * g + b


def _memory_attention(q, k_head, v_head):
    outs = []
    for h in range(N_XHEADS):
        sl = slice(h * XHEAD_DIM, (h + 1) * XHEAD_DIM)
        s = _dot_nt(_bf(q[:, sl]), _bf(k_head(h))) * (XHEAD_DIM ** -0.5)
        e = jnp.exp(s - jnp.max(s, axis=-1, keepdims=True))
        p = e / jnp.sum(e, axis=-1, keepdims=True)
        outs.append(_dot(_bf(p), _bf(v_head(h))))
    return jnp.concatenate(outs, axis=-1)


def _full(shape):
    return pl.BlockSpec(shape, lambda *_: tuple(0 for _ in shape))


def _kv_body(mem_ref, w_ref, k_ref, v_ref):
    kv = _dot(_bf(mem_ref[...]), _bf(w_ref[0]))
    k_ref[0] = kv[:, :XATTN_WIDTH]
    v_ref[0] = kv[:, XATTN_WIDTH:]


def _kv_projection(mem2d, w_kv):
    rows = mem2d.shape[0]
    tm = 512
    out = jax.ShapeDtypeStruct((DEPTH, rows, XATTN_WIDTH), jnp.float32)
    return pl.pallas_call(
        _kv_body,
        grid=(DEPTH, rows // tm),
        in_specs=[pl.BlockSpec((tm, D_MODEL), lambda i, m: (m, 0)),
                  pl.BlockSpec((1, D_MODEL, 2 * XATTN_WIDTH), lambda i, m: (i, 0, 0))],
        out_specs=[pl.BlockSpec((1, tm, XATTN_WIDTH), lambda i, m: (i, m, 0)),
                   pl.BlockSpec((1, tm, XATTN_WIDTH), lambda i, m: (i, m, 0))],
        out_shape=[out, out],
        compiler_params=pltpu.CompilerParams(dimension_semantics=("arbitrary", "arbitrary"),
                                             vmem_limit_bytes=_vmem_limit(32 * 1024 * 1024)),
        name="kv_projection",
    )(mem2d, w_kv)


def _prompt_token_body(is_a, nl, n_tiles, x_ref, w_in_ref, cw_ref, cb_ref, ng_ref, nb_ref, k_ref, v_ref, w_out_ref,
                       g1_ref, b1_ref, sample_ref, y_ref, hist_ref, buf_ref):
    c = MIX_WIDTH
    tl = x_ref.shape[0]
    width = CONV_A_WIDTH if is_a else CONV_B_WIDTH
    step = pl.program_id(0)
    seq_step = step % nl

    @pl.when(step == n_tiles)
    def _():
        y_ref[...] = sample_ref[...]

    @pl.when((step < n_tiles) & (seq_step == 0))
    def _():
        buf_ref[pl.ds(0, HIST_PAD), :] = jnp.zeros((HIST_PAD, c), jnp.float32)

    @pl.when(step < n_tiles)
    def _():
        rg = tl // PROMPT_ROW_GROUPS
        groups = [slice(g * rg, (g + 1) * rg) for g in range(PROMPT_ROW_GROUPS)]
        xs = [x_ref[rows, :] for rows in groups]
        us = [_dot(_bf(x), w_in_ref[...]) for x in xs]
        for rows, u in zip(groups, us):
            conv_in = u[:, :c] * _sigmoid(u[:, c:2 * c]) if is_a else u[:, c:2 * c] * u[:, 2 * c:3 * c]
            buf_ref[pl.ds(HIST_PAD + rows.start, rg), :] = conv_in

        head = lambda ref: lambda h: ref[0, 0, :, h * XHEAD_DIM:(h + 1) * XHEAD_DIM]
        for rows, x, u in zip(groups, xs, us):
            base = HIST_PAD - (width - 1) + rows.start
            conv = None
            for phase in range(V7X_SUBLANES):
                taps = [t for t in range(width) if (base + t) % V7X_SUBLANES == phase]
                if not taps:
                    continue
                n = rg if phase == 0 else rg + V7X_SUBLANES
                part = None
                for t in taps:
                    term = cw_ref[t:t + 1, :] * buf_ref[pl.ds(base + t - phase, n), :]
                    part = term if part is None else part + term
                part = part[phase:phase + rg, :]
                conv = part if conv is None else conv + part

            if is_a:
                mix = _silu(_layer_norm(conv + cb_ref[...], ng_ref[...], nb_ref[...]))
                q = u[:, 2 * c:]
            else:
                mix = u[:, :c] * conv
                q = u[:, 3 * c:]
            attn = _memory_attention(q, head(k_ref), head(v_ref))
            out = _dot(_bf(jnp.concatenate([mix, attn], axis=-1)), w_out_ref[...])
            y_ref[rows, :] = _layer_norm(DEEPNORM_ALPHA * x + out, g1_ref[...], b1_ref[...])

    @pl.when((step < n_tiles) & (seq_step == nl - 1))
    def _():
        hist_ref[0] = buf_ref[pl.ds(HIST_PAD + tl - (width - 1), width - 1), :]

    @pl.when(step < n_tiles)
    def _():
        buf_ref[pl.ds(0, HIST_PAD), :] = buf_ref[pl.ds(tl, HIST_PAD), :]


def _prompt_token_sublayer(is_a, layer, x, sample_rows, batch, seq, w_in, cw, cb, ng, nb, k, v, w_out, g1, b1):
    tl = PROMPT_SEQ_TILE
    assert sample_rows.shape == (tl, D_MODEL)
    nl = seq // tl
    n_tiles = batch * nl
    c = MIX_WIDTH
    width = CONV_A_WIDTH if is_a else CONV_B_WIDTH
    n_in = w_in.shape[1]
    tile = lambda s: jnp.minimum(s, n_tiles - 1)
    mem_spec = pl.BlockSpec((1, 1, N_MEM, XATTN_WIDTH), lambda s: (layer, tile(s) // nl, 0, 0))
    return pl.pallas_call(
        functools.partial(_prompt_token_body, is_a, nl, n_tiles),
        grid=(n_tiles + 1,),
        in_specs=[pl.BlockSpec((tl, D_MODEL), lambda s: (tile(s), 0)),
                  _full((D_MODEL, n_in)), _full((width, c)), _full((1, c)), _full((1, c)), _full((1, c)),
                  mem_spec, mem_spec,
                  _full((c + XATTN_WIDTH, D_MODEL)), _full((1, D_MODEL)), _full((1, D_MODEL)),
                  _full((tl, D_MODEL))],
        out_specs=[pl.BlockSpec((tl, D_MODEL), lambda s: (s, 0)),
                   pl.BlockSpec((1, width - 1, c), lambda s: (tile(s) // nl, 0, 0))],
        out_shape=[jax.ShapeDtypeStruct(((n_tiles + 1) * tl, D_MODEL), jnp.float32),
                   jax.ShapeDtypeStruct((batch, width - 1, c), jnp.float32)],
        scratch_shapes=[pltpu.VMEM((HIST_PAD + tl, c), jnp.float32)],
        compiler_params=pltpu.CompilerParams(dimension_semantics=("arbitrary",),
                                             vmem_limit_bytes=_vmem_limit(48 * 1024 * 1024)),
        name="prompt_token_a" if is_a else "prompt_token_b",
    )(x, w_in, cw, cb, ng, nb, k, v, w_out, g1, b1, sample_rows)


def _sample_mix_body(is_a, n_seq, x_ref, w_in_ref, hist_ref, cw_ref, cb_ref, ng_ref, nb_ref,
                     mix_ref, q_ref, new_hist_ref):
    c = MIX_WIDTH
    width = CONV_A_WIDTH if is_a else CONV_B_WIDTH
    n_hist = width - 1
    n_pos = x_ref.shape[0] // n_seq
    u = _dot(_bf(x_ref[...]), w_in_ref[...])
    if is_a:
        conv_in = u[:, :c] * _sigmoid(u[:, c:2 * c])
        q_ref[...] = u[:, 2 * c:]
    else:
        conv_in = u[:, c:2 * c] * u[:, 2 * c:3 * c]
        q_ref[...] = u[:, 3 * c:]

    def full_row(j):
        if j < n_hist:
            return hist_ref[j]
        return conv_in[(j - n_hist) * n_seq:(j - n_hist + 1) * n_seq, :]

    for l in range(n_pos):
        conv = cw_ref[0:1, :] * full_row(l)
        for t in range(1, width):
            conv = conv + cw_ref[t:t + 1, :] * full_row(l + t)
        rows = slice(l * n_seq, (l + 1) * n_seq)
        if is_a:
            mix_ref[rows, :] = _silu(_layer_norm(conv + cb_ref[...], ng_ref[...], nb_ref[...]))
        else:
            mix_ref[rows, :] = u[rows, :c] * conv
    for j in range(n_hist):
        new_hist_ref[j] = full_row(j + n_pos)


def _sample_mix(is_a, n_seq, rows, x, x_block, w_in, hist, cw, cb, ng, nb):
    c = MIX_WIDTH
    small = (w_in, hist, cw, cb, ng, nb)
    return pl.pallas_call(
        functools.partial(_sample_mix_body, is_a, n_seq),
        grid=(1,),
        in_specs=[pl.BlockSpec((rows, D_MODEL), lambda i: (x_block, 0))] + [_full(a.shape) for a in small],
        out_specs=[_full((rows, c)), _full((rows, XATTN_WIDTH)), _full(hist.shape)],
        out_shape=[jax.ShapeDtypeStruct((rows, c), jnp.float32),
                   jax.ShapeDtypeStruct((rows, XATTN_WIDTH), jnp.float32),
                   jax.ShapeDtypeStruct(hist.shape, jnp.float32)],
        compiler_params=pltpu.CompilerParams(dimension_semantics=("arbitrary",),
                                             vmem_limit_bytes=_vmem_limit(48 * 1024 * 1024)),
        name="sample_mix_a" if is_a else "sample_mix_b",
    )(x, *small)


def _sample_attn_body(n_seq, q_ref, k_ref, v_ref, o_ref, bias_ref):
    bb = k_ref.shape[1]
    n_pos = q_ref.shape[0] // n_seq
    first = pl.multiple_of(pl.program_id(0) * bb, bb)

    @pl.when(pl.program_id(0) == 0)
    def _():
        r = lax.broadcasted_iota(jnp.int32, bias_ref.shape, 0)
        col = lax.broadcasted_iota(jnp.int32, bias_ref.shape, 1)
        valid = ((r % N_XHEADS) == (col // (n_pos * bb))) & ((r // (N_MEM * N_XHEADS)) == (col % bb))
        bias_ref[...] = jnp.where(valid, 0.0, NEG_INF)

    q = jnp.concatenate([q_ref[pl.ds(l * n_seq + first, bb), :] for l in range(n_pos)], axis=0)
    nq = n_pos * bb
    n_rows = bb * N_MEM * N_XHEADS
    k_rows = k_ref[0].reshape(n_rows, XHEAD_DIM)
    v_rows = v_ref[0].reshape(n_rows, XHEAD_DIM)
    q_heads = jnp.concatenate([q[:, h * XHEAD_DIM:(h + 1) * XHEAD_DIM] for h in range(N_XHEADS)], axis=0)
    s = _dot_nt(_bf(k_rows), _bf(q_heads)) * (XHEAD_DIM ** -0.5) + bias_ref[...]
    e = jnp.exp(s - jnp.max(s, axis=0, keepdims=True))
    p = e / jnp.sum(e, axis=0, keepdims=True)
    o_heads = _dot_tn(_bf(p), _bf(v_rows))
    o = jnp.concatenate([o_heads[h * nq:(h + 1) * nq] for h in range(N_XHEADS)], axis=1)
    for l in range(n_pos):
        o_ref[pl.ds(l * n_seq + first, bb), :] = o[l * bb:(l + 1) * bb, :]


def _sample_attention(layer, n_seq, q, mem_k, mem_v):
    rows = q.shape[0]
    bb = SAMPLE_BATCH_BLOCK
    mem_spec = pl.BlockSpec((1, bb, N_MEM, N_XHEADS, XHEAD_DIM), lambda i: (layer, i, 0, 0, 0))
    return pl.pallas_call(
        functools.partial(_sample_attn_body, n_seq),
        grid=(n_seq // bb,),
        in_specs=[pl.BlockSpec((rows, XATTN_WIDTH), lambda i: (0, 0)), mem_spec, mem_spec],
        out_specs=pl.BlockSpec((rows, XATTN_WIDTH), lambda i: (0, 0)),
        out_shape=jax.ShapeDtypeStruct((rows, XATTN_WIDTH), jnp.float32),
        scratch_shapes=[pltpu.VMEM((bb * N_MEM * N_XHEADS, N_XHEADS * (rows // n_seq) * bb), jnp.float32)],
        compiler_params=pltpu.CompilerParams(dimension_semantics=("arbitrary",),
                                             vmem_limit_bytes=_vmem_limit(40 * 1024 * 1024)),
        name="sample_attention",
    )(q, mem_k, mem_v)


def _sample_out_body(x_ref, mix_ref, attn_ref, w_out_ref, g1_ref, b1_ref, y_ref):
    cat = jnp.concatenate([mix_ref[...], attn_ref[...]], axis=-1)
    out = _dot(_bf(cat), w_out_ref[...])
    y_ref[...] = _layer_norm(DEEPNORM_ALPHA * x_ref[...] + out, g1_ref[...], b1_ref[...])


def _sample_out(rows, x, x_block, mix, attn, w_out, g1, b1):
    small = (mix, attn, w_out, g1, b1)
    return pl.pallas_call(
        _sample_out_body,
        grid=(1,),
        in_specs=[pl.BlockSpec((rows, D_MODEL), lambda i: (x_block, 0))] + [_full(a.shape) for a in small],
        out_specs=_full((rows, D_MODEL)),
        out_shape=jax.ShapeDtypeStruct((rows, D_MODEL), jnp.float32),
        compiler_params=pltpu.CompilerParams(dimension_semantics=("arbitrary",),
                                             vmem_limit_bytes=_vmem_limit(32 * 1024 * 1024)),
        name="sample_out",
    )(x, *small)


def _first_index_of(mask, index, n):
    cand = jnp.where(mask, index, float(n))
    while cand.ndim > 2:
        cand = jnp.min(cand, axis=0)
    return jnp.min(cand, axis=0, keepdims=True)


def _max_all(x):
    while x.ndim > 2:
        x = jnp.max(x, axis=0)
    return jnp.max(x, axis=0, keepdims=True)


def _route_gates(logits, bias):
    tm = logits.shape[1]
    scores = _sigmoid(logits)
    biased = scores + bias
    grp = biased.reshape(N_GROUPS, GROUP_SIZE, tm)

    within = lax.broadcasted_iota(jnp.int32, grp.shape, 1).astype(jnp.float32)
    top1 = jnp.max(grp, axis=1, keepdims=True)
    first = jnp.min(jnp.where(grp == top1, within, float(GROUP_SIZE)), axis=1, keepdims=True)
    top2 = jnp.max(jnp.where(within == first, NEG_INF, grp), axis=1, keepdims=True)
    grp_score = (top1 + top2).reshape(N_GROUPS, tm)

    gidx = lax.broadcasted_iota(jnp.int32, grp_score.shape, 0).astype(jnp.float32)
    grp_sel = jnp.zeros(grp_score.shape, jnp.float32)
    for _ in range(TOPK_GROUPS):
        best = jnp.max(grp_score, axis=0, keepdims=True)
        pick = gidx == _first_index_of(grp_score == best, gidx, N_GROUPS)
        grp_sel = jnp.where(pick, 1.0, grp_sel)
        grp_score = jnp.where(pick, NEG_INF, grp_score)

    eidx = (lax.broadcasted_iota(jnp.int32, grp.shape, 0) * GROUP_SIZE
            + lax.broadcasted_iota(jnp.int32, grp.shape, 1)).astype(jnp.float32)
    cand = jnp.where(grp_sel.reshape(N_GROUPS, 1, tm) > 0.0, grp, NEG_INF)
    chosen = jnp.zeros(grp.shape, jnp.float32)
    for _ in range(TOP_K):
        best = _max_all(cand).reshape(1, 1, tm)
        pick = eidx == _first_index_of(cand == best, eidx, N_EXPERTS).reshape(1, 1, tm)
        chosen = jnp.where(pick, 1.0, chosen)
        cand = jnp.where(pick, NEG_INF, cand)

    w = jnp.where(chosen > 0.0, scores.reshape(grp.shape), 0.0)
    total = jnp.sum(jnp.sum(w, axis=0), axis=0, keepdims=True).reshape(1, 1, tm)
    gates = (w / total * ROUTED_SCALE).reshape(N_EXPERTS, tm)
    return chosen.reshape(N_EXPERTS, tm), gates


def _slot_matches(sel_row, first_slot=0, n_slots=None):
    tm = sel_row.shape[1]
    n_slots = SLOT_WINDOW if n_slots is None else n_slots
    slot = (lax.broadcasted_iota(jnp.int32, (n_slots, tm), 0) + first_slot).astype(jnp.float32)
    return jnp.broadcast_to(sel_row, (n_slots, tm)) == slot


def _slot_onehot(sel, first_expert):
    rows = [_slot_matches(sel[e:e + 1, :]) for e in range(first_expert, first_expert + EXPERT_CHUNK)]
    return jnp.where(jnp.concatenate(rows, axis=0), 1.0, 0.0).astype(jnp.bfloat16)


def _route_body(x_ref, wr_ref, bias_ref, xs_ref, sel_ref, gate_ref, over_ref, flag_ref):
    tm = TOKEN_TILE
    n_sub = x_ref.shape[0] // tm
    any_over = None
    for t in range(n_sub):
        cols = slice(t * tm, (t + 1) * tm)
        xb = _bf(x_ref[cols, :])
        chosen, gates = _route_gates(_dot_nt(wr_ref[...], xb), bias_ref[...])

        earlier = (lax.broadcasted_iota(jnp.int32, (tm, tm), 0) < lax.broadcasted_iota(jnp.int32, (tm, tm), 1))
        pos = _dot(_bf(chosen), jnp.where(earlier, 1.0, 0.0).astype(jnp.bfloat16))
        routed = chosen > 0.0
        in_window = routed & (pos < float(SLOT_WINDOW))
        sel = jnp.where(in_window, pos, -1.0)
        sel_ref[:, cols] = sel
        gate_ref[:, cols] = jnp.where(in_window, gates, 0.0)

        over = jnp.where(routed & (pos >= float(SLOT_WINDOW)), gates, 0.0)
        pad = jnp.zeros((GATE_LANES - N_EXPERTS, tm), jnp.float32)
        over_ref[cols, :] = jnp.concatenate([over, pad], axis=0).T
        tile_over = jnp.max(jnp.max(over, axis=0, keepdims=True), axis=1, keepdims=True)
        any_over = tile_over if any_over is None else jnp.maximum(any_over, tile_over)

        for first in range(0, N_EXPERTS, EXPERT_CHUNK):
            slots = _bf(_dot(_slot_onehot(sel, first), xb)).reshape(EXPERT_CHUNK, SLOT_WINDOW, D_MODEL)
            xs_ref[first:first + EXPERT_CHUNK, t * SLOT_WINDOW:(t + 1) * SLOT_WINDOW, :] = slots
    flag_ref[...] = jnp.broadcast_to(any_over, flag_ref.shape[1:])[None]


def _route_dispatch(x, w_router_t, bias_col):
    rows = x.shape[0]
    n_sub = ROUTE_TILES_PER_STEP
    tm = TOKEN_TILE * n_sub
    n_tiles = rows // tm
    window = SLOT_WINDOW * n_sub
    return pl.pallas_call(
        _route_body,
        grid=(n_tiles,),
        in_specs=[pl.BlockSpec((tm, D_MODEL), lambda i: (i, 0)),
                  _full((N_EXPERTS, D_MODEL)), _full((N_EXPERTS, 1))],
        out_specs=[pl.BlockSpec((N_EXPERTS, window, D_MODEL), lambda i: (0, i, 0)),
                   pl.BlockSpec((N_EXPERTS, tm), lambda i: (0, i)),
                   pl.BlockSpec((N_EXPERTS, tm), lambda i: (0, i)),
                   pl.BlockSpec((tm, GATE_LANES), lambda i: (i, 0)),
                   pl.BlockSpec((1, 8, V7X_LANES), lambda i: (i, 0, 0))],
        out_shape=[jax.ShapeDtypeStruct((N_EXPERTS, n_tiles * window, D_MODEL), jnp.bfloat16),
                   jax.ShapeDtypeStruct((N_EXPERTS, rows), jnp.float32),
                   jax.ShapeDtypeStruct((N_EXPERTS, rows), jnp.float32),
                   jax.ShapeDtypeStruct((rows, GATE_LANES), jnp.float32),
                   jax.ShapeDtypeStruct((n_tiles, 8, V7X_LANES), jnp.float32)],
        compiler_params=pltpu.CompilerParams(dimension_semantics=("arbitrary",),
                                             vmem_limit_bytes=_vmem_limit(48 * 1024 * 1024)),
        name="route_dispatch",
    )(x, w_router_t, bias_col)


def _expert_body(n_tiles, xs_ref, sel_ref, gate_ref, wg_ref, wu_ref, wd_ref, ys_ref, wg_bf, wu_bf, wd_bf):
    @pl.when(pl.program_id(1) == 0)
    def _():
        wg_bf[...] = _bf(wg_ref[0, 0])
        wu_bf[...] = _bf(wu_ref[0, 0])
        wd_bf[...] = _bf(wd_ref[0, 0])

    row = pl.ds(pl.program_id(0) % V7X_SUBLANES, 1)
    tm = sel_ref.shape[1] // n_tiles
    n_groups = SLOT_WINDOW // SLOT_GROUP
    xs = xs_ref[0].reshape(n_tiles, n_groups, SLOT_GROUP, D_MODEL)

    def run_group(g):
        slot_gates = []
        for t in range(n_tiles):
            cols = slice(t * tm, (t + 1) * tm)
            match = _slot_matches(sel_ref[row, cols], g * SLOT_GROUP, SLOT_GROUP)
            slot_gates.append(jnp.sum(jnp.where(match, gate_ref[row, cols], 0.0), axis=-1, keepdims=True))
        gate = jnp.concatenate(slot_gates, axis=0)
        x = xs[:, g].reshape(n_tiles * SLOT_GROUP, D_MODEL)
        h = _silu(_dot(x, wg_bf[...])) * _dot(x, wu_bf[...])
        y = _bf(_dot(_bf(h), wd_bf[...]) * gate)
        for t in range(n_tiles):
            ys_ref[0, pl.ds(t * SLOT_WINDOW + g * SLOT_GROUP, SLOT_GROUP), :] = y[t * SLOT_GROUP:(t + 1) * SLOT_GROUP]

    n_always = min(SLOT_GROUPS_ALWAYS, n_groups)
    for g in range(n_always):
        run_group(g)

    fullest = jnp.max(sel_ref[row, :])
    for g in range(n_always, n_groups):
        @pl.when(fullest >= float(g * SLOT_GROUP))
        def _():
            run_group(g)

        @pl.when(fullest < float(g * SLOT_GROUP))
        def _():
            for t in range(n_tiles):
                ys_ref[0, pl.ds(t * SLOT_WINDOW + g * SLOT_GROUP, SLOT_GROUP), :] = jnp.zeros(
                    (SLOT_GROUP, D_MODEL), jnp.bfloat16)


def _experts(layer, xs, sel, gate, w_gate, w_up, w_down):
    n_slots = xs.shape[1]
    n_chunks = EXPERT_ROW_CHUNKS
    rows = n_slots // n_chunks
    n_tiles = rows // SLOT_WINDOW
    tokens = sel.shape[1] // n_chunks
    w_in_spec = pl.BlockSpec((1, 1, D_MODEL, EXPERT_FF), lambda e, c: (layer, e, 0, 0))
    route_spec = pl.BlockSpec((V7X_SUBLANES, tokens), lambda e, c: (e // V7X_SUBLANES, c))
    return pl.pallas_call(
        functools.partial(_expert_body, n_tiles),
        grid=(N_EXPERTS, n_chunks),
        in_specs=[pl.BlockSpec((1, rows, D_MODEL), lambda e, c: (e, c, 0)),
                  route_spec, route_spec, w_in_spec, w_in_spec,
                  pl.BlockSpec((1, 1, EXPERT_FF, D_MODEL), lambda e, c: (layer, e, 0, 0))],
        out_specs=pl.BlockSpec((1, rows, D_MODEL), lambda e, c: (e, c, 0)),
        out_shape=jax.ShapeDtypeStruct((N_EXPERTS, n_slots, D_MODEL), jnp.bfloat16),
        scratch_shapes=[pltpu.VMEM((D_MODEL, EXPERT_FF), jnp.bfloat16), pltpu.VMEM((D_MODEL, EXPERT_FF), jnp.bfloat16),
                        pltpu.VMEM((EXPERT_FF, D_MODEL), jnp.bfloat16)],
        compiler_params=pltpu.CompilerParams(dimension_semantics=("arbitrary", "arbitrary"),
                                             vmem_limit_bytes=_vmem_limit(40 * 1024 * 1024)),
        name="experts",
    )(xs, sel, gate, w_gate, w_up, w_down)


def _combine_body(has_extra, n_prompt_tiles, x_ref, sel_ref, ys_ref, wsg_ref, wsu_ref, wsd_ref, g2_ref, b2_ref, *rest):
    rest = list(rest)
    extra_ref = rest.pop(0) if has_extra else None
    out_refs = [rest.pop(0)] if n_prompt_tiles is None else [rest.pop(0), rest.pop(0)]
    wsg_bf, wsu_bf, wsd_bf = rest
    tile = pl.program_id(0)

    @pl.when(tile == 0)
    def _():
        wsg_bf[...] = _bf(wsg_ref[0])
        wsu_bf[...] = _bf(wsu_ref[0])
        wsd_bf[...] = _bf(wsd_ref[0])

    x = x_ref[...]
    xb = _bf(x)
    y = _dot(_bf(_silu(_dot(xb, wsg_bf[...])) * _dot(xb, wsu_bf[...])), wsd_bf[...])
    for first in range(0, N_EXPERTS, EXPERT_CHUNK):
        ys = ys_ref[first:first + EXPERT_CHUNK].reshape(EXPERT_CHUNK * SLOT_WINDOW, D_MODEL)
        y = y + _dot_tn(_slot_onehot(sel_ref, first), ys)
    if has_extra:
        y = y + extra_ref[...]
    y = _layer_norm(DEEPNORM_ALPHA * x + y, g2_ref[...], b2_ref[...])

    if n_prompt_tiles is None:
        out_refs[0][...] = y
    else:
        @pl.when(tile < n_prompt_tiles)
        def _():
            out_refs[0][...] = y

        @pl.when(tile >= n_prompt_tiles)
        def _():
            out_refs[1][...] = y


def _combine(layer, x, sel, ys, ws_gate, ws_up, ws_down, g2, b2, extra=None, prompt_rows=None):
    rows = x.shape[0]
    tm = TOKEN_TILE
    has_extra = extra is not None
    in_specs = [pl.BlockSpec((tm, D_MODEL), lambda i: (i, 0)),
                pl.BlockSpec((N_EXPERTS, tm), lambda i: (0, i)),
                pl.BlockSpec((N_EXPERTS, SLOT_WINDOW, D_MODEL), lambda i: (0, i, 0)),
                pl.BlockSpec((1, D_MODEL, EXPERT_FF), lambda i: (layer, 0, 0)),
                pl.BlockSpec((1, D_MODEL, EXPERT_FF), lambda i: (layer, 0, 0)),
                pl.BlockSpec((1, EXPERT_FF, D_MODEL), lambda i: (layer, 0, 0)),
                _full((1, D_MODEL)), _full((1, D_MODEL))]
    args = [x, sel, ys, ws_gate, ws_up, ws_down, g2, b2]
    if has_extra:
        in_specs.append(pl.BlockSpec((tm, D_MODEL), lambda i: (i, 0)))
        args.append(extra)
    if prompt_rows is None:
        n_prompt_tiles = None
        out_specs = pl.BlockSpec((tm, D_MODEL), lambda i: (i, 0))
        out_shape = jax.ShapeDtypeStruct((rows, D_MODEL), jnp.float32)
    else:
        n_prompt_tiles = prompt_rows // tm
        out_specs = [pl.BlockSpec((tm, D_MODEL), lambda i: (jnp.minimum(i, n_prompt_tiles - 1), 0)),
                     pl.BlockSpec((tm, D_MODEL), lambda i: (jnp.maximum(i - n_prompt_tiles, 0), 0))]
        out_shape = [jax.ShapeDtypeStruct((prompt_rows, D_MODEL), jnp.float32),
                     jax.ShapeDtypeStruct((rows - prompt_rows, D_MODEL), jnp.float32)]
    return pl.pallas_call(
        functools.partial(_combine_body, has_extra, n_prompt_tiles),
        grid=(rows // tm,),
        in_specs=in_specs,
        out_specs=out_specs,
        out_shape=out_shape,
        scratch_shapes=[pltpu.VMEM((D_MODEL, EXPERT_FF), jnp.bfloat16), pltpu.VMEM((D_MODEL, EXPERT_FF), jnp.bfloat16),
                        pltpu.VMEM((EXPERT_FF, D_MODEL), jnp.bfloat16)],
        compiler_params=pltpu.CompilerParams(dimension_semantics=("arbitrary",),
                                             vmem_limit_bytes=_vmem_limit(48 * 1024 * 1024)),
        name="combine_extra" if has_extra else "combine",
    )(*args)


def _dense_body(x_ref, gate_ref, wg_ref, wu_ref, wd_ref, y_ref, xb_ref):
    e = pl.program_id(1)

    @pl.when(e == 0)
    def _():
        xb_ref[...] = _bf(x_ref[...])
        y_ref[...] = jnp.zeros(y_ref.shape, jnp.float32)

    xb = xb_ref[...]
    h = _silu(_dot(xb, _bf(wg_ref[0, 0]))) * _dot(xb, _bf(wu_ref[0, 0]))
    down = _dot(_bf(h), _bf(wd_ref[0, 0]))
    lane = lax.broadcasted_iota(jnp.int32, gate_ref.shape, 1)
    gate = jnp.sum(jnp.where(lane == e, gate_ref[...], 0.0), axis=-1, keepdims=True)
    y_ref[...] += down * gate


def _dense_experts(layer, x, gates, w_gate, w_up, w_down):
    rows = x.shape[0]
    tm = DENSE_TILE
    w_in_spec = pl.BlockSpec((1, 1, D_MODEL, EXPERT_FF), lambda i, e: (layer, e, 0, 0))
    return pl.pallas_call(
        _dense_body,
        grid=(rows // tm, N_EXPERTS),
        in_specs=[pl.BlockSpec((tm, D_MODEL), lambda i, e: (i, 0)),
                  pl.BlockSpec((tm, GATE_LANES), lambda i, e: (i, 0)),
                  w_in_spec, w_in_spec,
                  pl.BlockSpec((1, 1, EXPERT_FF, D_MODEL), lambda i, e: (layer, e, 0, 0))],
        out_specs=pl.BlockSpec((tm, D_MODEL), lambda i, e: (i, 0)),
        out_shape=jax.ShapeDtypeStruct((rows, D_MODEL), jnp.float32),
        scratch_shapes=[pltpu.VMEM((tm, D_MODEL), jnp.bfloat16)],
        compiler_params=pltpu.CompilerParams(dimension_semantics=("arbitrary", "arbitrary"),
                                             vmem_limit_bytes=_vmem_limit(40 * 1024 * 1024)),
        name="dense_overflow",
    )(x, gates, w_gate, w_up, w_down)


def _channel_sublayer(layer, x, w_router_t, bias_col, w_gate, w_up, w_down, ws_gate, ws_up, ws_down, g2, b2,
                      prompt_rows=None):
    xs, sel, gate, over, flags = _route_dispatch(x, w_router_t, bias_col)
    ys = _experts(layer, xs, sel, gate, w_gate, w_up, w_down)
    rest = (x, sel, ys, ws_gate, ws_up, ws_down, g2, b2)

    def with_overflow():
        extra = _dense_experts(layer, x, over, w_gate, w_up, w_down)
        return _combine(layer, *rest, extra=extra, prompt_rows=prompt_rows)

    def without_overflow():
        return _combine(layer, *rest, prompt_rows=prompt_rows)

    return lax.cond(jnp.max(flags) > 0.0, with_overflow, without_overflow)


def kernel(x_prompt, x_sample, mem_prompt, cache_mem_k, cache_mem_v, state_conv_a, state_conv_b, w_in_a, conv_a_w, conv_a_b, norm_a_g, norm_a_b, w_in_b, conv_b_w, w_kv, w_out, ln1_g, ln1_b, w_router, router_bias, w_gate, w_up, w_down, ws_gate, ws_up, ws_down, ln2_g, ln2_b):
    batch, seq, d = x_prompt.shape
    n_seq, n_pos, _ = x_sample.shape
    c = MIX_WIDTH
    p_rows, s_rows = batch * seq, n_pos * n_seq
    s_block = p_rows // s_rows
    row = lambda a: a.reshape(1, -1)

    x_p = x_prompt.reshape(p_rows, d)
    x_s, x_s_block = x_sample.transpose(1, 0, 2).reshape(s_rows, d), 0
    k_all, v_all = _kv_projection(mem_prompt.reshape(batch * N_MEM, d), w_kv)
    k_p = k_all.reshape(DEPTH, batch, N_MEM, XATTN_WIDTH)
    v_p = v_all.reshape(DEPTH, batch, N_MEM, XATTN_WIDTH)

    conv_a_p, conv_b_p, conv_a_s, conv_b_s = [], [], [], []
    for i in range(DEPTH):
        j = i // N_MIXERS
        is_a = i % N_MIXERS == 0
        if is_a:
            w_in, cw = _bf(w_in_a[j]), conv_a_w[j]
            cb, ng, nb = row(conv_a_b[j]), row(norm_a_g[j]), row(norm_a_b[j])
            hist_s = state_conv_a[j]
        else:
            w_in, cw = _bf(w_in_b[j]), conv_b_w[j]
            cb = ng = nb = jnp.zeros((1, c), jnp.float32)
            hist_s = state_conv_b[j]
        w_o = _bf(w_out[i])
        g1, b1 = row(ln1_g[i]), row(ln1_b[i])

        mix, q, hist_s_new = _sample_mix(is_a, n_seq, s_rows, x_s, x_s_block, w_in, hist_s.transpose(1, 0, 2),
                                         cw, cb, ng, nb)
        attn = _sample_attention(i, n_seq, q, cache_mem_k, cache_mem_v)
        h_s = _sample_out(s_rows, x_s, x_s_block, mix, attn, w_o, g1, b1)
        h, hist_p_new = _prompt_token_sublayer(is_a, i, x_p, h_s, batch, seq, w_in, cw, cb, ng, nb,
                                               k_p, v_p, w_o, g1, b1)
        hist_s_new = hist_s_new.transpose(1, 0, 2)
        if is_a:
            conv_a_p.append(hist_p_new)
            conv_a_s.append(hist_s_new)
        else:
            conv_b_p.append(hist_p_new)
            conv_b_s.append(hist_s_new)

        last = i == DEPTH - 1
        h = _channel_sublayer(i, h, _bf(w_router[i].T), router_bias[i].reshape(N_EXPERTS, 1),
                              w_gate, w_up, w_down, ws_gate, ws_up, ws_down, row(ln2_g[i]), row(ln2_b[i]),
                              prompt_rows=p_rows if last else None)
        if not last:
            x_p = h
            x_s, x_s_block = h, s_block

    y_p, y_s = h
    new_k = k_all.reshape(DEPTH, batch, N_MEM, N_XHEADS, XHEAD_DIM)
    new_v = v_all.reshape(DEPTH, batch, N_MEM, N_XHEADS, XHEAD_DIM)
    return (y_p.reshape(batch, seq, d), y_s.reshape(n_pos, n_seq, d).transpose(1, 0, 2), new_k, new_v,
            jnp.stack(conv_a_p), jnp.stack(conv_b_p), jnp.stack(conv_a_s), jnp.stack(conv_b_s))
```

```python
import functools

import jax
import jax.numpy as jnp
from jax import lax
from jax.experimental import pallas as pl
from jax.experimental.pallas import tpu as pltpu

D_MODEL = 1024
DEPTH = 2
N_MIXERS = 2
MIX_WIDTH = D_MODEL // 2
N_MEM = 256
N_XHEADS = 4
XHEAD_DIM = MIX_WIDTH // N_XHEADS
XATTN_WIDTH = N_XHEADS * XHEAD_DIM
CONV_A_WIDTH = 31
CONV_B_WIDTH = 3
N_EXPERTS = 64
TOP_K = 8
N_GROUPS = 8
GROUP_SIZE = N_EXPERTS // N_GROUPS
TOPK_GROUPS = 4
EXPERT_FF = D_MODEL // 4
ROUTED_SCALE = 2.5
LN_EPS = 1e-5
DEEPNORM_ALPHA = (2 * DEPTH) ** 0.25

V7X_LANES = 128
V7X_SUBLANES = 8
V7X_VMEM_BYTES = 64 * 1024 * 1024

HIST_PAD = 32
PROMPT_SEQ_TILE = 512
PROMPT_ROW_GROUPS = 1
SAMPLE_BATCH_BLOCK = 8
TOKEN_TILE = 256
SLOT_WINDOW = 64
SLOT_GROUP = 16
SLOT_GROUPS_ALWAYS = 3
ROUTE_TILES_PER_STEP = 2
EXPERT_CHUNK = 8
EXPERT_ROW_CHUNKS = 2
GATE_LANES = V7X_LANES
DENSE_TILE = 512
NEG_INF = float("-inf")


def _vmem_limit(nbytes):
    return int(min(max(nbytes, 16 * 1024 * 1024), V7X_VMEM_BYTES - 8 * 1024 * 1024))


def _bf(x):
    return x.astype(jnp.bfloat16)


def _dot(a, b):
    return jnp.dot(a, b, preferred_element_type=jnp.float32)


def _dot_nt(a, b):
    return lax.dot_general(a, b, (((1,), (1,)), ((), ())), preferred_element_type=jnp.float32)


def _dot_tn(a, b):
    return lax.dot_general(a, b, (((0,), (0,)), ((), ())), preferred_element_type=jnp.float32)


def _sigmoid(x):
    return 1.0 / (1.0 + jnp.exp(-x))


def _silu(x):
    return x * _sigmoid(x)


def _layer_norm(x, g, b):
    mu = jnp.mean(x, axis=-1, keepdims=True)
    xc = x - mu
    var = jnp.mean(xc * xc, axis=-1, keepdims=True)
    return xc * lax.rsqrt(var + LN_EPS) * g + b


def _memory_attention(q, k_head, v_head):
    outs = []
    for h in range(N_XHEADS):
        sl = slice(h * XHEAD_DIM, (h + 1) * XHEAD_DIM)
        s = _dot_nt(_bf(q[:, sl]), _bf(k_head(h))) * (XHEAD_DIM ** -0.5)
        e = jnp.exp(s - jnp.max(s, axis=-1, keepdims=True))
        p = e / jnp.sum(e, axis=-1, keepdims=True)
        outs.append(_dot(_bf(p), _bf(v_head(h))))
    return jnp.concatenate(outs, axis=-1)


def _full(shape):
    return pl.BlockSpec(shape, lambda *_: tuple(0 for _ in shape))


def _kv_body(mem_ref, w_ref, k_ref, v_ref):
    kv = _dot(_bf(mem_ref[...]), _bf(w_ref[0]))
    k_ref[0] = kv[:, :XATTN_WIDTH]
    v_ref[0] = kv[:, XATTN_WIDTH:]


def _kv_projection(mem2d, w_kv):
    rows = mem2d.shape[0]
    tm = 512
    out = jax.ShapeDtypeStruct((DEPTH, rows, XATTN_WIDTH), jnp.float32)
    return pl.pallas_call(
        _kv_body,
        grid=(DEPTH, rows // tm),
        in_specs=[pl.BlockSpec((tm, D_MODEL), lambda i, m: (m, 0)),
                  pl.BlockSpec((1, D_MODEL, 2 * XATTN_WIDTH), lambda i, m: (i, 0, 0))],
        out_specs=[pl.BlockSpec((1, tm, XATTN_WIDTH), lambda i, m: (i, m, 0)),
                   pl.BlockSpec((1, tm, XATTN_WIDTH), lambda i, m: (i, m, 0))],
        out_shape=[out, out],
        compiler_params=pltpu.CompilerParams(dimension_semantics=("arbitrary", "arbitrary"),
                                             vmem_limit_bytes=_vmem_limit(32 * 1024 * 1024)),
        name="kv_projection",
    )(mem2d, w_kv)


def _prompt_token_body(is_a, nl, n_tiles, x_ref, w_in_ref, cw_ref, cb_ref, ng_ref, nb_ref, k_ref, v_ref, w_out_ref,
                       g1_ref, b1_ref, sample_ref, y_ref, hist_ref, buf_ref):
    c = MIX_WIDTH
    tl = x_ref.shape[0]
    width = CONV_A_WIDTH if is_a else CONV_B_WIDTH
    step = pl.program_id(0)
    seq_step = step % nl

    @pl.when(step == n_tiles)
    def _():
        y_ref[...] = sample_ref[...]

    @pl.when((step < n_tiles) & (seq_step == 0))
    def _():
        buf_ref[pl.ds(0, HIST_PAD), :] = jnp.zeros((HIST_PAD, c), jnp.float32)

    @pl.when(step < n_tiles)
    def _():
        rg = tl // PROMPT_ROW_GROUPS
        groups = [slice(g * rg, (g + 1) * rg) for g in range(PROMPT_ROW_GROUPS)]
        xs = [x_ref[rows, :] for rows in groups]
        us = [_dot(_bf(x), w_in_ref[...]) for x in xs]
        for rows, u in zip(groups, us):
            conv_in = u[:, :c] * _sigmoid(u[:, c:2 * c]) if is_a else u[:, c:2 * c] * u[:, 2 * c:3 * c]
            buf_ref[pl.ds(HIST_PAD + rows.start, rg), :] = conv_in

        head = lambda ref: lambda h: ref[0, 0, :, h * XHEAD_DIM:(h + 1) * XHEAD_DIM]
        for rows, x, u in zip(groups, xs, us):
            base = HIST_PAD - (width - 1) + rows.start
            conv = None
            for phase in range(V7X_SUBLANES):
                taps = [t for t in range(width) if (base + t) % V7X_SUBLANES == phase]
                if not taps:
                    continue
                n = rg if phase == 0 else rg + V7X_SUBLANES
                part = None
                for t in taps:
                    term = cw_ref[t:t + 1, :] * buf_ref[pl.ds(base + t - phase, n), :]
                    part = term if part is None else part + term
                part = part[phase:phase + rg, :]
                conv = part if conv is None else conv + part

            if is_a:
                mix = _silu(_layer_norm(conv + cb_ref[...], ng_ref[...], nb_ref[...]))
                q = u[:, 2 * c:]
            else:
                mix = u[:, :c] * conv
                q = u[:, 3 * c:]
            attn = _memory_attention(q, head(k_ref), head(v_ref))
            out = _dot(_bf(jnp.concatenate([mix, attn], axis=-1)), w_out_ref[...])
            y_ref[rows, :] = _layer_norm(DEEPNORM_ALPHA * x + out, g1_ref[...], b1_ref[...])

    @pl.when((step < n_tiles) & (seq_step == nl - 1))
    def _():
        hist_ref[0] = buf_ref[pl.ds(HIST_PAD + tl - (width - 1), width - 1), :]

    @pl.when(step < n_tiles)
    def _():
        buf_ref[pl.ds(0, HIST_PAD), :] = buf_ref[pl.ds(tl, HIST_PAD), :]


def _prompt_token_sublayer(is_a, layer, x, sample_rows, batch, seq, w_in, cw, cb, ng, nb, k, v, w_out, g1, b1):
    tl = PROMPT_SEQ_TILE
    assert sample_rows.shape == (tl, D_MODEL)
    nl = seq // tl
    n_tiles = batch * nl
    c = MIX_WIDTH
    width = CONV_A_WIDTH if is_a else CONV_B_WIDTH
    n_in = w_in.shape[1]
    tile = lambda s: jnp.minimum(s, n_tiles - 1)
    mem_spec = pl.BlockSpec((1, 1, N_MEM, XATTN_WIDTH), lambda s: (layer, tile(s) // nl, 0, 0))
    return pl.pallas_call(
        functools.partial(_prompt_token_body, is_a, nl, n_tiles),
        grid=(n_tiles + 1,),
        in_specs=[pl.BlockSpec((tl, D_MODEL), lambda s: (tile(s), 0)),
                  _full((D_MODEL, n_in)), _full((width, c)), _full((1, c)), _full((1, c)), _full((1, c)),
                  mem_spec, mem_spec,
                  _full((c + XATTN_WIDTH, D_MODEL)), _full((1, D_MODEL)), _full((1, D_MODEL)),
                  _full((tl, D_MODEL))],
        out_specs=[pl.BlockSpec((tl, D_MODEL), lambda s: (s, 0)),
                   pl.BlockSpec((1, width - 1, c), lambda s: (tile(s) // nl, 0, 0))],
        out_shape=[jax.ShapeDtypeStruct(((n_tiles + 1) * tl, D_MODEL), jnp.float32),
                   jax.ShapeDtypeStruct((batch, width - 1, c), jnp.float32)],
        scratch_shapes=[pltpu.VMEM((HIST_PAD + tl, c), jnp.float32)],
        compiler_params=pltpu.CompilerParams(dimension_semantics=("arbitrary",),
                                             vmem_limit_bytes=_vmem_limit(48 * 1024 * 1024)),
        name="prompt_token_a" if is_a else "prompt_token_b",
    )(x, w_in, cw, cb, ng, nb, k, v, w_out, g1, b1, sample_rows)


def _sample_mix_body(is_a, n_seq, x_ref, w_in_ref, hist_ref, cw_ref, cb_ref, ng_ref, nb_ref,
                     mix_ref, q_ref, new_hist_ref):
    c = MIX_WIDTH
    width = CONV_A_WIDTH if is_a else CONV_B_WIDTH
    n_hist = width - 1
    n_pos = x_ref.shape[0] // n_seq
    u = _dot(_bf(x_ref[...]), w_in_ref[...])
    if is_a:
        conv_in = u[:, :c] * _sigmoid(u[:, c:2 * c])
        q_ref[...] = u[:, 2 * c:]
    else:
        conv_in = u[:, c:2 * c] * u[:, 2 * c:3 * c]
        q_ref[...] = u[:, 3 * c:]

    def full_row(j):
        if j < n_hist:
            return hist_ref[j]
        return conv_in[(j - n_hist) * n_seq:(j - n_hist + 1) * n_seq, :]

    for l in range(n_pos):
        conv = cw_ref[0:1, :] * full_row(l)
        for t in range(1, width):
            conv = conv + cw_ref[t:t + 1, :] * full_row(l + t)
        rows = slice(l * n_seq, (l + 1) * n_seq)
        if is_a:
            mix_ref[rows, :] = _silu(_layer_norm(conv + cb_ref[...], ng_ref[...], nb_ref[...]))
        else:
            mix_ref[rows, :] = u[rows, :c] * conv
    for j in range(n_hist):
        new_hist_ref[j] = full_row(j + n_pos)


def _sample_mix(is_a, n_seq, rows, x, x_block, w_in, hist, cw, cb, ng, nb):
    c = MIX_WIDTH
    small = (w_in, hist, cw, cb, ng, nb)
    return pl.pallas_call(
        functools.partial(_sample_mix_body, is_a, n_seq),
        grid=(1,),
        in_specs=[pl.BlockSpec((rows, D_MODEL), lambda i: (x_block, 0))] + [_full(a.shape) for a in small],
        out_specs=[_full((rows, c)), _full((rows, XATTN_WIDTH)), _full(hist.shape)],
        out_shape=[jax.ShapeDtypeStruct((rows, c), jnp.float32),
                   jax.ShapeDtypeStruct((rows, XATTN_WIDTH), jnp.float32),
                   jax.ShapeDtypeStruct(hist.shape, jnp.float32)],
        compiler_params=pltpu.CompilerParams(dimension_semantics=("arbitrary",),
                                             vmem_limit_bytes=_vmem_limit(48 * 1024 * 1024)),
        name="sample_mix_a" if is_a else "sample_mix_b",
    )(x, *small)


def _sample_attn_body(n_seq, q_ref, k_ref, v_ref, o_ref, bias_ref):
    bb = k_ref.shape[1]
    n_pos = q_ref.shape[0] // n_seq
    first = pl.multiple_of(pl.program_id(0) * bb, bb)

    @pl.when(pl.program_id(0) == 0)
    def _():
        r = lax.broadcasted_iota(jnp.int32, bias_ref.shape, 0)
        col = lax.broadcasted_iota(jnp.int32, bias_ref.shape, 1)
        valid = ((r % N_XHEADS) == (col // (n_pos * bb))) & ((r // (N_MEM * N_XHEADS)) == (col % bb))
        bias_ref[...] = jnp.where(valid, 0.0, NEG_INF)

    q = jnp.concatenate([q_ref[pl.ds(l * n_seq + first, bb), :] for l in range(n_pos)], axis=0)
    nq = n_pos * bb
    n_rows = bb * N_MEM * N_XHEADS
    k_rows = k_ref[0].reshape(n_rows, XHEAD_DIM)
    v_rows = v_ref[0].reshape(n_rows, XHEAD_DIM)
    q_heads = jnp.concatenate([q[:, h * XHEAD_DIM:(h + 1) * XHEAD_DIM] for h in range(N_XHEADS)], axis=0)
    s = _dot_nt(_bf(k_rows), _bf(q_heads)) * (XHEAD_DIM ** -0.5) + bias_ref[...]
    e = jnp.exp(s - jnp.max(s, axis=0, keepdims=True))
    p = e / jnp.sum(e, axis=0, keepdims=True)
    o_heads = _dot_tn(_bf(p), _bf(v_rows))
    o = jnp.concatenate([o_heads[h * nq:(h + 1) * nq] for h in range(N_XHEADS)], axis=1)
    for l in range(n_pos):
        o_ref[pl.ds(l * n_seq + first, bb), :] = o[l * bb:(l + 1) * bb, :]


def _sample_attention(layer, n_seq, q, mem_k, mem_v):
    rows = q.shape[0]
    bb = SAMPLE_BATCH_BLOCK
    mem_spec = pl.BlockSpec((1, bb, N_MEM, N_XHEADS, XHEAD_DIM), lambda i: (layer, i, 0, 0, 0))
    return pl.pallas_call(
        functools.partial(_sample_attn_body, n_seq),
        grid=(n_seq // bb,),
        in_specs=[pl.BlockSpec((rows, XATTN_WIDTH), lambda i: (0, 0)), mem_spec, mem_spec],
        out_specs=pl.BlockSpec((rows, XATTN_WIDTH), lambda i: (0, 0)),
        out_shape=jax.ShapeDtypeStruct((rows, XATTN_WIDTH), jnp.float32),
        scratch_shapes=[pltpu.VMEM((bb * N_MEM * N_XHEADS, N_XHEADS * (rows // n_seq) * bb), jnp.float32)],
        compiler_params=pltpu.CompilerParams(dimension_semantics=("arbitrary",),
                                             vmem_limit_bytes=_vmem_limit(40 * 1024 * 1024)),
        name="sample_attention",
    )(q, mem_k, mem_v)


def _sample_out_body(x_ref, mix_ref, attn_ref, w_out_ref, g1_ref, b1_ref, y_ref):
    cat = jnp.concatenate([mix_ref[...], attn_ref[...]], axis=-1)
    out = _dot(_bf(cat), w_out_ref[...])
    y_ref[...] = _layer_norm(DEEPNORM_ALPHA * x_ref[...] + out, g1_ref[...], b1_ref[...])


def _sample_out(rows, x, x_block, mix, attn, w_out, g1, b1):
    small = (mix, attn, w_out, g1, b1)
    return pl.pallas_call(
        _sample_out_body,
        grid=(1,),
        in_specs=[pl.BlockSpec((rows, D_MODEL), lambda i: (x_block, 0))] + [_full(a.shape) for a in small],
        out_specs=_full((rows, D_MODEL)),
        out_shape=jax.ShapeDtypeStruct((rows, D_MODEL), jnp.float32),
        compiler_params=pltpu.CompilerParams(dimension_semantics=("arbitrary",),
                                             vmem_limit_bytes=_vmem_limit(32 * 1024 * 1024)),
        name="sample_out",
    )(x, *small)


def _first_index_of(mask, index, n):
    cand = jnp.where(mask, index, float(n))
    while cand.ndim > 2:
        cand = jnp.min(cand, axis=0)
    return jnp.min(cand, axis=0, keepdims=True)


def _max_all(x):
    while x.ndim > 2:
        x = jnp.max(x, axis=0)
    return jnp.max(x, axis=0, keepdims=True)


def _route_gates(logits, bias):
    tm = logits.shape[1]
    scores = _sigmoid(logits)
    biased = scores + bias
    grp = biased.reshape(N_GROUPS, GROUP_SIZE, tm)

    within = lax.broadcasted_iota(jnp.int32, grp.shape, 1).astype(jnp.float32)
    top1 = jnp.max(grp, axis=1, keepdims=True)
    first = jnp.min(jnp.where(grp == top1, within, float(GROUP_SIZE)), axis=1, keepdims=True)
    top2 = jnp.max(jnp.where(within == first, NEG_INF, grp), axis=1, keepdims=True)
    grp_score = (top1 + top2).reshape(N_GROUPS, tm)

    gidx = lax.broadcasted_iota(jnp.int32, grp_score.shape, 0).astype(jnp.float32)
    grp_sel = jnp.zeros(grp_score.shape, jnp.float32)
    for _ in range(TOPK_GROUPS):
        best = jnp.max(grp_score, axis=0, keepdims=True)
        pick = gidx == _first_index_of(grp_score == best, gidx, N_GROUPS)
        grp_sel = jnp.where(pick, 1.0, grp_sel)
        grp_score = jnp.where(pick, NEG_INF, grp_score)

    eidx = (lax.broadcasted_iota(jnp.int32, grp.shape, 0) * GROUP_SIZE
            + lax.broadcasted_iota(jnp.int32, grp.shape, 1)).astype(jnp.float32)
    cand = jnp.where(grp_sel.reshape(N_GROUPS, 1, tm) > 0.0, grp, NEG_INF)
    chosen = jnp.zeros(grp.shape, jnp.float32)
    for _ in range(TOP_K):
        best = _max_all(cand).reshape(1, 1, tm)
        pick = eidx == _first_index_of(cand == best, eidx, N_EXPERTS).reshape(1, 1, tm)
        chosen = jnp.where(pick, 1.0, chosen)
        cand = jnp.where(pick, NEG_INF, cand)

    w = jnp.where(chosen > 0.0, scores.reshape(grp.shape), 0.0)
    total = jnp.sum(jnp.sum(w, axis=0), axis=0, keepdims=True).reshape(1, 1, tm)
    gates = (w / total * ROUTED_SCALE).reshape(N_EXPERTS, tm)
    return chosen.reshape(N_EXPERTS, tm), gates


def _slot_matches(sel_row, first_slot=0, n_slots=None):
    tm = sel_row.shape[1]
    n_slots = SLOT_WINDOW if n_slots is None else n_slots
    slot = (lax.broadcasted_iota(jnp.int32, (n_slots, tm), 0) + first_slot).astype(jnp.float32)
    return jnp.broadcast_to(sel_row, (n_slots, tm)) == slot


def _slot_onehot(sel, first_expert):
    rows = [_slot_matches(sel[e:e + 1, :]) for e in range(first_expert, first_expert + EXPERT_CHUNK)]
    return jnp.where(jnp.concatenate(rows, axis=0), 1.0, 0.0).astype(jnp.bfloat16)


def _route_body(x_ref, wr_ref, bias_ref, xs_ref, sel_ref, gate_ref, over_ref, flag_ref):
    tm = TOKEN_TILE
    n_sub = x_ref.shape[0] // tm
    any_over = None
    for t in range(n_sub):
        cols = slice(t * tm, (t + 1) * tm)
        xb = _bf(x_ref[cols, :])
        chosen, gates = _route_gates(_dot_nt(wr_ref[...], xb), bias_ref[...])

        earlier = (lax.broadcasted_iota(jnp.int32, (tm, tm), 0) < lax.broadcasted_iota(jnp.int32, (tm, tm), 1))
        pos = _dot(_bf(chosen), jnp.where(earlier, 1.0, 0.0).astype(jnp.bfloat16))
        routed = chosen > 0.0
        in_window = routed & (pos < float(SLOT_WINDOW))
        sel = jnp.where(in_window, pos, -1.0)
        sel_ref[:, cols] = sel
        gate_ref[:, cols] = jnp.where(in_window, gates, 0.0)

        over = jnp.where(routed & (pos >= float(SLOT_WINDOW)), gates, 0.0)
        pad = jnp.zeros((GATE_LANES - N_EXPERTS, tm), jnp.float32)
        over_ref[cols, :] = jnp.concatenate([over, pad], axis=0).T
        tile_over = jnp.max(jnp.max(over, axis=0, keepdims=True), axis=1, keepdims=True)
        any_over = tile_over if any_over is None else jnp.maximum(any_over, tile_over)

        for first in range(0, N_EXPERTS, EXPERT_CHUNK):
            slots = _bf(_dot(_slot_onehot(sel, first), xb)).reshape(EXPERT_CHUNK, SLOT_WINDOW, D_MODEL)
            xs_ref[first:first + EXPERT_CHUNK, t * SLOT_WINDOW:(t + 1) * SLOT_WINDOW, :] = slots
    flag_ref[...] = jnp.broadcast_to(any_over, flag_ref.shape[1:])[None]


def _route_dispatch(x, w_router_t, bias_col):
    rows = x.shape[0]
    n_sub = ROUTE_TILES_PER_STEP
    tm = TOKEN_TILE * n_sub
    n_tiles = rows // tm
    window = SLOT_WINDOW * n_sub
    return pl.pallas_call(
        _route_body,
        grid=(n_tiles,),
        in_specs=[pl.BlockSpec((tm, D_MODEL), lambda i: (i, 0)),
                  _full((N_EXPERTS, D_MODEL)), _full((N_EXPERTS, 1))],
        out_specs=[pl.BlockSpec((N_EXPERTS, window, D_MODEL), lambda i: (0, i, 0)),
                   pl.BlockSpec((N_EXPERTS, tm), lambda i: (0, i)),
                   pl.BlockSpec((N_EXPERTS, tm), lambda i: (0, i)),
                   pl.BlockSpec((tm, GATE_LANES), lambda i: (i, 0)),
                   pl.BlockSpec((1, 8, V7X_LANES), lambda i: (i, 0, 0))],
        out_shape=[jax.ShapeDtypeStruct((N_EXPERTS, n_tiles * window, D_MODEL), jnp.bfloat16),
                   jax.ShapeDtypeStruct((N_EXPERTS, rows), jnp.float32),
                   jax.ShapeDtypeStruct((N_EXPERTS, rows), jnp.float32),
                   jax.ShapeDtypeStruct((rows, GATE_LANES), jnp.float32),
                   jax.ShapeDtypeStruct((n_tiles, 8, V7X_LANES), jnp.float32)],
        compiler_params=pltpu.CompilerParams(dimension_semantics=("arbitrary",),
                                             vmem_limit_bytes=_vmem_limit(48 * 1024 * 1024)),
        name="route_dispatch",
    )(x, w_router_t, bias_col)


def _expert_body(n_tiles, xs_ref, sel_ref, gate_ref, wg_ref, wu_ref, wd_ref, ys_ref, wg_bf, wu_bf, wd_bf):
    @pl.when(pl.program_id(1) == 0)
    def _():
        wg_bf[...] = _bf(wg_ref[0, 0])
        wu_bf[...] = _bf(wu_ref[0, 0])
        wd_bf[...] = _bf(wd_ref[0, 0])

    row = pl.ds(pl.program_id(0) % V7X_SUBLANES, 1)
    tm = sel_ref.shape[1] // n_tiles
    n_groups = SLOT_WINDOW // SLOT_GROUP
    xs = xs_ref[0].reshape(n_tiles, n_groups, SLOT_GROUP, D_MODEL)

    def run_groups(g0, g1):
        first, n = g0 * SLOT_GROUP, (g1 - g0) * SLOT_GROUP
        slot_gates = []
        for t in range(n_tiles):
            cols = slice(t * tm, (t + 1) * tm)
            match = _slot_matches(sel_ref[row, cols], first, n)
            slot_gates.append(jnp.sum(jnp.where(match, gate_ref[row, cols], 0.0), axis=-1, keepdims=True))
        gate = jnp.concatenate(slot_gates, axis=0)
        x = xs[:, g0:g1].reshape(n_tiles * n, D_MODEL)
        h = _silu(_dot(x, wg_bf[...])) * _dot(x, wu_bf[...])
        y = _bf(_dot(_bf(h), wd_bf[...]) * gate)
        for t in range(n_tiles):
            ys_ref[0, pl.ds(t * SLOT_WINDOW + first, n), :] = y[t * n:(t + 1) * n]

    n_always = min(SLOT_GROUPS_ALWAYS, n_groups)
    run_groups(0, n_always)

    fullest = jnp.max(sel_ref[row, :])
    for g in range(n_always, n_groups):
        @pl.when(fullest >= float(g * SLOT_GROUP))
        def _():
            run_groups(g, g + 1)

        @pl.when(fullest < float(g * SLOT_GROUP))
        def _():
            for t in range(n_tiles):
                ys_ref[0, pl.ds(t * SLOT_WINDOW + g * SLOT_GROUP, SLOT_GROUP), :] = jnp.zeros(
                    (SLOT_GROUP, D_MODEL), jnp.bfloat16)


def _experts(layer, xs, sel, gate, w_gate, w_up, w_down):
    n_slots = xs.shape[1]
    n_chunks = EXPERT_ROW_CHUNKS
    rows = n_slots // n_chunks
    n_tiles = rows // SLOT_WINDOW
    tokens = sel.shape[1] // n_chunks
    w_in_spec = pl.BlockSpec((1, 1, D_MODEL, EXPERT_FF), lambda e, c: (layer, e, 0, 0))
    route_spec = pl.BlockSpec((V7X_SUBLANES, tokens), lambda e, c: (e // V7X_SUBLANES, c))
    return pl.pallas_call(
        functools.partial(_expert_body, n_tiles),
        grid=(N_EXPERTS, n_chunks),
        in_specs=[pl.BlockSpec((1, rows, D_MODEL), lambda e, c: (e, c, 0)),
                  route_spec, route_spec, w_in_spec, w_in_spec,
                  pl.BlockSpec((1, 1, EXPERT_FF, D_MODEL), lambda e, c: (layer, e, 0, 0))],
        out_specs=pl.BlockSpec((1, rows, D_MODEL), lambda e, c: (e, c, 0)),
        out_shape=jax.ShapeDtypeStruct((N_EXPERTS, n_slots, D_MODEL), jnp.bfloat16),
        scratch_shapes=[pltpu.VMEM((D_MODEL, EXPERT_FF), jnp.bfloat16), pltpu.VMEM((D_MODEL, EXPERT_FF), jnp.bfloat16),
                        pltpu.VMEM((EXPERT_FF, D_MODEL), jnp.bfloat16)],
        compiler_params=pltpu.CompilerParams(dimension_semantics=("arbitrary", "arbitrary"),
                                             vmem_limit_bytes=_vmem_limit(40 * 1024 * 1024)),
        name="experts",
    )(xs, sel, gate, w_gate, w_up, w_down)


def _combine_body(has_extra, n_prompt_tiles, x_ref, sel_ref, ys_ref, wsg_ref, wsu_ref, wsd_ref, g2_ref, b2_ref, *rest):
    rest = list(rest)
    extra_ref = rest.pop(0) if has_extra else None
    out_refs = [rest.pop(0)] if n_prompt_tiles is None else [rest.pop(0), rest.pop(0)]
    wsg_bf, wsu_bf, wsd_bf = rest
    tile = pl.program_id(0)

    @pl.when(tile == 0)
    def _():
        wsg_bf[...] = _bf(wsg_ref[0])
        wsu_bf[...] = _bf(wsu_ref[0])
        wsd_bf[...] = _bf(wsd_ref[0])

    x = x_ref[...]
    xb = _bf(x)
    y = _dot(_bf(_silu(_dot(xb, wsg_bf[...])) * _dot(xb, wsu_bf[...])), wsd_bf[...])
    for first in range(0, N_EXPERTS, EXPERT_CHUNK):
        ys = ys_ref[first:first + EXPERT_CHUNK].reshape(EXPERT_CHUNK * SLOT_WINDOW, D_MODEL)
        y = y + _dot_tn(_slot_onehot(sel_ref, first), ys)
    if has_extra:
        y = y + extra_ref[...]
    y = _layer_norm(DEEPNORM_ALPHA * x + y, g2_ref[...], b2_ref[...])

    if n_prompt_tiles is None:
        out_refs[0][...] = y
    else:
        @pl.when(tile < n_prompt_tiles)
        def _():
            out_refs[0][...] = y

        @pl.when(tile >= n_prompt_tiles)
        def _():
            out_refs[1][...] = y


def _combine(layer, x, sel, ys, ws_gate, ws_up, ws_down, g2, b2, extra=None, prompt_rows=None):
    rows = x.shape[0]
    tm = TOKEN_TILE
    has_extra = extra is not None
    in_specs = [pl.BlockSpec((tm, D_MODEL), lambda i: (i, 0)),
                pl.BlockSpec((N_EXPERTS, tm), lambda i: (0, i)),
                pl.BlockSpec((N_EXPERTS, SLOT_WINDOW, D_MODEL), lambda i: (0, i, 0)),
                pl.BlockSpec((1, D_MODEL, EXPERT_FF), lambda i: (layer, 0, 0)),
                pl.BlockSpec((1, D_MODEL, EXPERT_FF), lambda i: (layer, 0, 0)),
                pl.BlockSpec((1, EXPERT_FF, D_MODEL), lambda i: (layer, 0, 0)),
                _full((1, D_MODEL)), _full((1, D_MODEL))]
    args = [x, sel, ys, ws_gate, ws_up, ws_down, g2, b2]
    if has_extra:
        in_specs.append(pl.BlockSpec((tm, D_MODEL), lambda i: (i, 0)))
        args.append(extra)
    if prompt_rows is None:
        n_prompt_tiles = None
        out_specs = pl.BlockSpec((tm, D_MODEL), lambda i: (i, 0))
        out_shape = jax.ShapeDtypeStruct((rows, D_MODEL), jnp.float32)
    else:
        n_prompt_tiles = prompt_rows // tm
        out_specs = [pl.BlockSpec((tm, D_MODEL), lambda i: (jnp.minimum(i, n_prompt_tiles - 1), 0)),
                     pl.BlockSpec((tm, D_MODEL), lambda i: (jnp.maximum(i - n_prompt_tiles, 0), 0))]
        out_shape = [jax.ShapeDtypeStruct((prompt_rows, D_MODEL), jnp.float32),
                     jax.ShapeDtypeStruct((rows - prompt_rows, D_MODEL), jnp.float32)]
    return pl.pallas_call(
        functools.partial(_combine_body, has_extra, n_prompt_tiles),
        grid=(rows // tm,),
        in_specs=in_specs,
        out_specs=out_specs,
        out_shape=out_shape,
        scratch_shapes=[pltpu.VMEM((D_MODEL, EXPERT_FF), jnp.bfloat16), pltpu.VMEM((D_MODEL, EXPERT_FF), jnp.bfloat16),
                        pltpu.VMEM((EXPERT_FF, D_MODEL), jnp.bfloat16)],
        compiler_params=pltpu.CompilerParams(dimension_semantics=("arbitrary",),
                                             vmem_limit_bytes=_vmem_limit(48 * 1024 * 1024)),
        name="combine_extra" if has_extra else "combine",
    )(*args)


def _dense_body(x_ref, gate_ref, wg_ref, wu_ref, wd_ref, y_ref, xb_ref):
    e = pl.program_id(1)

    @pl.when(e == 0)
    def _():
        xb_ref[...] = _bf(x_ref[...])
        y_ref[...] = jnp.zeros(y_ref.shape, jnp.float32)

    xb = xb_ref[...]
    h = _silu(_dot(xb, _bf(wg_ref[0, 0]))) * _dot(xb, _bf(wu_ref[0, 0]))
    down = _dot(_bf(h), _bf(wd_ref[0, 0]))
    lane = lax.broadcasted_iota(jnp.int32, gate_ref.shape, 1)
    gate = jnp.sum(jnp.where(lane == e, gate_ref[...], 0.0), axis=-1, keepdims=True)
    y_ref[...] += down * gate


def _dense_experts(layer, x, gates, w_gate, w_up, w_down):
    rows = x.shape[0]
    tm = DENSE_TILE
    w_in_spec = pl.BlockSpec((1, 1, D_MODEL, EXPERT_FF), lambda i, e: (layer, e, 0, 0))
    return pl.pallas_call(
        _dense_body,
        grid=(rows // tm, N_EXPERTS),
        in_specs=[pl.BlockSpec((tm, D_MODEL), lambda i, e: (i, 0)),
                  pl.BlockSpec((tm, GATE_LANES), lambda i, e: (i, 0)),
                  w_in_spec, w_in_spec,
                  pl.BlockSpec((1, 1, EXPERT_FF, D_MODEL), lambda i, e: (layer, e, 0, 0))],
        out_specs=pl.BlockSpec((tm, D_MODEL), lambda i, e: (i, 0)),
        out_shape=jax.ShapeDtypeStruct((rows, D_MODEL), jnp.float32),
        scratch_shapes=[pltpu.VMEM((tm, D_MODEL), jnp.bfloat16)],
        compiler_params=pltpu.CompilerParams(dimension_semantics=("arbitrary", "arbitrary"),
                                             vmem_limit_bytes=_vmem_limit(40 * 1024 * 1024)),
        name="dense_overflow",
    )(x, gates, w_gate, w_up, w_down)


def _channel_sublayer(layer, x, w_router_t, bias_col, w_gate, w_up, w_down, ws_gate, ws_up, ws_down, g2, b2,
                      prompt_rows=None):
    xs, sel, gate, over, flags = _route_dispatch(x, w_router_t, bias_col)
    ys = _experts(layer, xs, sel, gate, w_gate, w_up, w_down)
    rest = (x, sel, ys, ws_gate, ws_up, ws_down, g2, b2)

    def with_overflow():
        extra = _dense_experts(layer, x, over, w_gate, w_up, w_down)
        return _combine(layer, *rest, extra=extra, prompt_rows=prompt_rows)

    def without_overflow():
        return _combine(layer, *rest, prompt_rows=prompt_rows)

    return lax.cond(jnp.max(flags) > 0.0, with_overflow, without_overflow)


def kernel(x_prompt, x_sample, mem_prompt, cache_mem_k, cache_mem_v, state_conv_a, state_conv_b, w_in_a, conv_a_w, conv_a_b, norm_a_g, norm_a_b, w_in_b, conv_b_w, w_kv, w_out, ln1_g, ln1_b, w_router, router_bias, w_gate, w_up, w_down, ws_gate, ws_up, ws_down, ln2_g, ln2_b):
    batch, seq, d = x_prompt.shape
    n_seq, n_pos, _ = x_sample.shape
    c = MIX_WIDTH
    p_rows, s_rows = batch * seq, n_pos * n_seq
    s_block = p_rows // s_rows
    row = lambda a: a.reshape(1, -1)

    x_p = x_prompt.reshape(p_rows, d)
    x_s, x_s_block = x_sample.transpose(1, 0, 2).reshape(s_rows, d), 0
    k_all, v_all = _kv_projection(mem_prompt.reshape(batch * N_MEM, d), w_kv)
    k_p = k_all.reshape(DEPTH, batch, N_MEM, XATTN_WIDTH)
    v_p = v_all.reshape(DEPTH, batch, N_MEM, XATTN_WIDTH)

    conv_a_p, conv_b_p, conv_a_s, conv_b_s = [], [], [], []
    for i in range(DEPTH):
        j = i // N_MIXERS
        is_a = i % N_MIXERS == 0
        if is_a:
            w_in, cw = _bf(w_in_a[j]), conv_a_w[j]
            cb, ng, nb = row(conv_a_b[j]), row(norm_a_g[j]), row(norm_a_b[j])
            hist_s = state_conv_a[j]
        else:
            w_in, cw = _bf(w_in_b[j]), conv_b_w[j]
            cb = ng = nb = jnp.zeros((1, c), jnp.float32)
            hist_s = state_conv_b[j]
        w_o = _bf(w_out[i])
        g1, b1 = row(ln1_g[i]), row(ln1_b[i])

        mix, q, hist_s_new = _sample_mix(is_a, n_seq, s_rows, x_s, x_s_block, w_in, hist_s.transpose(1, 0, 2),
                                         cw, cb, ng, nb)
        attn = _sample_attention(i, n_seq, q, cache_mem_k, cache_mem_v)
        h_s = _sample_out(s_rows, x_s, x_s_block, mix, attn, w_o, g1, b1)
        h, hist_p_new = _prompt_token_sublayer(is_a, i, x_p, h_s, batch, seq, w_in, cw, cb, ng, nb,
                                               k_p, v_p, w_o, g1, b1)
        hist_s_new = hist_s_new.transpose(1, 0, 2)
        if is_a:
            conv_a_p.append(hist_p_new)
            conv_a_s.append(hist_s_new)
        else:
            conv_b_p.append(hist_p_new)
            conv_b_s.append(hist_s_new)

        last = i == DEPTH - 1
        h = _channel_sublayer(i, h, _bf(w_router[i].T), router_bias[i].reshape(N_EXPERTS, 1),
                              w_gate, w_up, w_down, ws_gate, ws_up, ws_down, row(ln2_g[i]), row(ln2_b[i]),
                              prompt_rows=p_rows if last else None)
        if not last:
            x_p = h
            x_s, x_s_block = h, s_block

    y_p, y_s = h
    new_k = k_all.reshape(DEPTH, batch, N_MEM, N_XHEADS, XHEAD_DIM)
    new_v = v_all.reshape(DEPTH, batch, N_MEM, N_XHEADS, XHEAD_DIM)
    return (y_p.reshape(batch, seq, d), y_s.reshape(n_pos, n_seq, d).transpose(1, 0, 2), new_k, new_v,
            jnp.stack(conv_a_p), jnp.stack(conv_b_p), jnp.stack(conv_a_s), jnp.stack(conv_b_s))
```

```python
import functools

import jax
import jax.numpy as jnp
from jax import lax
from jax.experimental import pallas as pl
from jax.experimental.pallas import tpu as pltpu

D_MODEL = 1024
DEPTH = 2
N_MIXERS = 2
MIX_WIDTH = D_MODEL // 2
N_MEM = 256
N_XHEADS = 4
XHEAD_DIM = MIX_WIDTH // N_XHEADS
XATTN_WIDTH = N_XHEADS * XHEAD_DIM
CONV_A_WIDTH = 31
CONV_B_WIDTH = 3
N_EXPERTS = 64
TOP_K = 8
N_GROUPS = 8
GROUP_SIZE = N_EXPERTS // N_GROUPS
TOPK_GROUPS = 4
EXPERT_FF = D_MODEL // 4
ROUTED_SCALE = 2.5
LN_EPS = 1e-5
DEEPNORM_ALPHA = (2 * DEPTH) ** 0.25

V7X_LANES = 128
V7X_SUBLANES = 8
V7X_VMEM_BYTES = 64 * 1024 * 1024

HIST_PAD = 32
PROMPT_SEQ_TILE = 512
PROMPT_ROW_GROUPS = 1
SAMPLE_BATCH_BLOCK = 8
TOKEN_TILE = 256
SLOT_WINDOW = 64
SLOT_GROUP = 16
SLOT_GROUPS_ALWAYS = 3
ROUTE_TILES_PER_STEP = 2
EXPERT_CHUNK = 8
COMBINE_CHUNK = 16
EXPERT_ROW_CHUNKS = 2
GATE_LANES = V7X_LANES
DENSE_TILE = 512
NEG_INF = float("-inf")


def _vmem_limit(nbytes):
    return int(min(max(nbytes, 16 * 1024 * 1024), V7X_VMEM_BYTES - 8 * 1024 * 1024))


def _bf(x):
    return x.astype(jnp.bfloat16)


def _dot(a, b):
    return jnp.dot(a, b, preferred_element_type=jnp.float32)


def _dot_nt(a, b):
    return lax.dot_general(a, b, (((1,), (1,)), ((), ())), preferred_element_type=jnp.float32)


def _dot_tn(a, b):
    return lax.dot_general(a, b, (((0,), (0,)), ((), ())), preferred_element_type=jnp.float32)


def _sigmoid(x):
    return 1.0 / (1.0 + jnp.exp(-x))


def _silu(x):
    return x * _sigmoid(x)


def _layer_norm(x, g, b):
    mu = jnp.mean(x, axis=-1, keepdims=True)
    xc = x - mu
    var = jnp.mean(xc * xc, axis=-1, keepdims=True)
    return xc * lax.rsqrt(var + LN_EPS) * g + b


def _memory_attention(q, k_head, v_head):
    outs = []
    for h in range(N_XHEADS):
        sl = slice(h * XHEAD_DIM, (h + 1) * XHEAD_DIM)
        s = _dot_nt(_bf(q[:, sl]), _bf(k_head(h))) * (XHEAD_DIM ** -0.5)
        e = jnp.exp(s - jnp.max(s, axis=-1, keepdims=True))
        p = e / jnp.sum(e, axis=-1, keepdims=True)
        outs.append(_dot(_bf(p), _bf(v_head(h))))
    return jnp.concatenate(outs, axis=-1)


def _full(shape):
    return pl.BlockSpec(shape, lambda *_: tuple(0 for _ in shape))


def _kv_body(mem_ref, w_ref, k_ref, v_ref):
    kv = _dot(_bf(mem_ref[...]), _bf(w_ref[0]))
    k_ref[0] = kv[:, :XATTN_WIDTH]
    v_ref[0] = kv[:, XATTN_WIDTH:]


def _kv_projection(mem2d, w_kv):
    rows = mem2d.shape[0]
    tm = 512
    out = jax.ShapeDtypeStruct((DEPTH, rows, XATTN_WIDTH), jnp.float32)
    return pl.pallas_call(
        _kv_body,
        grid=(DEPTH, rows // tm),
        in_specs=[pl.BlockSpec((tm, D_MODEL), lambda i, m: (m, 0)),
                  pl.BlockSpec((1, D_MODEL, 2 * XATTN_WIDTH), lambda i, m: (i, 0, 0))],
        out_specs=[pl.BlockSpec((1, tm, XATTN_WIDTH), lambda i, m: (i, m, 0)),
                   pl.BlockSpec((1, tm, XATTN_WIDTH), lambda i, m: (i, m, 0))],
        out_shape=[out, out],
        compiler_params=pltpu.CompilerParams(dimension_semantics=("arbitrary", "arbitrary"),
                                             vmem_limit_bytes=_vmem_limit(32 * 1024 * 1024)),
        name="kv_projection",
    )(mem2d, w_kv)


def _prompt_token_body(is_a, nl, n_tiles, x_ref, w_in_ref, cw_ref, cb_ref, ng_ref, nb_ref, k_ref, v_ref, w_out_ref,
                       g1_ref, b1_ref, sample_ref, y_ref, hist_ref, buf_ref):
    c = MIX_WIDTH
    tl = x_ref.shape[0]
    width = CONV_A_WIDTH if is_a else CONV_B_WIDTH
    step = pl.program_id(0)
    seq_step = step % nl

    @pl.when(step == n_tiles)
    def _():
        y_ref[...] = sample_ref[...]

    @pl.when((step < n_tiles) & (seq_step == 0))
    def _():
        buf_ref[pl.ds(0, HIST_PAD), :] = jnp.zeros((HIST_PAD, c), jnp.float32)

    @pl.when(step < n_tiles)
    def _():
        rg = tl // PROMPT_ROW_GROUPS
        groups = [slice(g * rg, (g + 1) * rg) for g in range(PROMPT_ROW_GROUPS)]
        xs = [x_ref[rows, :] for rows in groups]
        us = [_dot(_bf(x), w_in_ref[...]) for x in xs]
        for rows, u in zip(groups, us):
            conv_in = u[:, :c] * _sigmoid(u[:, c:2 * c]) if is_a else u[:, c:2 * c] * u[:, 2 * c:3 * c]
            buf_ref[pl.ds(HIST_PAD + rows.start, rg), :] = conv_in

        head = lambda ref: lambda h: ref[0, 0, :, h * XHEAD_DIM:(h + 1) * XHEAD_DIM]
        for rows, x, u in zip(groups, xs, us):
            base = HIST_PAD - (width - 1) + rows.start
            conv = None
            for phase in range(V7X_SUBLANES):
                taps = [t for t in range(width) if (base + t) % V7X_SUBLANES == phase]
                if not taps:
                    continue
                n = rg if phase == 0 else rg + V7X_SUBLANES
                part = None
                for t in taps:
                    term = cw_ref[t:t + 1, :] * buf_ref[pl.ds(base + t - phase, n), :]
                    part = term if part is None else part + term
                part = part[phase:phase + rg, :]
                conv = part if conv is None else conv + part

            if is_a:
                mix = _silu(_layer_norm(conv + cb_ref[...], ng_ref[...], nb_ref[...]))
                q = u[:, 2 * c:]
            else:
                mix = u[:, :c] * conv
                q = u[:, 3 * c:]
            attn = _memory_attention(q, head(k_ref), head(v_ref))
            out = _dot(_bf(jnp.concatenate([mix, attn], axis=-1)), w_out_ref[...])
            y_ref[rows, :] = _layer_norm(DEEPNORM_ALPHA * x + out, g1_ref[...], b1_ref[...])

    @pl.when((step < n_tiles) & (seq_step == nl - 1))
    def _():
        hist_ref[0] = buf_ref[pl.ds(HIST_PAD + tl - (width - 1), width - 1), :]

    @pl.when(step < n_tiles)
    def _():
        buf_ref[pl.ds(0, HIST_PAD), :] = buf_ref[pl.ds(tl, HIST_PAD), :]


def _prompt_token_sublayer(is_a, layer, x, sample_rows, batch, seq, w_in, cw, cb, ng, nb, k, v, w_out, g1, b1):
    tl = PROMPT_SEQ_TILE
    assert sample_rows.shape == (tl, D_MODEL)
    nl = seq // tl
    n_tiles = batch * nl
    c = MIX_WIDTH
    width = CONV_A_WIDTH if is_a else CONV_B_WIDTH
    n_in = w_in.shape[1]
    tile = lambda s: jnp.minimum(s, n_tiles - 1)
    mem_spec = pl.BlockSpec((1, 1, N_MEM, XATTN_WIDTH), lambda s: (layer, tile(s) // nl, 0, 0))
    return pl.pallas_call(
        functools.partial(_prompt_token_body, is_a, nl, n_tiles),
        grid=(n_tiles + 1,),
        in_specs=[pl.BlockSpec((tl, D_MODEL), lambda s: (tile(s), 0)),
                  _full((D_MODEL, n_in)), _full((width, c)), _full((1, c)), _full((1, c)), _full((1, c)),
                  mem_spec, mem_spec,
                  _full((c + XATTN_WIDTH, D_MODEL)), _full((1, D_MODEL)), _full((1, D_MODEL)),
                  _full((tl, D_MODEL))],
        out_specs=[pl.BlockSpec((tl, D_MODEL), lambda s: (s, 0)),
                   pl.BlockSpec((1, width - 1, c), lambda s: (tile(s) // nl, 0, 0))],
        out_shape=[jax.ShapeDtypeStruct(((n_tiles + 1) * tl, D_MODEL), jnp.float32),
                   jax.ShapeDtypeStruct((batch, width - 1, c), jnp.float32)],
        scratch_shapes=[pltpu.VMEM((HIST_PAD + tl, c), jnp.float32)],
        compiler_params=pltpu.CompilerParams(dimension_semantics=("arbitrary",),
                                             vmem_limit_bytes=_vmem_limit(48 * 1024 * 1024)),
        name="prompt_token_a" if is_a else "prompt_token_b",
    )(x, w_in, cw, cb, ng, nb, k, v, w_out, g1, b1, sample_rows)


def _sample_mix_body(is_a, n_seq, x_ref, w_in_ref, hist_ref, cw_ref, cb_ref, ng_ref, nb_ref,
                     mix_ref, q_ref, new_hist_ref):
    c = MIX_WIDTH
    width = CONV_A_WIDTH if is_a else CONV_B_WIDTH
    n_hist = width - 1
    n_pos = x_ref.shape[0] // n_seq
    u = _dot(_bf(x_ref[...]), w_in_ref[...])
    if is_a:
        conv_in = u[:, :c] * _sigmoid(u[:, c:2 * c])
        q_ref[...] = u[:, 2 * c:]
    else:
        conv_in = u[:, c:2 * c] * u[:, 2 * c:3 * c]
        q_ref[...] = u[:, 3 * c:]

    def full_row(j):
        if j < n_hist:
            return hist_ref[j]
        return conv_in[(j - n_hist) * n_seq:(j - n_hist + 1) * n_seq, :]

    for l in range(n_pos):
        conv = cw_ref[0:1, :] * full_row(l)
        for t in range(1, width):
            conv = conv + cw_ref[t:t + 1, :] * full_row(l + t)
        rows = slice(l * n_seq, (l + 1) * n_seq)
        if is_a:
            mix_ref[rows, :] = _silu(_layer_norm(conv + cb_ref[...], ng_ref[...], nb_ref[...]))
        else:
            mix_ref[rows, :] = u[rows, :c] * conv
    for j in range(n_hist):
        new_hist_ref[j] = full_row(j + n_pos)


def _sample_mix(is_a, n_seq, rows, x, x_block, w_in, hist, cw, cb, ng, nb):
    c = MIX_WIDTH
    small = (w_in, hist, cw, cb, ng, nb)
    return pl.pallas_call(
        functools.partial(_sample_mix_body, is_a, n_seq),
        grid=(1,),
        in_specs=[pl.BlockSpec((rows, D_MODEL), lambda i: (x_block, 0))] + [_full(a.shape) for a in small],
        out_specs=[_full((rows, c)), _full((rows, XATTN_WIDTH)), _full(hist.shape)],
        out_shape=[jax.ShapeDtypeStruct((rows, c), jnp.float32),
                   jax.ShapeDtypeStruct((rows, XATTN_WIDTH), jnp.float32),
                   jax.ShapeDtypeStruct(hist.shape, jnp.float32)],
        compiler_params=pltpu.CompilerParams(dimension_semantics=("arbitrary",),
                                             vmem_limit_bytes=_vmem_limit(48 * 1024 * 1024)),
        name="sample_mix_a" if is_a else "sample_mix_b",
    )(x, *small)


def _sample_attn_body(n_seq, q_ref, k_ref, v_ref, o_ref, bias_ref):
    bb = k_ref.shape[1]
    n_pos = q_ref.shape[0] // n_seq
    first = pl.multiple_of(pl.program_id(0) * bb, bb)

    @pl.when(pl.program_id(0) == 0)
    def _():
        r = lax.broadcasted_iota(jnp.int32, bias_ref.shape, 0)
        col = lax.broadcasted_iota(jnp.int32, bias_ref.shape, 1)
        valid = ((r % N_XHEADS) == (col // (n_pos * bb))) & ((r // (N_MEM * N_XHEADS)) == (col % bb))
        bias_ref[...] = jnp.where(valid, 0.0, NEG_INF)

    q = jnp.concatenate([q_ref[pl.ds(l * n_seq + first, bb), :] for l in range(n_pos)], axis=0)
    nq = n_pos * bb
    n_rows = bb * N_MEM * N_XHEADS
    k_rows = k_ref[0].reshape(n_rows, XHEAD_DIM)
    v_rows = v_ref[0].reshape(n_rows, XHEAD_DIM)
    q_heads = jnp.concatenate([q[:, h * XHEAD_DIM:(h + 1) * XHEAD_DIM] for h in range(N_XHEADS)], axis=0)
    s = _dot_nt(_bf(k_rows), _bf(q_heads)) * (XHEAD_DIM ** -0.5) + bias_ref[...]
    e = jnp.exp(s - jnp.max(s, axis=0, keepdims=True))
    p = e / jnp.sum(e, axis=0, keepdims=True)
    o_heads = _dot_tn(_bf(p), _bf(v_rows))
    o = jnp.concatenate([o_heads[h * nq:(h + 1) * nq] for h in range(N_XHEADS)], axis=1)
    for l in range(n_pos):
        o_ref[pl.ds(l * n_seq + first, bb), :] = o[l * bb:(l + 1) * bb, :]


def _sample_attention(layer, n_seq, q, mem_k, mem_v):
    rows = q.shape[0]
    bb = SAMPLE_BATCH_BLOCK
    mem_spec = pl.BlockSpec((1, bb, N_MEM, N_XHEADS, XHEAD_DIM), lambda i: (layer, i, 0, 0, 0))
    return pl.pallas_call(
        functools.partial(_sample_attn_body, n_seq),
        grid=(n_seq // bb,),
        in_specs=[pl.BlockSpec((rows, XATTN_WIDTH), lambda i: (0, 0)), mem_spec, mem_spec],
        out_specs=pl.BlockSpec((rows, XATTN_WIDTH), lambda i: (0, 0)),
        out_shape=jax.ShapeDtypeStruct((rows, XATTN_WIDTH), jnp.float32),
        scratch_shapes=[pltpu.VMEM((bb * N_MEM * N_XHEADS, N_XHEADS * (rows // n_seq) * bb), jnp.float32)],
        compiler_params=pltpu.CompilerParams(dimension_semantics=("arbitrary",),
                                             vmem_limit_bytes=_vmem_limit(40 * 1024 * 1024)),
        name="sample_attention",
    )(q, mem_k, mem_v)


def _sample_out_body(x_ref, mix_ref, attn_ref, w_out_ref, g1_ref, b1_ref, y_ref):
    cat = jnp.concatenate([mix_ref[...], attn_ref[...]], axis=-1)
    out = _dot(_bf(cat), w_out_ref[...])
    y_ref[...] = _layer_norm(DEEPNORM_ALPHA * x_ref[...] + out, g1_ref[...], b1_ref[...])


def _sample_out(rows, x, x_block, mix, attn, w_out, g1, b1):
    small = (mix, attn, w_out, g1, b1)
    return pl.pallas_call(
        _sample_out_body,
        grid=(1,),
        in_specs=[pl.BlockSpec((rows, D_MODEL), lambda i: (x_block, 0))] + [_full(a.shape) for a in small],
        out_specs=_full((rows, D_MODEL)),
        out_shape=jax.ShapeDtypeStruct((rows, D_MODEL), jnp.float32),
        compiler_params=pltpu.CompilerParams(dimension_semantics=("arbitrary",),
                                             vmem_limit_bytes=_vmem_limit(32 * 1024 * 1024)),
        name="sample_out",
    )(x, *small)


def _first_index_of(mask, index, n):
    cand = jnp.where(mask, index, float(n))
    while cand.ndim > 2:
        cand = jnp.min(cand, axis=0)
    return jnp.min(cand, axis=0, keepdims=True)


def _max_all(x):
    while x.ndim > 2:
        x = jnp.max(x, axis=0)
    return jnp.max(x, axis=0, keepdims=True)


def _route_gates(logits, bias):
    tm = logits.shape[1]
    scores = _sigmoid(logits)
    biased = scores + bias
    grp = biased.reshape(N_GROUPS, GROUP_SIZE, tm)

    within = lax.broadcasted_iota(jnp.int32, grp.shape, 1).astype(jnp.float32)
    top1 = jnp.max(grp, axis=1, keepdims=True)
    first = jnp.min(jnp.where(grp == top1, within, float(GROUP_SIZE)), axis=1, keepdims=True)
    top2 = jnp.max(jnp.where(within == first, NEG_INF, grp), axis=1, keepdims=True)
    grp_score = (top1 + top2).reshape(N_GROUPS, tm)

    gidx = lax.broadcasted_iota(jnp.int32, grp_score.shape, 0).astype(jnp.float32)
    grp_sel = jnp.zeros(grp_score.shape, jnp.float32)
    for _ in range(TOPK_GROUPS):
        best = jnp.max(grp_score, axis=0, keepdims=True)
        pick = gidx == _first_index_of(grp_score == best, gidx, N_GROUPS)
        grp_sel = jnp.where(pick, 1.0, grp_sel)
        grp_score = jnp.where(pick, NEG_INF, grp_score)

    eidx = (lax.broadcasted_iota(jnp.int32, grp.shape, 0) * GROUP_SIZE
            + lax.broadcasted_iota(jnp.int32, grp.shape, 1)).astype(jnp.float32)
    cand = jnp.where(grp_sel.reshape(N_GROUPS, 1, tm) > 0.0, grp, NEG_INF)
    chosen = jnp.zeros(grp.shape, jnp.float32)
    for _ in range(TOP_K):
        best = _max_all(cand).reshape(1, 1, tm)
        pick = eidx == _first_index_of(cand == best, eidx, N_EXPERTS).reshape(1, 1, tm)
        chosen = jnp.where(pick, 1.0, chosen)
        cand = jnp.where(pick, NEG_INF, cand)

    w = jnp.where(chosen > 0.0, scores.reshape(grp.shape), 0.0)
    total = jnp.sum(jnp.sum(w, axis=0), axis=0, keepdims=True).reshape(1, 1, tm)
    gates = (w / total * ROUTED_SCALE).reshape(N_EXPERTS, tm)
    return chosen.reshape(N_EXPERTS, tm), gates


def _slot_matches(sel_row, first_slot=0, n_slots=None):
    tm = sel_row.shape[1]
    n_slots = SLOT_WINDOW if n_slots is None else n_slots
    slot = (lax.broadcasted_iota(jnp.int32, (n_slots, tm), 0) + first_slot).astype(jnp.float32)
    return jnp.broadcast_to(sel_row, (n_slots, tm)) == slot


def _slot_onehot(sel, first_expert):
    rows = [_slot_matches(sel[e:e + 1, :]) for e in range(first_expert, first_expert + EXPERT_CHUNK)]
    return jnp.where(jnp.concatenate(rows, axis=0), 1.0, 0.0).astype(jnp.bfloat16)


def _route_body(x_ref, wr_ref, bias_ref, xs_ref, sel_ref, gate_ref, over_ref, flag_ref, fill_ref):
    tm = TOKEN_TILE
    n_sub = x_ref.shape[0] // tm
    n_groups = SLOT_WINDOW // SLOT_GROUP
    any_over = None
    fills = []
    for t in range(n_sub):
        cols = slice(t * tm, (t + 1) * tm)
        xb = _bf(x_ref[cols, :])
        chosen, gates = _route_gates(_dot_nt(wr_ref[...], xb), bias_ref[...])

        earlier = (lax.broadcasted_iota(jnp.int32, (tm, tm), 0) < lax.broadcasted_iota(jnp.int32, (tm, tm), 1))
        pos = _dot(_bf(chosen), jnp.where(earlier, 1.0, 0.0).astype(jnp.bfloat16))
        routed = chosen > 0.0
        in_window = routed & (pos < float(SLOT_WINDOW))
        sel = jnp.where(in_window, pos, -1.0)
        sel_ref[:, cols] = sel
        gate_ref[:, cols] = jnp.where(in_window, gates, 0.0)
        fills.append(jnp.max(sel, axis=1, keepdims=True))

        over = jnp.where(routed & (pos >= float(SLOT_WINDOW)), gates, 0.0)
        pad = jnp.zeros((GATE_LANES - N_EXPERTS, tm), jnp.float32)
        over_ref[cols, :] = jnp.concatenate([over, pad], axis=0).T
        tile_over = jnp.max(jnp.max(over, axis=0, keepdims=True), axis=1, keepdims=True)
        any_over = tile_over if any_over is None else jnp.maximum(any_over, tile_over)

        for first in range(0, N_EXPERTS, EXPERT_CHUNK):
            slots = _bf(_dot(_slot_onehot(sel, first), xb))
            xs_ref[first:first + EXPERT_CHUNK, :, t] = slots.reshape(EXPERT_CHUNK, n_groups, SLOT_GROUP, D_MODEL)
    flag_ref[...] = jnp.broadcast_to(any_over, flag_ref.shape[1:])[None]
    lane = lax.broadcasted_iota(jnp.int32, (N_EXPERTS, V7X_LANES), 1)
    fill = jnp.full((N_EXPERTS, V7X_LANES), -1.0, jnp.float32)
    for t in range(n_sub):
        fill = jnp.where(lane == t, fills[t], fill)
    fill_ref[...] = fill[None]


def _route_dispatch(x, w_router_t, bias_col):
    rows = x.shape[0]
    n_sub = ROUTE_TILES_PER_STEP
    tm = TOKEN_TILE * n_sub
    n_steps = rows // tm
    n_groups = SLOT_WINDOW // SLOT_GROUP
    return pl.pallas_call(
        _route_body,
        grid=(n_steps,),
        in_specs=[pl.BlockSpec((tm, D_MODEL), lambda i: (i, 0)),
                  _full((N_EXPERTS, D_MODEL)), _full((N_EXPERTS, 1))],
        out_specs=[pl.BlockSpec((N_EXPERTS, n_groups, n_sub, SLOT_GROUP, D_MODEL), lambda i: (0, 0, i, 0, 0)),
                   pl.BlockSpec((N_EXPERTS, tm), lambda i: (0, i)),
                   pl.BlockSpec((N_EXPERTS, tm), lambda i: (0, i)),
                   pl.BlockSpec((tm, GATE_LANES), lambda i: (i, 0)),
                   pl.BlockSpec((1, 8, V7X_LANES), lambda i: (i, 0, 0)),
                   pl.BlockSpec((1, N_EXPERTS, V7X_LANES), lambda i: (i, 0, 0))],
        out_shape=[jax.ShapeDtypeStruct((N_EXPERTS, n_groups, n_steps * n_sub, SLOT_GROUP, D_MODEL), jnp.bfloat16),
                   jax.ShapeDtypeStruct((N_EXPERTS, rows), jnp.float32),
                   jax.ShapeDtypeStruct((N_EXPERTS, rows), jnp.float32),
                   jax.ShapeDtypeStruct((rows, GATE_LANES), jnp.float32),
                   jax.ShapeDtypeStruct((n_steps, 8, V7X_LANES), jnp.float32),
                   jax.ShapeDtypeStruct((n_steps, N_EXPERTS, V7X_LANES), jnp.float32)],
        compiler_params=pltpu.CompilerParams(dimension_semantics=("arbitrary",),
                                             vmem_limit_bytes=_vmem_limit(48 * 1024 * 1024)),
        name="route_dispatch",
    )(x, w_router_t, bias_col)


def _expert_body(need_ref, last_e_ref, last_c_ref, xm_ref, xt_ref, sel_ref, gate_ref, wg_ref, wu_ref, wd_ref,
                 ym_ref, yt_ref, wg_bf, wu_bf, wd_bf):
    del last_e_ref, last_c_ref

    @pl.when(pl.program_id(1) == 0)
    def _():
        wg_bf[...] = _bf(wg_ref[0, 0])
        wu_bf[...] = _bf(wu_ref[0, 0])
        wd_bf[...] = _bf(wd_ref[0, 0])

    row = pl.ds(pl.program_id(0) % V7X_SUBLANES, 1)
    n_main = xm_ref.shape[1]
    n_tiles = xm_ref.shape[2]
    tm = sel_ref.shape[1] // n_tiles

    def run(x_ref, y_ref, first_group):
        n_g = x_ref.shape[1]
        slot_gates = []
        for g in range(first_group, first_group + n_g):
            for t in range(n_tiles):
                cols = slice(t * tm, (t + 1) * tm)
                match = _slot_matches(sel_ref[row, cols], g * SLOT_GROUP, SLOT_GROUP)
                slot_gates.append(jnp.sum(jnp.where(match, gate_ref[row, cols], 0.0), axis=-1, keepdims=True))
        gate = jnp.concatenate(slot_gates, axis=0)
        x = x_ref[0].reshape(n_g * n_tiles * SLOT_GROUP, D_MODEL)
        h = _silu(_dot(x, wg_bf[...])) * _dot(x, wu_bf[...])
        y_ref[0] = _bf(_dot(_bf(h), wd_bf[...]) * gate).reshape(n_g, n_tiles, SLOT_GROUP, D_MODEL)

    run(xm_ref, ym_ref, 0)

    @pl.when(need_ref[pl.program_id(0) * pl.num_programs(1) + pl.program_id(1)] > 0)
    def _():
        run(xt_ref, yt_ref, n_main)


def _slot_group_split():
    n_groups = SLOT_WINDOW // SLOT_GROUP
    n_main = min(SLOT_GROUPS_ALWAYS, n_groups - 1)
    n_tail = n_groups - n_main
    assert n_main % n_tail == 0, "the trailing groups must form one block of the group axis"
    return n_main, n_tail


def _experts(layer, xs, sel, gate, fill, w_gate, w_up, w_down):
    n_e, _, n_tiles_all, _, _ = xs.shape
    n_chunks = EXPERT_ROW_CHUNKS
    n_tiles = n_tiles_all // n_chunks
    tokens = sel.shape[1] // n_chunks
    n_main, n_tail = _slot_group_split()

    need = (jnp.max(fill.reshape(n_e, n_chunks, n_tiles), axis=-1) >= n_main * SLOT_GROUP).astype(jnp.int32)
    steps = jnp.arange(n_e * n_chunks, dtype=jnp.int32)
    last = jnp.maximum(lax.cummax(jnp.where(need.reshape(-1) > 0, steps, -1)), 0)
    last_e, last_c = last // n_chunks, last % n_chunks

    w_in_spec = pl.BlockSpec((1, 1, D_MODEL, EXPERT_FF), lambda e, c, *_: (layer, e, 0, 0))
    route_spec = pl.BlockSpec((V7X_SUBLANES, tokens), lambda e, c, *_: (e // V7X_SUBLANES, c))
    main_spec = pl.BlockSpec((1, n_main, n_tiles, SLOT_GROUP, D_MODEL), lambda e, c, *_: (e, 0, c, 0, 0))

    def tail_map(group_block):
        return lambda e, c, nd, le, lc: (le[e * n_chunks + c], group_block, lc[e * n_chunks + c], 0, 0)

    tail_block = (1, n_tail, n_tiles, SLOT_GROUP, D_MODEL)
    ym, yt = pl.pallas_call(
        _expert_body,
        grid_spec=pltpu.PrefetchScalarGridSpec(
            num_scalar_prefetch=3,
            grid=(n_e, n_chunks),
            in_specs=[main_spec, pl.BlockSpec(tail_block, tail_map(n_main // n_tail)),
                      route_spec, route_spec, w_in_spec, w_in_spec,
                      pl.BlockSpec((1, 1, EXPERT_FF, D_MODEL), lambda e, c, *_: (layer, e, 0, 0))],
            out_specs=[main_spec, pl.BlockSpec(tail_block, tail_map(0))],
            scratch_shapes=[pltpu.VMEM((D_MODEL, EXPERT_FF), jnp.bfloat16),
                            pltpu.VMEM((D_MODEL, EXPERT_FF), jnp.bfloat16),
                            pltpu.VMEM((EXPERT_FF, D_MODEL), jnp.bfloat16)],
        ),
        out_shape=[jax.ShapeDtypeStruct((n_e, n_main, n_tiles_all, SLOT_GROUP, D_MODEL), jnp.bfloat16),
                   jax.ShapeDtypeStruct((n_e, n_tail, n_tiles_all, SLOT_GROUP, D_MODEL), jnp.bfloat16)],
        compiler_params=pltpu.CompilerParams(dimension_semantics=("arbitrary", "arbitrary"),
                                             vmem_limit_bytes=_vmem_limit(40 * 1024 * 1024)),
        name="experts",
    )(need.reshape(-1), last_e, last_c, xs, xs, sel, gate, w_gate, w_up, w_down)
    return ym, yt, need


def _combine_body(has_extra, n_prompt_tiles, tiles_per_chunk, tile_need_ref, last_ref, need_ref,
                  x_ref, sel_ref, ym_ref, yt_ref, wsg_ref, wsu_ref, wsd_ref, g2_ref, b2_ref, *rest):
    del last_ref
    rest = list(rest)
    extra_ref = rest.pop(0) if has_extra else None
    out_refs = [rest.pop(0)] if n_prompt_tiles is None else [rest.pop(0), rest.pop(0)]
    wsg_bf, wsu_bf, wsd_bf, acc_ref = rest
    tile = pl.program_id(0)
    n_main, n_tail = ym_ref.shape[1], yt_ref.shape[1]
    n_chunks = need_ref.shape[0] // N_EXPERTS

    @pl.when(tile == 0)
    def _():
        wsg_bf[...] = _bf(wsg_ref[0])
        wsu_bf[...] = _bf(wsu_ref[0])
        wsd_bf[...] = _bf(wsd_ref[0])

    def onehot(first_expert, first_group, n_g):
        rows = [_slot_matches(sel_ref[e:e + 1, :], first_group * SLOT_GROUP, n_g * SLOT_GROUP)
                for e in range(first_expert, first_expert + COMBINE_CHUNK)]
        return jnp.where(jnp.concatenate(rows, axis=0), 1.0, 0.0).astype(jnp.bfloat16)

    x = x_ref[...]
    xb = _bf(x)
    y = _dot(_bf(_silu(_dot(xb, wsg_bf[...])) * _dot(xb, wsu_bf[...])), wsd_bf[...])
    for first in range(0, N_EXPERTS, COMBINE_CHUNK):
        ys = ym_ref[first:first + COMBINE_CHUNK, :, 0].reshape(COMBINE_CHUNK * n_main * SLOT_GROUP, D_MODEL)
        y = y + _dot_tn(onehot(first, 0, n_main), ys)
    if has_extra:
        y = y + extra_ref[...]
    acc_ref[...] = y

    @pl.when(tile_need_ref[tile] > 0)
    def _():
        chunk = tile // tiles_per_chunk
        part = jnp.zeros(acc_ref.shape, jnp.float32)
        for first in range(0, N_EXPERTS, COMBINE_CHUNK):
            rows = []
            for e in range(first, first + COMBINE_CHUNK):
                ye = yt_ref[e, :, 0].reshape(n_tail * SLOT_GROUP, D_MODEL)
                rows.append(jnp.where(need_ref[e * n_chunks + chunk] > 0, ye, jnp.zeros_like(ye)))
            part = part + _dot_tn(onehot(first, n_main, n_tail), jnp.concatenate(rows, axis=0))
        acc_ref[...] += part

    y = _layer_norm(DEEPNORM_ALPHA * x + acc_ref[...], g2_ref[...], b2_ref[...])

    if n_prompt_tiles is None:
        out_refs[0][...] = y
    else:
        @pl.when(tile < n_prompt_tiles)
        def _():
            out_refs[0][...] = y

        @pl.when(tile >= n_prompt_tiles)
        def _():
            out_refs[1][...] = y


def _combine(layer, x, sel, fill, ym, yt, need, ws_gate, ws_up, ws_down, g2, b2, extra=None, prompt_rows=None):
    rows = x.shape[0]
    tm = TOKEN_TILE
    n_tiles = rows // tm
    has_extra = extra is not None
    n_main, n_tail = ym.shape[1], yt.shape[1]
    tile_need = (jnp.max(fill, axis=0) >= n_main * SLOT_GROUP).astype(jnp.int32)
    tiles = jnp.arange(n_tiles, dtype=jnp.int32)
    last = jnp.maximum(lax.cummax(jnp.where(tile_need > 0, tiles, -1)), 0)

    in_specs = [pl.BlockSpec((tm, D_MODEL), lambda i, *_: (i, 0)),
                pl.BlockSpec((N_EXPERTS, tm), lambda i, *_: (0, i)),
                pl.BlockSpec((N_EXPERTS, n_main, 1, SLOT_GROUP, D_MODEL), lambda i, *_: (0, 0, i, 0, 0)),
                pl.BlockSpec((N_EXPERTS, n_tail, 1, SLOT_GROUP, D_MODEL), lambda i, tn, la, nd: (0, 0, la[i], 0, 0)),
                pl.BlockSpec((1, D_MODEL, EXPERT_FF), lambda i, *_: (layer, 0, 0)),
                pl.BlockSpec((1, D_MODEL, EXPERT_FF), lambda i, *_: (layer, 0, 0)),
                pl.BlockSpec((1, EXPERT_FF, D_MODEL), lambda i, *_: (layer, 0, 0)),
                pl.BlockSpec((1, D_MODEL), lambda i, *_: (0, 0)), pl.BlockSpec((1, D_MODEL), lambda i, *_: (0, 0))]
    args = [x, sel, ym, yt, ws_gate, ws_up, ws_down, g2, b2]
    if has_extra:
        in_specs.append(pl.BlockSpec((tm, D_MODEL), lambda i, *_: (i, 0)))
        args.append(extra)
    if prompt_rows is None:
        n_prompt_tiles = None
        out_specs = pl.BlockSpec((tm, D_MODEL), lambda i, *_: (i, 0))
        out_shape = jax.ShapeDtypeStruct((rows, D_MODEL), jnp.float32)
    else:
        n_prompt_tiles = prompt_rows // tm
        out_specs = [pl.BlockSpec((tm, D_MODEL), lambda i, *_: (jnp.minimum(i, n_prompt_tiles - 1), 0)),
                     pl.BlockSpec((tm, D_MODEL), lambda i, *_: (jnp.maximum(i - n_prompt_tiles, 0), 0))]
        out_shape = [jax.ShapeDtypeStruct((prompt_rows, D_MODEL), jnp.float32),
                     jax.ShapeDtypeStruct((rows - prompt_rows, D_MODEL), jnp.float32)]
    return pl.pallas_call(
        functools.partial(_combine_body, has_extra, n_prompt_tiles, n_tiles // need.shape[1]),
        grid_spec=pltpu.PrefetchScalarGridSpec(
            num_scalar_prefetch=3,
            grid=(n_tiles,),
            in_specs=in_specs,
            out_specs=out_specs,
            scratch_shapes=[pltpu.VMEM((D_MODEL, EXPERT_FF), jnp.bfloat16),
                            pltpu.VMEM((D_MODEL, EXPERT_FF), jnp.bfloat16),
                            pltpu.VMEM((EXPERT_FF, D_MODEL), jnp.bfloat16),
                            pltpu.VMEM((tm, D_MODEL), jnp.float32)],
        ),
        out_shape=out_shape,
        compiler_params=pltpu.CompilerParams(dimension_semantics=("arbitrary",),
                                             vmem_limit_bytes=_vmem_limit(48 * 1024 * 1024)),
        name="combine_extra" if has_extra else "combine",
    )(tile_need, last, need.reshape(-1), *args)


def _dense_body(x_ref, gate_ref, wg_ref, wu_ref, wd_ref, y_ref, xb_ref):
    e = pl.program_id(1)

    @pl.when(e == 0)
    def _():
        xb_ref[...] = _bf(x_ref[...])
        y_ref[...] = jnp.zeros(y_ref.shape, jnp.float32)

    xb = xb_ref[...]
    h = _silu(_dot(xb, _bf(wg_ref[0, 0]))) * _dot(xb, _bf(wu_ref[0, 0]))
    down = _dot(_bf(h), _bf(wd_ref[0, 0]))
    lane = lax.broadcasted_iota(jnp.int32, gate_ref.shape, 1)
    gate = jnp.sum(jnp.where(lane == e, gate_ref[...], 0.0), axis=-1, keepdims=True)
    y_ref[...] += down * gate


def _dense_experts(layer, x, gates, w_gate, w_up, w_down):
    rows = x.shape[0]
    tm = DENSE_TILE
    w_in_spec = pl.BlockSpec((1, 1, D_MODEL, EXPERT_FF), lambda i, e: (layer, e, 0, 0))
    return pl.pallas_call(
        _dense_body,
        grid=(rows // tm, N_EXPERTS),
        in_specs=[pl.BlockSpec((tm, D_MODEL), lambda i, e: (i, 0)),
                  pl.BlockSpec((tm, GATE_LANES), lambda i, e: (i, 0)),
                  w_in_spec, w_in_spec,
                  pl.BlockSpec((1, 1, EXPERT_FF, D_MODEL), lambda i, e: (layer, e, 0, 0))],
        out_specs=pl.BlockSpec((tm, D_MODEL), lambda i, e: (i, 0)),
        out_shape=jax.ShapeDtypeStruct((rows, D_MODEL), jnp.float32),
        scratch_shapes=[pltpu.VMEM((tm, D_MODEL), jnp.bfloat16)],
        compiler_params=pltpu.CompilerParams(dimension_semantics=("arbitrary", "arbitrary"),
                                             vmem_limit_bytes=_vmem_limit(40 * 1024 * 1024)),
        name="dense_overflow",
    )(x, gates, w_gate, w_up, w_down)


def _channel_sublayer(layer, x, w_router_t, bias_col, w_gate, w_up, w_down, ws_gate, ws_up, ws_down, g2, b2,
                      prompt_rows=None):
    xs, sel, gate, over, flags, fill = _route_dispatch(x, w_router_t, bias_col)
    fill = fill[:, :, :ROUTE_TILES_PER_STEP].transpose(1, 0, 2).reshape(N_EXPERTS, -1)
    ym, yt, need = _experts(layer, xs, sel, gate, fill, w_gate, w_up, w_down)
    rest = (x, sel, fill, ym, yt, need, ws_gate, ws_up, ws_down, g2, b2)

    def with_overflow():
        extra = _dense_experts(layer, x, over, w_gate, w_up, w_down)
        return _combine(layer, *rest, extra=extra, prompt_rows=prompt_rows)

    def without_overflow():
        return _combine(layer, *rest, prompt_rows=prompt_rows)

    return lax.cond(jnp.max(flags) > 0.0, with_overflow, without_overflow)


def kernel(x_prompt, x_sample, mem_prompt, cache_mem_k, cache_mem_v, state_conv_a, state_conv_b, w_in_a, conv_a_w, conv_a_b, norm_a_g, norm_a_b, w_in_b, conv_b_w, w_kv, w_out, ln1_g, ln1_b, w_router, router_bias, w_gate, w_up, w_down, ws_gate, ws_up, ws_down, ln2_g, ln2_b):
    batch, seq, d = x_prompt.shape
    n_seq, n_pos, _ = x_sample.shape
    c = MIX_WIDTH
    p_rows, s_rows = batch * seq, n_pos * n_seq
    s_block = p_rows // s_rows
    row = lambda a: a.reshape(1, -1)

    x_p = x_prompt.reshape(p_rows, d)
    x_s, x_s_block = x_sample.transpose(1, 0, 2).reshape(s_rows, d), 0
    k_all, v_all = _kv_projection(mem_prompt.reshape(batch * N_MEM, d), w_kv)
    k_p = k_all.reshape(DEPTH, batch, N_MEM, XATTN_WIDTH)
    v_p = v_all.reshape(DEPTH, batch, N_MEM, XATTN_WIDTH)

    conv_a_p, conv_b_p, conv_a_s, conv_b_s = [], [], [], []
    for i in range(DEPTH):
        j = i // N_MIXERS
        is_a = i % N_MIXERS == 0
        if is_a:
            w_in, cw = _bf(w_in_a[j]), conv_a_w[j]
            cb, ng, nb = row(conv_a_b[j]), row(norm_a_g[j]), row(norm_a_b[j])
            hist_s = state_conv_a[j]
        else:
            w_in, cw = _bf(w_in_b[j]), conv_b_w[j]
            cb = ng = nb = jnp.zeros((1, c), jnp.float32)
            hist_s = state_conv_b[j]
        w_o = _bf(w_out[i])
        g1, b1 = row(ln1_g[i]), row(ln1_b[i])

        mix, q, hist_s_new = _sample_mix(is_a, n_seq, s_rows, x_s, x_s_block, w_in, hist_s.transpose(1, 0, 2),
                                         cw, cb, ng, nb)
        attn = _sample_attention(i, n_seq, q, cache_mem_k, cache_mem_v)
        h_s = _sample_out(s_rows, x_s, x_s_block, mix, attn, w_o, g1, b1)
        h, hist_p_new = _prompt_token_sublayer(is_a, i, x_p, h_s, batch, seq, w_in, cw, cb, ng, nb,
                                               k_p, v_p, w_o, g1, b1)
        hist_s_new = hist_s_new.transpose(1, 0, 2)
        if is_a:
            conv_a_p.append(hist_p_new)
            conv_a_s.append(hist_s_new)
        else:
            conv_b_p.append(hist_p_new)
            conv_b_s.append(hist_s_new)

        last = i == DEPTH - 1
        h = _channel_sublayer(i, h, _bf(w_router[i].T), router_bias[i].reshape(N_EXPERTS, 1),
                              w_gate, w_up, w_down, ws_gate, ws_up, ws_down, row(ln2_g[i]), row(ln2_b[i]),
                              prompt_rows=p_rows if last else None)
        if not last:
            x_p = h
            x_s, x_s_block = h, s_block

    y_p, y_s = h
    new_k = k_all.reshape(DEPTH, batch, N_MEM, N_XHEADS, XHEAD_DIM)
    new_v = v_all.reshape(DEPTH, batch, N_MEM, N_XHEADS, XHEAD_DIM)
    return (y_p.reshape(batch, seq, d), y_s.reshape(n_pos, n_seq, d).transpose(1, 0, 2), new_k, new_v,
            jnp.stack(conv_a_p), jnp.stack(conv_b_p), jnp.stack(conv_a_s), jnp.stack(conv_b_s))
```

```python
import functools

import jax
import jax.numpy as jnp
from jax import lax
from jax.experimental import pallas as pl
from jax.experimental.pallas import tpu as pltpu

D_MODEL = 1024
DEPTH = 2
N_MIXERS = 2
MIX_WIDTH = D_MODEL // 2
N_MEM = 256
N_XHEADS = 4
XHEAD_DIM = MIX_WIDTH // N_XHEADS
XATTN_WIDTH = N_XHEADS * XHEAD_DIM
CONV_A_WIDTH = 31
CONV_B_WIDTH = 3
N_EXPERTS = 64
TOP_K = 8
N_GROUPS = 8
GROUP_SIZE = N_EXPERTS // N_GROUPS
TOPK_GROUPS = 4
EXPERT_FF = D_MODEL // 4
ROUTED_SCALE = 2.5
LN_EPS = 1e-5
DEEPNORM_ALPHA = (2 * DEPTH) ** 0.25

V7X_LANES = 128
V7X_SUBLANES = 8
V7X_VMEM_BYTES = 64 * 1024 * 1024

HIST_PAD = 32
PROMPT_SEQ_TILE = 512
PROMPT_ROW_GROUPS = 1
SAMPLE_BATCH_BLOCK = 8
TOKEN_TILE = 256
SLOT_WINDOW = 64
SLOT_GROUP = 16
SLOT_GROUPS_ALWAYS = 3
ROUTE_TILES_PER_STEP = 2
EXPERT_CHUNK = 8
COMBINE_CHUNK = 16
EXPERT_ROW_CHUNKS = 2
GATE_LANES = V7X_LANES
DENSE_TILE = 512
NEG_INF = float("-inf")


def _vmem_limit(nbytes):
    return int(min(max(nbytes, 16 * 1024 * 1024), V7X_VMEM_BYTES - 8 * 1024 * 1024))


def _bf(x):
    return x.astype(jnp.bfloat16)


def _dot(a, b):
    return jnp.dot(a, b, preferred_element_type=jnp.float32)


def _dot_nt(a, b):
    return lax.dot_general(a, b, (((1,), (1,)), ((), ())), preferred_element_type=jnp.float32)


def _dot_tn(a, b):
    return lax.dot_general(a, b, (((0,), (0,)), ((), ())), preferred_element_type=jnp.float32)


def _sigmoid(x):
    return 1.0 / (1.0 + jnp.exp(-x))


def _silu(x):
    return x * _sigmoid(x)


def _layer_norm(x, g, b):
    mu = jnp.mean(x, axis=-1, keepdims=True)
    xc = x - mu
    var = jnp.mean(xc * xc, axis=-1, keepdims=True)
    return xc * lax.rsqrt(var + LN_EPS) * g + b


def _memory_attention(q, k_head, v_head):
    outs = []
    for h in range(N_XHEADS):
        sl = slice(h * XHEAD_DIM, (h + 1) * XHEAD_DIM)
        s = _dot_nt(_bf(q[:, sl]), _bf(k_head(h))) * (XHEAD_DIM ** -0.5)
        e = jnp.exp(s - jnp.max(s, axis=-1, keepdims=True))
        p = e / jnp.sum(e, axis=-1, keepdims=True)
        outs.append(_dot(_bf(p), _bf(v_head(h))))
    return jnp.concatenate(outs, axis=-1)


def _full(shape):
    return pl.BlockSpec(shape, lambda *_: tuple(0 for _ in shape))


def _kv_body(mem_ref, w_ref, k_ref, v_ref):
    kv = _dot(_bf(mem_ref[...]), _bf(w_ref[0]))
    k_ref[0] = kv[:, :XATTN_WIDTH]
    v_ref[0] = kv[:, XATTN_WIDTH:]


def _kv_projection(mem2d, w_kv):
    rows = mem2d.shape[0]
    tm = 512
    out = jax.ShapeDtypeStruct((DEPTH, rows, XATTN_WIDTH), jnp.float32)
    return pl.pallas_call(
        _kv_body,
        grid=(DEPTH, rows // tm),
        in_specs=[pl.BlockSpec((tm, D_MODEL), lambda i, m: (m, 0)),
                  pl.BlockSpec((1, D_MODEL, 2 * XATTN_WIDTH), lambda i, m: (i, 0, 0))],
        out_specs=[pl.BlockSpec((1, tm, XATTN_WIDTH), lambda i, m: (i, m, 0)),
                   pl.BlockSpec((1, tm, XATTN_WIDTH), lambda i, m: (i, m, 0))],
        out_shape=[out, out],
        compiler_params=pltpu.CompilerParams(dimension_semantics=("arbitrary", "arbitrary"),
                                             vmem_limit_bytes=_vmem_limit(32 * 1024 * 1024)),
        name="kv_projection",
    )(mem2d, w_kv)


def _prompt_token_body(is_a, nl, n_tiles, x_ref, w_in_ref, cw_ref, cb_ref, ng_ref, nb_ref, k_ref, v_ref, w_out_ref,
                       g1_ref, b1_ref, sample_ref, y_ref, hist_ref, buf_ref):
    c = MIX_WIDTH
    tl = x_ref.shape[0]
    width = CONV_A_WIDTH if is_a else CONV_B_WIDTH
    step = pl.program_id(0)
    seq_step = step % nl

    @pl.when(step == n_tiles)
    def _():
        y_ref[...] = sample_ref[...]

    @pl.when((step < n_tiles) & (seq_step == 0))
    def _():
        buf_ref[pl.ds(0, HIST_PAD), :] = jnp.zeros((HIST_PAD, c), jnp.float32)

    @pl.when(step < n_tiles)
    def _():
        rg = tl // PROMPT_ROW_GROUPS
        groups = [slice(g * rg, (g + 1) * rg) for g in range(PROMPT_ROW_GROUPS)]
        xs = [x_ref[rows, :] for rows in groups]
        us = [_dot(_bf(x), w_in_ref[...]) for x in xs]
        for rows, u in zip(groups, us):
            conv_in = u[:, :c] * _sigmoid(u[:, c:2 * c]) if is_a else u[:, c:2 * c] * u[:, 2 * c:3 * c]
            buf_ref[pl.ds(HIST_PAD + rows.start, rg), :] = conv_in

        head = lambda ref: lambda h: ref[0, 0, :, h * XHEAD_DIM:(h + 1) * XHEAD_DIM]
        for rows, x, u in zip(groups, xs, us):
            base = HIST_PAD - (width - 1) + rows.start
            conv = None
            for phase in range(V7X_SUBLANES):
                taps = [t for t in range(width) if (base + t) % V7X_SUBLANES == phase]
                if not taps:
                    continue
                n = rg if phase == 0 else rg + V7X_SUBLANES
                part = None
                for t in taps:
                    term = cw_ref[t:t + 1, :] * buf_ref[pl.ds(base + t - phase, n), :]
                    part = term if part is None else part + term
                part = part[phase:phase + rg, :]
                conv = part if conv is None else conv + part

            if is_a:
                mix = _silu(_layer_norm(conv + cb_ref[...], ng_ref[...], nb_ref[...]))
                q = u[:, 2 * c:]
            else:
                mix = u[:, :c] * conv
                q = u[:, 3 * c:]
            attn = _memory_attention(q, head(k_ref), head(v_ref))
            out = _dot(_bf(jnp.concatenate([mix, attn], axis=-1)), w_out_ref[...])
            y_ref[rows, :] = _layer_norm(DEEPNORM_ALPHA * x + out, g1_ref[...], b1_ref[...])

    @pl.when((step < n_tiles) & (seq_step == nl - 1))
    def _():
        hist_ref[0] = buf_ref[pl.ds(HIST_PAD + tl - (width - 1), width - 1), :]

    @pl.when(step < n_tiles)
    def _():
        buf_ref[pl.ds(0, HIST_PAD), :] = buf_ref[pl.ds(tl, HIST_PAD), :]


def _prompt_token_sublayer(is_a, layer, x, sample_rows, batch, seq, w_in, cw, cb, ng, nb, k, v, w_out, g1, b1):
    tl = PROMPT_SEQ_TILE
    assert sample_rows.shape == (tl, D_MODEL)
    nl = seq // tl
    n_tiles = batch * nl
    c = MIX_WIDTH
    width = CONV_A_WIDTH if is_a else CONV_B_WIDTH
    n_in = w_in.shape[1]
    tile = lambda s: jnp.minimum(s, n_tiles - 1)
    mem_spec = pl.BlockSpec((1, 1, N_MEM, XATTN_WIDTH), lambda s: (layer, tile(s) // nl, 0, 0))
    return pl.pallas_call(
        functools.partial(_prompt_token_body, is_a, nl, n_tiles),
        grid=(n_tiles + 1,),
        in_specs=[pl.BlockSpec((tl, D_MODEL), lambda s: (tile(s), 0)),
                  _full((D_MODEL, n_in)), _full((width, c)), _full((1, c)), _full((1, c)), _full((1, c)),
                  mem_spec, mem_spec,
                  _full((c + XATTN_WIDTH, D_MODEL)), _full((1, D_MODEL)), _full((1, D_MODEL)),
                  _full((tl, D_MODEL))],
        out_specs=[pl.BlockSpec((tl, D_MODEL), lambda s: (s, 0)),
                   pl.BlockSpec((1, width - 1, c), lambda s: (tile(s) // nl, 0, 0))],
        out_shape=[jax.ShapeDtypeStruct(((n_tiles + 1) * tl, D_MODEL), jnp.float32),
                   jax.ShapeDtypeStruct((batch, width - 1, c), jnp.float32)],
        scratch_shapes=[pltpu.VMEM((HIST_PAD + tl, c), jnp.float32)],
        compiler_params=pltpu.CompilerParams(dimension_semantics=("arbitrary",),
                                             vmem_limit_bytes=_vmem_limit(48 * 1024 * 1024)),
        name="prompt_token_a" if is_a else "prompt_token_b",
    )(x, w_in, cw, cb, ng, nb, k, v, w_out, g1, b1, sample_rows)


def _sample_mix_body(is_a, n_seq, x_ref, w_in_ref, hist_ref, cw_ref, cb_ref, ng_ref, nb_ref,
                     mix_ref, q_ref, new_hist_ref):
    c = MIX_WIDTH
    width = CONV_A_WIDTH if is_a else CONV_B_WIDTH
    n_hist = width - 1
    n_pos = x_ref.shape[0] // n_seq
    u = _dot(_bf(x_ref[...]), w_in_ref[...])
    if is_a:
        conv_in = u[:, :c] * _sigmoid(u[:, c:2 * c])
        q_ref[...] = u[:, 2 * c:]
    else:
        conv_in = u[:, c:2 * c] * u[:, 2 * c:3 * c]
        q_ref[...] = u[:, 3 * c:]

    def full_row(j):
        if j < n_hist:
            return hist_ref[j]
        return conv_in[(j - n_hist) * n_seq:(j - n_hist + 1) * n_seq, :]

    for l in range(n_pos):
        conv = cw_ref[0:1, :] * full_row(l)
        for t in range(1, width):
            conv = conv + cw_ref[t:t + 1, :] * full_row(l + t)
        rows = slice(l * n_seq, (l + 1) * n_seq)
        if is_a:
            mix_ref[rows, :] = _silu(_layer_norm(conv + cb_ref[...], ng_ref[...], nb_ref[...]))
        else:
            mix_ref[rows, :] = u[rows, :c] * conv
    for j in range(n_hist):
        new_hist_ref[j] = full_row(j + n_pos)


def _sample_mix(is_a, n_seq, rows, x, x_block, w_in, hist, cw, cb, ng, nb):
    c = MIX_WIDTH
    small = (w_in, hist, cw, cb, ng, nb)
    return pl.pallas_call(
        functools.partial(_sample_mix_body, is_a, n_seq),
        grid=(1,),
        in_specs=[pl.BlockSpec((rows, D_MODEL), lambda i: (x_block, 0))] + [_full(a.shape) for a in small],
        out_specs=[_full((rows, c)), _full((rows, XATTN_WIDTH)), _full(hist.shape)],
        out_shape=[jax.ShapeDtypeStruct((rows, c), jnp.float32),
                   jax.ShapeDtypeStruct((rows, XATTN_WIDTH), jnp.float32),
                   jax.ShapeDtypeStruct(hist.shape, jnp.float32)],
        compiler_params=pltpu.CompilerParams(dimension_semantics=("arbitrary",),
                                             vmem_limit_bytes=_vmem_limit(48 * 1024 * 1024)),
        name="sample_mix_a" if is_a else "sample_mix_b",
    )(x, *small)


def _sample_attn_body(n_seq, q_ref, k_ref, v_ref, o_ref, bias_ref):
    bb = k_ref.shape[1]
    n_pos = q_ref.shape[0] // n_seq
    first = pl.multiple_of(pl.program_id(0) * bb, bb)

    @pl.when(pl.program_id(0) == 0)
    def _():
        r = lax.broadcasted_iota(jnp.int32, bias_ref.shape, 0)
        col = lax.broadcasted_iota(jnp.int32, bias_ref.shape, 1)
        valid = ((r % N_XHEADS) == (col // (n_pos * bb))) & ((r // (N_MEM * N_XHEADS)) == (col % bb))
        bias_ref[...] = jnp.where(valid, 0.0, NEG_INF)

    q = jnp.concatenate([q_ref[pl.ds(l * n_seq + first, bb), :] for l in range(n_pos)], axis=0)
    nq = n_pos * bb
    n_rows = bb * N_MEM * N_XHEADS
    k_rows = k_ref[0].reshape(n_rows, XHEAD_DIM)
    v_rows = v_ref[0].reshape(n_rows, XHEAD_DIM)
    q_heads = jnp.concatenate([q[:, h * XHEAD_DIM:(h + 1) * XHEAD_DIM] for h in range(N_XHEADS)], axis=0)
    s = _dot_nt(_bf(k_rows), _bf(q_heads)) * (XHEAD_DIM ** -0.5) + bias_ref[...]
    e = jnp.exp(s - jnp.max(s, axis=0, keepdims=True))
    p = e / jnp.sum(e, axis=0, keepdims=True)
    o_heads = _dot_tn(_bf(p), _bf(v_rows))
    o = jnp.concatenate([o_heads[h * nq:(h + 1) * nq] for h in range(N_XHEADS)], axis=1)
    for l in range(n_pos):
        o_ref[pl.ds(l * n_seq + first, bb), :] = o[l * bb:(l + 1) * bb, :]


def _sample_attention(layer, n_seq, q, mem_k, mem_v):
    rows = q.shape[0]
    bb = SAMPLE_BATCH_BLOCK
    mem_spec = pl.BlockSpec((1, bb, N_MEM, N_XHEADS, XHEAD_DIM), lambda i: (layer, i, 0, 0, 0))
    return pl.pallas_call(
        functools.partial(_sample_attn_body, n_seq),
        grid=(n_seq // bb,),
        in_specs=[pl.BlockSpec((rows, XATTN_WIDTH), lambda i: (0, 0)), mem_spec, mem_spec],
        out_specs=pl.BlockSpec((rows, XATTN_WIDTH), lambda i: (0, 0)),
        out_shape=jax.ShapeDtypeStruct((rows, XATTN_WIDTH), jnp.float32),
        scratch_shapes=[pltpu.VMEM((bb * N_MEM * N_XHEADS, N_XHEADS * (rows // n_seq) * bb), jnp.float32)],
        compiler_params=pltpu.CompilerParams(dimension_semantics=("arbitrary",),
                                             vmem_limit_bytes=_vmem_limit(40 * 1024 * 1024)),
        name="sample_attention",
    )(q, mem_k, mem_v)


def _sample_out_body(x_ref, mix_ref, attn_ref, w_out_ref, g1_ref, b1_ref, y_ref):
    cat = jnp.concatenate([mix_ref[...], attn_ref[...]], axis=-1)
    out = _dot(_bf(cat), w_out_ref[...])
    y_ref[...] = _layer_norm(DEEPNORM_ALPHA * x_ref[...] + out, g1_ref[...], b1_ref[...])


def _sample_out(rows, x, x_block, mix, attn, w_out, g1, b1):
    small = (mix, attn, w_out, g1, b1)
    return pl.pallas_call(
        _sample_out_body,
        grid=(1,),
        in_specs=[pl.BlockSpec((rows, D_MODEL), lambda i: (x_block, 0))] + [_full(a.shape) for a in small],
        out_specs=_full((rows, D_MODEL)),
        out_shape=jax.ShapeDtypeStruct((rows, D_MODEL), jnp.float32),
        compiler_params=pltpu.CompilerParams(dimension_semantics=("arbitrary",),
                                             vmem_limit_bytes=_vmem_limit(32 * 1024 * 1024)),
        name="sample_out",
    )(x, *small)


def _first_index_of(mask, index, n):
    cand = jnp.where(mask, index, float(n))
    while cand.ndim > 2:
        cand = jnp.min(cand, axis=0)
    return jnp.min(cand, axis=0, keepdims=True)


def _max_all(x):
    while x.ndim > 2:
        x = jnp.max(x, axis=0)
    return jnp.max(x, axis=0, keepdims=True)


def _route_gates(logits, bias):
    tm = logits.shape[1]
    scores = _sigmoid(logits)
    biased = scores + bias
    grp = biased.reshape(N_GROUPS, GROUP_SIZE, tm)

    within = lax.broadcasted_iota(jnp.int32, grp.shape, 1).astype(jnp.float32)
    top1 = jnp.max(grp, axis=1, keepdims=True)
    first = jnp.min(jnp.where(grp == top1, within, float(GROUP_SIZE)), axis=1, keepdims=True)
    top2 = jnp.max(jnp.where(within == first, NEG_INF, grp), axis=1, keepdims=True)
    grp_score = (top1 + top2).reshape(N_GROUPS, tm)

    gidx = lax.broadcasted_iota(jnp.int32, grp_score.shape, 0).astype(jnp.float32)
    grp_sel = jnp.zeros(grp_score.shape, jnp.float32)
    for _ in range(TOPK_GROUPS):
        best = jnp.max(grp_score, axis=0, keepdims=True)
        pick = gidx == _first_index_of(grp_score == best, gidx, N_GROUPS)
        grp_sel = jnp.where(pick, 1.0, grp_sel)
        grp_score = jnp.where(pick, NEG_INF, grp_score)

    eidx = (lax.broadcasted_iota(jnp.int32, grp.shape, 0) * GROUP_SIZE
            + lax.broadcasted_iota(jnp.int32, grp.shape, 1)).astype(jnp.float32)
    cand = jnp.where(grp_sel.reshape(N_GROUPS, 1, tm) > 0.0, grp, NEG_INF)
    chosen = jnp.zeros(grp.shape, jnp.float32)
    for _ in range(TOP_K):
        best = _max_all(cand).reshape(1, 1, tm)
        pick = eidx == _first_index_of(cand == best, eidx, N_EXPERTS).reshape(1, 1, tm)
        chosen = jnp.where(pick, 1.0, chosen)
        cand = jnp.where(pick, NEG_INF, cand)

    w = jnp.where(chosen > 0.0, scores.reshape(grp.shape), 0.0)
    total = jnp.sum(jnp.sum(w, axis=0), axis=0, keepdims=True).reshape(1, 1, tm)
    gates = (w / total * ROUTED_SCALE).reshape(N_EXPERTS, tm)
    return chosen.reshape(N_EXPERTS, tm), gates


def _slot_matches(sel_row, first_slot=0, n_slots=None):
    tm = sel_row.shape[1]
    n_slots = SLOT_WINDOW if n_slots is None else n_slots
    slot = (lax.broadcasted_iota(jnp.int32, (n_slots, tm), 0) + first_slot).astype(jnp.float32)
    return jnp.broadcast_to(sel_row, (n_slots, tm)) == slot


def _slot_onehot(sel, first_expert):
    rows = [_slot_matches(sel[e:e + 1, :]) for e in range(first_expert, first_expert + EXPERT_CHUNK)]
    return jnp.where(jnp.concatenate(rows, axis=0), 1.0, 0.0).astype(jnp.bfloat16)


def _route_body(x_ref, wr_ref, bias_ref, xs_ref, sel_ref, gate_ref, over_ref, flag_ref, fill_ref):
    tm = TOKEN_TILE
    n_sub = x_ref.shape[0] // tm
    n_groups = SLOT_WINDOW // SLOT_GROUP
    any_over = None
    fills = []
    for t in range(n_sub):
        cols = slice(t * tm, (t + 1) * tm)
        xb = _bf(x_ref[cols, :])
        chosen, gates = _route_gates(_dot_nt(wr_ref[...], xb), bias_ref[...])

        earlier = (lax.broadcasted_iota(jnp.int32, (tm, tm), 0) < lax.broadcasted_iota(jnp.int32, (tm, tm), 1))
        pos = _dot(_bf(chosen), jnp.where(earlier, 1.0, 0.0).astype(jnp.bfloat16))
        routed = chosen > 0.0
        in_window = routed & (pos < float(SLOT_WINDOW))
        sel = jnp.where(in_window, pos, -1.0)
        sel_ref[:, cols] = sel
        gate_ref[:, cols] = jnp.where(in_window, gates, 0.0)
        fills.append(jnp.max(sel, axis=1, keepdims=True))

        over = jnp.where(routed & (pos >= float(SLOT_WINDOW)), gates, 0.0)
        pad = jnp.zeros((GATE_LANES - N_EXPERTS, tm), jnp.float32)
        over_ref[cols, :] = jnp.concatenate([over, pad], axis=0).T
        tile_over = jnp.max(jnp.max(over, axis=0, keepdims=True), axis=1, keepdims=True)
        any_over = tile_over if any_over is None else jnp.maximum(any_over, tile_over)

        for first in range(0, N_EXPERTS, EXPERT_CHUNK):
            slots = _bf(_dot(_slot_onehot(sel, first), xb))
            xs_ref[first:first + EXPERT_CHUNK, :, t] = slots.reshape(EXPERT_CHUNK, n_groups, SLOT_GROUP, D_MODEL)
    flag_ref[...] = jnp.broadcast_to(any_over, flag_ref.shape[1:])[None]
    lane = lax.broadcasted_iota(jnp.int32, (N_EXPERTS, V7X_LANES), 1)
    fill = jnp.full((N_EXPERTS, V7X_LANES), -1.0, jnp.float32)
    for t in range(n_sub):
        fill = jnp.where(lane == t, fills[t], fill)
    fill_ref[...] = fill[None]


def _route_dispatch(x, w_router_t, bias_col):
    rows = x.shape[0]
    n_sub = ROUTE_TILES_PER_STEP
    tm = TOKEN_TILE * n_sub
    n_steps = rows // tm
    n_groups = SLOT_WINDOW // SLOT_GROUP
    return pl.pallas_call(
        _route_body,
        grid=(n_steps,),
        in_specs=[pl.BlockSpec((tm, D_MODEL), lambda i: (i, 0)),
                  _full((N_EXPERTS, D_MODEL)), _full((N_EXPERTS, 1))],
        out_specs=[pl.BlockSpec((N_EXPERTS, n_groups, n_sub, SLOT_GROUP, D_MODEL), lambda i: (0, 0, i, 0, 0)),
                   pl.BlockSpec((N_EXPERTS, tm), lambda i: (0, i)),
                   pl.BlockSpec((N_EXPERTS, tm), lambda i: (0, i)),
                   pl.BlockSpec((tm, GATE_LANES), lambda i: (i, 0)),
                   pl.BlockSpec((1, 8, V7X_LANES), lambda i: (i, 0, 0)),
                   pl.BlockSpec((1, N_EXPERTS, V7X_LANES), lambda i: (i, 0, 0))],
        out_shape=[jax.ShapeDtypeStruct((N_EXPERTS, n_groups, n_steps * n_sub, SLOT_GROUP, D_MODEL), jnp.bfloat16),
                   jax.ShapeDtypeStruct((N_EXPERTS, rows), jnp.float32),
                   jax.ShapeDtypeStruct((N_EXPERTS, rows), jnp.float32),
                   jax.ShapeDtypeStruct((rows, GATE_LANES), jnp.float32),
                   jax.ShapeDtypeStruct((n_steps, 8, V7X_LANES), jnp.float32),
                   jax.ShapeDtypeStruct((n_steps, N_EXPERTS, V7X_LANES), jnp.float32)],
        compiler_params=pltpu.CompilerParams(dimension_semantics=("arbitrary",),
                                             vmem_limit_bytes=_vmem_limit(48 * 1024 * 1024)),
        name="route_dispatch",
    )(x, w_router_t, bias_col)


def _expert_body(need_ref, last_e_ref, last_c_ref, xm_ref, xt_ref, sel_ref, gate_ref, wg_ref, wu_ref, wd_ref,
                 ym_ref, yt_ref, wg_bf, wu_bf, wd_bf):
    del last_e_ref, last_c_ref

    @pl.when(pl.program_id(1) == 0)
    def _():
        wg_bf[...] = _bf(wg_ref[0, 0])
        wu_bf[...] = _bf(wu_ref[0, 0])
        wd_bf[...] = _bf(wd_ref[0, 0])

    row = pl.ds(pl.program_id(0) % V7X_SUBLANES, 1)
    n_main = xm_ref.shape[1]
    n_tiles = xm_ref.shape[2]
    tm = sel_ref.shape[1] // n_tiles

    def run(x_ref, y_ref, first_group):
        n_g = x_ref.shape[1]
        rows = n_tiles * SLOT_GROUP
        proj = None
        for g in range(n_g + 1):
            nxt = None
            if g < n_g:
                x = x_ref[0, g].reshape(rows, D_MODEL)
                nxt = (_dot(x, wg_bf[...]), _dot(x, wu_bf[...]))
            if proj is not None:
                slot_gates = []
                for t in range(n_tiles):
                    cols = slice(t * tm, (t + 1) * tm)
                    match = _slot_matches(sel_ref[row, cols], (first_group + g - 1) * SLOT_GROUP, SLOT_GROUP)
                    slot_gates.append(jnp.sum(jnp.where(match, gate_ref[row, cols], 0.0), axis=-1, keepdims=True))
                gate = jnp.concatenate(slot_gates, axis=0)
                h = _silu(proj[0]) * proj[1]
                y = _bf(_dot(_bf(h), wd_bf[...]) * gate)
                y_ref[0, g - 1] = y.reshape(n_tiles, SLOT_GROUP, D_MODEL)
            proj = nxt

    run(xm_ref, ym_ref, 0)

    @pl.when(need_ref[pl.program_id(0) * pl.num_programs(1) + pl.program_id(1)] > 0)
    def _():
        run(xt_ref, yt_ref, n_main)


def _slot_group_split():
    n_groups = SLOT_WINDOW // SLOT_GROUP
    n_main = min(SLOT_GROUPS_ALWAYS, n_groups - 1)
    n_tail = n_groups - n_main
    assert n_main % n_tail == 0, "the trailing groups must form one block of the group axis"
    return n_main, n_tail


def _experts(layer, xs, sel, gate, fill, w_gate, w_up, w_down):
    n_e, _, n_tiles_all, _, _ = xs.shape
    n_chunks = EXPERT_ROW_CHUNKS
    n_tiles = n_tiles_all // n_chunks
    tokens = sel.shape[1] // n_chunks
    n_main, n_tail = _slot_group_split()

    need = (jnp.max(fill.reshape(n_e, n_chunks, n_tiles), axis=-1) >= n_main * SLOT_GROUP).astype(jnp.int32)
    steps = jnp.arange(n_e * n_chunks, dtype=jnp.int32)
    last = jnp.maximum(lax.cummax(jnp.where(need.reshape(-1) > 0, steps, -1)), 0)
    last_e, last_c = last // n_chunks, last % n_chunks

    w_in_spec = pl.BlockSpec((1, 1, D_MODEL, EXPERT_FF), lambda e, c, *_: (layer, e, 0, 0))
    route_spec = pl.BlockSpec((V7X_SUBLANES, tokens), lambda e, c, *_: (e // V7X_SUBLANES, c))
    main_spec = pl.BlockSpec((1, n_main, n_tiles, SLOT_GROUP, D_MODEL), lambda e, c, *_: (e, 0, c, 0, 0))

    def tail_map(group_block):
        return lambda e, c, nd, le, lc: (le[e * n_chunks + c], group_block, lc[e * n_chunks + c], 0, 0)

    tail_block = (1, n_tail, n_tiles, SLOT_GROUP, D_MODEL)
    ym, yt = pl.pallas_call(
        _expert_body,
        grid_spec=pltpu.PrefetchScalarGridSpec(
            num_scalar_prefetch=3,
            grid=(n_e, n_chunks),
            in_specs=[main_spec, pl.BlockSpec(tail_block, tail_map(n_main // n_tail)),
                      route_spec, route_spec, w_in_spec, w_in_spec,
                      pl.BlockSpec((1, 1, EXPERT_FF, D_MODEL), lambda e, c, *_: (layer, e, 0, 0))],
            out_specs=[main_spec, pl.BlockSpec(tail_block, tail_map(0))],
            scratch_shapes=[pltpu.VMEM((D_MODEL, EXPERT_FF), jnp.bfloat16),
                            pltpu.VMEM((D_MODEL, EXPERT_FF), jnp.bfloat16),
                            pltpu.VMEM((EXPERT_FF, D_MODEL), jnp.bfloat16)],
        ),
        out_shape=[jax.ShapeDtypeStruct((n_e, n_main, n_tiles_all, SLOT_GROUP, D_MODEL), jnp.bfloat16),
                   jax.ShapeDtypeStruct((n_e, n_tail, n_tiles_all, SLOT_GROUP, D_MODEL), jnp.bfloat16)],
        compiler_params=pltpu.CompilerParams(dimension_semantics=("arbitrary", "arbitrary"),
                                             vmem_limit_bytes=_vmem_limit(40 * 1024 * 1024)),
        name="experts",
    )(need.reshape(-1), last_e, last_c, xs, xs, sel, gate, w_gate, w_up, w_down)
    return ym, yt, need


def _combine_body(has_extra, n_prompt_tiles, tiles_per_chunk, tile_need_ref, last_ref, need_ref,
                  x_ref, sel_ref, ym_ref, yt_ref, wsg_ref, wsu_ref, wsd_ref, g2_ref, b2_ref, *rest):
    del last_ref
    rest = list(rest)
    extra_ref = rest.pop(0) if has_extra else None
    out_refs = [rest.pop(0)] if n_prompt_tiles is None else [rest.pop(0), rest.pop(0)]
    wsg_bf, wsu_bf, wsd_bf, acc_ref = rest
    tile = pl.program_id(0)
    n_main, n_tail = ym_ref.shape[1], yt_ref.shape[1]
    n_chunks = need_ref.shape[0] // N_EXPERTS

    @pl.when(tile == 0)
    def _():
        wsg_bf[...] = _bf(wsg_ref[0])
        wsu_bf[...] = _bf(wsu_ref[0])
        wsd_bf[...] = _bf(wsd_ref[0])

    def onehot(first_expert, first_group, n_g):
        rows = [_slot_matches(sel_ref[e:e + 1, :], first_group * SLOT_GROUP, n_g * SLOT_GROUP)
                for e in range(first_expert, first_expert + COMBINE_CHUNK)]
        return jnp.where(jnp.concatenate(rows, axis=0), 1.0, 0.0).astype(jnp.bfloat16)

    x = x_ref[...]
    xb = _bf(x)
    y = _dot(_bf(_silu(_dot(xb, wsg_bf[...])) * _dot(xb, wsu_bf[...])), wsd_bf[...])
    for first in range(0, N_EXPERTS, COMBINE_CHUNK):
        ys = ym_ref[first:first + COMBINE_CHUNK, :, 0].reshape(COMBINE_CHUNK * n_main * SLOT_GROUP, D_MODEL)
        y = y + _dot_tn(onehot(first, 0, n_main), ys)
    if has_extra:
        y = y + extra_ref[...]
    acc_ref[...] = y

    @pl.when(tile_need_ref[tile] > 0)
    def _():
        chunk = tile // tiles_per_chunk
        part = jnp.zeros(acc_ref.shape, jnp.float32)
        for first in range(0, N_EXPERTS, COMBINE_CHUNK):
            rows = []
            for e in range(first, first + COMBINE_CHUNK):
                ye = yt_ref[e, :, 0].reshape(n_tail * SLOT_GROUP, D_MODEL)
                rows.append(jnp.where(need_ref[e * n_chunks + chunk] > 0, ye, jnp.zeros_like(ye)))
            part = part + _dot_tn(onehot(first, n_main, n_tail), jnp.concatenate(rows, axis=0))
        acc_ref[...] += part

    y = _layer_norm(DEEPNORM_ALPHA * x + acc_ref[...], g2_ref[...], b2_ref[...])

    if n_prompt_tiles is None:
        out_refs[0][...] = y
    else:
        @pl.when(tile < n_prompt_tiles)
        def _():
            out_refs[0][...] = y

        @pl.when(tile >= n_prompt_tiles)
        def _():
            out_refs[1][...] = y


def _combine(layer, x, sel, fill, ym, yt, need, ws_gate, ws_up, ws_down, g2, b2, extra=None, prompt_rows=None):
    rows = x.shape[0]
    tm = TOKEN_TILE
    n_tiles = rows // tm
    has_extra = extra is not None
    n_main, n_tail = ym.shape[1], yt.shape[1]
    tile_need = (jnp.max(fill, axis=0) >= n_main * SLOT_GROUP).astype(jnp.int32)
    tiles = jnp.arange(n_tiles, dtype=jnp.int32)
    last = jnp.maximum(lax.cummax(jnp.where(tile_need > 0, tiles, -1)), 0)

    in_specs = [pl.BlockSpec((tm, D_MODEL), lambda i, *_: (i, 0)),
                pl.BlockSpec((N_EXPERTS, tm), lambda i, *_: (0, i)),
                pl.BlockSpec((N_EXPERTS, n_main, 1, SLOT_GROUP, D_MODEL), lambda i, *_: (0, 0, i, 0, 0)),
                pl.BlockSpec((N_EXPERTS, n_tail, 1, SLOT_GROUP, D_MODEL), lambda i, tn, la, nd: (0, 0, la[i], 0, 0)),
                pl.BlockSpec((1, D_MODEL, EXPERT_FF), lambda i, *_: (layer, 0, 0)),
                pl.BlockSpec((1, D_MODEL, EXPERT_FF), lambda i, *_: (layer, 0, 0)),
                pl.BlockSpec((1, EXPERT_FF, D_MODEL), lambda i, *_: (layer, 0, 0)),
                pl.BlockSpec((1, D_MODEL), lambda i, *_: (0, 0)), pl.BlockSpec((1, D_MODEL), lambda i, *_: (0, 0))]
    args = [x, sel, ym, yt, ws_gate, ws_up, ws_down, g2, b2]
    if has_extra:
        in_specs.append(pl.BlockSpec((tm, D_MODEL), lambda i, *_: (i, 0)))
        args.append(extra)
    if prompt_rows is None:
        n_prompt_tiles = None
        out_specs = pl.BlockSpec((tm, D_MODEL), lambda i, *_: (i, 0))
        out_shape = jax.ShapeDtypeStruct((rows, D_MODEL), jnp.float32)
    else:
        n_prompt_tiles = prompt_rows // tm
        out_specs = [pl.BlockSpec((tm, D_MODEL), lambda i, *_: (jnp.minimum(i, n_prompt_tiles - 1), 0)),
                     pl.BlockSpec((tm, D_MODEL), lambda i, *_: (jnp.maximum(i - n_prompt_tiles, 0), 0))]
        out_shape = [jax.ShapeDtypeStruct((prompt_rows, D_MODEL), jnp.float32),
                     jax.ShapeDtypeStruct((rows - prompt_rows, D_MODEL), jnp.float32)]
    return pl.pallas_call(
        functools.partial(_combine_body, has_extra, n_prompt_tiles, n_tiles // need.shape[1]),
        grid_spec=pltpu.PrefetchScalarGridSpec(
            num_scalar_prefetch=3,
            grid=(n_tiles,),
            in_specs=in_specs,
            out_specs=out_specs,
            scratch_shapes=[pltpu.VMEM((D_MODEL, EXPERT_FF), jnp.bfloat16),
                            pltpu.VMEM((D_MODEL, EXPERT_FF), jnp.bfloat16),
                            pltpu.VMEM((EXPERT_FF, D_MODEL), jnp.bfloat16),
                            pltpu.VMEM((tm, D_MODEL), jnp.float32)],
        ),
        out_shape=out_shape,
        compiler_params=pltpu.CompilerParams(dimension_semantics=("arbitrary",),
                                             vmem_limit_bytes=_vmem_limit(48 * 1024 * 1024)),
        name="combine_extra" if has_extra else "combine",
    )(tile_need, last, need.reshape(-1), *args)


def _dense_body(x_ref, gate_ref, wg_ref, wu_ref, wd_ref, y_ref, xb_ref):
    e = pl.program_id(1)

    @pl.when(e == 0)
    def _():
        xb_ref[...] = _bf(x_ref[...])
        y_ref[...] = jnp.zeros(y_ref.shape, jnp.float32)

    xb = xb_ref[...]
    h = _silu(_dot(xb, _bf(wg_ref[0, 0]))) * _dot(xb, _bf(wu_ref[0, 0]))
    down = _dot(_bf(h), _bf(wd_ref[0, 0]))
    lane = lax.broadcasted_iota(jnp.int32, gate_ref.shape, 1)
    gate = jnp.sum(jnp.where(lane == e, gate_ref[...], 0.0), axis=-1, keepdims=True)
    y_ref[...] += down * gate


def _dense_experts(layer, x, gates, w_gate, w_up, w_down):
    rows = x.shape[0]
    tm = DENSE_TILE
    w_in_spec = pl.BlockSpec((1, 1, D_MODEL, EXPERT_FF), lambda i, e: (layer, e, 0, 0))
    return pl.pallas_call(
        _dense_body,
        grid=(rows // tm, N_EXPERTS),
        in_specs=[pl.BlockSpec((tm, D_MODEL), lambda i, e: (i, 0)),
                  pl.BlockSpec((tm, GATE_LANES), lambda i, e: (i, 0)),
                  w_in_spec, w_in_spec,
                  pl.BlockSpec((1, 1, EXPERT_FF, D_MODEL), lambda i, e: (layer, e, 0, 0))],
        out_specs=pl.BlockSpec((tm, D_MODEL), lambda i, e: (i, 0)),
        out_shape=jax.ShapeDtypeStruct((rows, D_MODEL), jnp.float32),
        scratch_shapes=[pltpu.VMEM((tm, D_MODEL), jnp.bfloat16)],
        compiler_params=pltpu.CompilerParams(dimension_semantics=("arbitrary", "arbitrary"),
                                             vmem_limit_bytes=_vmem_limit(40 * 1024 * 1024)),
        name="dense_overflow",
    )(x, gates, w_gate, w_up, w_down)


def _channel_sublayer(layer, x, w_router_t, bias_col, w_gate, w_up, w_down, ws_gate, ws_up, ws_down, g2, b2,
                      prompt_rows=None):
    xs, sel, gate, over, flags, fill = _route_dispatch(x, w_router_t, bias_col)
    fill = fill[:, :, :ROUTE_TILES_PER_STEP].transpose(1, 0, 2).reshape(N_EXPERTS, -1)
    ym, yt, need = _experts(layer, xs, sel, gate, fill, w_gate, w_up, w_down)
    rest = (x, sel, fill, ym, yt, need, ws_gate, ws_up, ws_down, g2, b2)

    def with_overflow():
        extra = _dense_experts(layer, x, over, w_gate, w_up, w_down)
        return _combine(layer, *rest, extra=extra, prompt_rows=prompt_rows)

    def without_overflow():
        return _combine(layer, *rest, prompt_rows=prompt_rows)

    return lax.cond(jnp.max(flags) > 0.0, with_overflow, without_overflow)


def kernel(x_prompt, x_sample, mem_prompt, cache_mem_k, cache_mem_v, state_conv_a, state_conv_b, w_in_a, conv_a_w, conv_a_b, norm_a_g, norm_a_b, w_in_b, conv_b_w, w_kv, w_out, ln1_g, ln1_b, w_router, router_bias, w_gate, w_up, w_down, ws_gate, ws_up, ws_down, ln2_g, ln2_b):
    batch, seq, d = x_prompt.shape
    n_seq, n_pos, _ = x_sample.shape
    c = MIX_WIDTH
    p_rows, s_rows = batch * seq, n_pos * n_seq
    s_block = p_rows // s_rows
    row = lambda a: a.reshape(1, -1)

    x_p = x_prompt.reshape(p_rows, d)
    x_s, x_s_block = x_sample.transpose(1, 0, 2).reshape(s_rows, d), 0
    k_all, v_all = _kv_projection(mem_prompt.reshape(batch * N_MEM, d), w_kv)
    k_p = k_all.reshape(DEPTH, batch, N_MEM, XATTN_WIDTH)
    v_p = v_all.reshape(DEPTH, batch, N_MEM, XATTN_WIDTH)

    conv_a_p, conv_b_p, conv_a_s, conv_b_s = [], [], [], []
    for i in range(DEPTH):
        j = i // N_MIXERS
        is_a = i % N_MIXERS == 0
        if is_a:
            w_in, cw = _bf(w_in_a[j]), conv_a_w[j]
            cb, ng, nb = row(conv_a_b[j]), row(norm_a_g[j]), row(norm_a_b[j])
            hist_s = state_conv_a[j]
        else:
            w_in, cw = _bf(w_in_b[j]), conv_b_w[j]
            cb = ng = nb = jnp.zeros((1, c), jnp.float32)
            hist_s = state_conv_b[j]
        w_o = _bf(w_out[i])
        g1, b1 = row(ln1_g[i]), row(ln1_b[i])

        mix, q, hist_s_new = _sample_mix(is_a, n_seq, s_rows, x_s, x_s_block, w_in, hist_s.transpose(1, 0, 2),
                                         cw, cb, ng, nb)
        attn = _sample_attention(i, n_seq, q, cache_mem_k, cache_mem_v)
        h_s = _sample_out(s_rows, x_s, x_s_block, mix, attn, w_o, g1, b1)
        h, hist_p_new = _prompt_token_sublayer(is_a, i, x_p, h_s, batch, seq, w_in, cw, cb, ng, nb,
                                               k_p, v_p, w_o, g1, b1)
        hist_s_new = hist_s_new.transpose(1, 0, 2)
        if is_a:
            conv_a_p.append(hist_p_new)
            conv_a_s.append(hist_s_new)
        else:
            conv_b_p.append(hist_p_new)
            conv_b_s.append(hist_s_new)

        last = i == DEPTH - 1
        h = _channel_sublayer(i, h, _bf(w_router[i].T), router_bias[i].reshape(N_EXPERTS, 1),
                              w_gate, w_up, w_down, ws_gate, ws_up, ws_down, row(ln2_g[i]), row(ln2_b[i]),
                              prompt_rows=p_rows if last else None)
        if not last:
            x_p = h
            x_s, x_s_block = h, s_block

    y_p, y_s = h
    new_k = k_all.reshape(DEPTH, batch, N_MEM, N_XHEADS, XHEAD_DIM)
    new_v = v_all.reshape(DEPTH, batch, N_MEM, N_XHEADS, XHEAD_DIM)
    return (y_p.reshape(batch, seq, d), y_s.reshape(n_pos, n_seq, d).transpose(1, 0, 2), new_k, new_v,
            jnp.stack(conv_a_p), jnp.stack(conv_b_p), jnp.stack(conv_a_s), jnp.stack(conv_b_s))
```

```python
import functools

import jax
import jax.numpy as jnp
from jax import lax
from jax.experimental import pallas as pl
from jax.experimental.pallas import tpu as pltpu

D_MODEL = 1024
DEPTH = 2
N_MIXERS = 2
MIX_WIDTH = D_MODEL // 2
N_MEM = 256
N_XHEADS = 4
XHEAD_DIM = MIX_WIDTH // N_XHEADS
XATTN_WIDTH = N_XHEADS * XHEAD_DIM
CONV_A_WIDTH = 31
CONV_B_WIDTH = 3
N_EXPERTS = 64
TOP_K = 8
N_GROUPS = 8
GROUP_SIZE = N_EXPERTS // N_GROUPS
TOPK_GROUPS = 4
EXPERT_FF = D_MODEL // 4
ROUTED_SCALE = 2.5
LN_EPS = 1e-5
DEEPNORM_ALPHA = (2 * DEPTH) ** 0.25

V7X_LANES = 128
V7X_SUBLANES = 8
V7X_VMEM_BYTES = 64 * 1024 * 1024

HIST_PAD = 32
PROMPT_SEQ_TILE = 512
PROMPT_ROW_GROUPS = 1
SAMPLE_BATCH_BLOCK = 8
TOKEN_TILE = 256
SLOT_WINDOW = 64
SLOT_GROUP = 16
SLOT_GROUPS_ALWAYS = 3
EXPERT_CHUNK = 8
COMBINE_CHUNK = 16
EXPERT_ROW_CHUNKS = 2
GATE_LANES = V7X_LANES
DENSE_TILE = 512
NEG_INF = float("-inf")


def _vmem_limit(nbytes):
    return int(min(max(nbytes, 16 * 1024 * 1024), V7X_VMEM_BYTES - 8 * 1024 * 1024))


def _bf(x):
    return x.astype(jnp.bfloat16)


def _dot(a, b):
    return jnp.dot(a, b, preferred_element_type=jnp.float32)


def _dot_nt(a, b):
    return lax.dot_general(a, b, (((1,), (1,)), ((), ())), preferred_element_type=jnp.float32)


def _dot_tn(a, b):
    return lax.dot_general(a, b, (((0,), (0,)), ((), ())), preferred_element_type=jnp.float32)


def _sigmoid(x):
    return 1.0 / (1.0 + jnp.exp(-x))


def _silu(x):
    return x * _sigmoid(x)


def _layer_norm(x, g, b):
    mu = jnp.mean(x, axis=-1, keepdims=True)
    xc = x - mu
    var = jnp.mean(xc * xc, axis=-1, keepdims=True)
    return xc * lax.rsqrt(var + LN_EPS) * g + b


def _memory_attention(q, k_head, v_head):
    outs = []
    for h in range(N_XHEADS):
        sl = slice(h * XHEAD_DIM, (h + 1) * XHEAD_DIM)
        s = _dot_nt(_bf(q[:, sl]), _bf(k_head(h))) * (XHEAD_DIM ** -0.5)
        e = jnp.exp(s - jnp.max(s, axis=-1, keepdims=True))
        p = e / jnp.sum(e, axis=-1, keepdims=True)
        outs.append(_dot(_bf(p), _bf(v_head(h))))
    return jnp.concatenate(outs, axis=-1)


def _full(shape):
    return pl.BlockSpec(shape, lambda *_: tuple(0 for _ in shape))


def _kv_body(mem_ref, w_ref, k_ref, v_ref):
    kv = _dot(_bf(mem_ref[...]), _bf(w_ref[0]))
    k_ref[0] = kv[:, :XATTN_WIDTH]
    v_ref[0] = kv[:, XATTN_WIDTH:]


def _kv_projection(mem2d, w_kv):
    rows = mem2d.shape[0]
    tm = 512
    out = jax.ShapeDtypeStruct((DEPTH, rows, XATTN_WIDTH), jnp.float32)
    return pl.pallas_call(
        _kv_body,
        grid=(DEPTH, rows // tm),
        in_specs=[pl.BlockSpec((tm, D_MODEL), lambda i, m: (m, 0)),
                  pl.BlockSpec((1, D_MODEL, 2 * XATTN_WIDTH), lambda i, m: (i, 0, 0))],
        out_specs=[pl.BlockSpec((1, tm, XATTN_WIDTH), lambda i, m: (i, m, 0)),
                   pl.BlockSpec((1, tm, XATTN_WIDTH), lambda i, m: (i, m, 0))],
        out_shape=[out, out],
        compiler_params=pltpu.CompilerParams(dimension_semantics=("arbitrary", "arbitrary"),
                                             vmem_limit_bytes=_vmem_limit(32 * 1024 * 1024)),
        name="kv_projection",
    )(mem2d, w_kv)


def _prompt_token_body(is_a, nl, n_tiles, x_ref, w_in_ref, cw_ref, cb_ref, ng_ref, nb_ref, k_ref, v_ref, w_out_ref,
                       g1_ref, b1_ref, sample_ref, y_ref, hist_ref, buf_ref):
    c = MIX_WIDTH
    tl = x_ref.shape[0]
    width = CONV_A_WIDTH if is_a else CONV_B_WIDTH
    step = pl.program_id(0)
    seq_step = step % nl

    @pl.when(step == n_tiles)
    def _():
        y_ref[...] = sample_ref[...]

    @pl.when((step < n_tiles) & (seq_step == 0))
    def _():
        buf_ref[pl.ds(0, HIST_PAD), :] = jnp.zeros((HIST_PAD, c), jnp.float32)

    @pl.when(step < n_tiles)
    def _():
        rg = tl // PROMPT_ROW_GROUPS
        groups = [slice(g * rg, (g + 1) * rg) for g in range(PROMPT_ROW_GROUPS)]
        xs = [x_ref[rows, :] for rows in groups]
        us = [_dot(_bf(x), w_in_ref[...]) for x in xs]
        for rows, u in zip(groups, us):
            conv_in = u[:, :c] * _sigmoid(u[:, c:2 * c]) if is_a else u[:, c:2 * c] * u[:, 2 * c:3 * c]
            buf_ref[pl.ds(HIST_PAD + rows.start, rg), :] = conv_in

        head = lambda ref: lambda h: ref[0, 0, :, h * XHEAD_DIM:(h + 1) * XHEAD_DIM]
        for rows, x, u in zip(groups, xs, us):
            base = HIST_PAD - (width - 1) + rows.start
            conv = None
            for phase in range(V7X_SUBLANES):
                taps = [t for t in range(width) if (base + t) % V7X_SUBLANES == phase]
                if not taps:
                    continue
                n = rg if phase == 0 else rg + V7X_SUBLANES
                part = None
                for t in taps:
                    term = cw_ref[t:t + 1, :] * buf_ref[pl.ds(base + t - phase, n), :]
                    part = term if part is None else part + term
                part = part[phase:phase + rg, :]
                conv = part if conv is None else conv + part

            if is_a:
                mix = _silu(_layer_norm(conv + cb_ref[...], ng_ref[...], nb_ref[...]))
                q = u[:, 2 * c:]
            else:
                mix = u[:, :c] * conv
                q = u[:, 3 * c:]
            attn = _memory_attention(q, head(k_ref), head(v_ref))
            out = _dot(_bf(jnp.concatenate([mix, attn], axis=-1)), w_out_ref[...])
            y_ref[rows, :] = _layer_norm(DEEPNORM_ALPHA * x + out, g1_ref[...], b1_ref[...])

    @pl.when((step < n_tiles) & (seq_step == nl - 1))
    def _():
        hist_ref[0] = buf_ref[pl.ds(HIST_PAD + tl - (width - 1), width - 1), :]

    @pl.when(step < n_tiles)
    def _():
        buf_ref[pl.ds(0, HIST_PAD), :] = buf_ref[pl.ds(tl, HIST_PAD), :]


def _prompt_token_sublayer(is_a, layer, x, sample_rows, batch, seq, w_in, cw, cb, ng, nb, k, v, w_out, g1, b1):
    tl = PROMPT_SEQ_TILE
    assert sample_rows.shape == (tl, D_MODEL)
    nl = seq // tl
    n_tiles = batch * nl
    c = MIX_WIDTH
    width = CONV_A_WIDTH if is_a else CONV_B_WIDTH
    n_in = w_in.shape[1]
    tile = lambda s: jnp.minimum(s, n_tiles - 1)
    mem_spec = pl.BlockSpec((1, 1, N_MEM, XATTN_WIDTH), lambda s: (layer, tile(s) // nl, 0, 0))
    return pl.pallas_call(
        functools.partial(_prompt_token_body, is_a, nl, n_tiles),
        grid=(n_tiles + 1,),
        in_specs=[pl.BlockSpec((tl, D_MODEL), lambda s: (tile(s), 0)),
                  _full((D_MODEL, n_in)), _full((width, c)), _full((1, c)), _full((1, c)), _full((1, c)),
                  mem_spec, mem_spec,
                  _full((c + XATTN_WIDTH, D_MODEL)), _full((1, D_MODEL)), _full((1, D_MODEL)),
                  _full((tl, D_MODEL))],
        out_specs=[pl.BlockSpec((tl, D_MODEL), lambda s: (s, 0)),
                   pl.BlockSpec((1, width - 1, c), lambda s: (tile(s) // nl, 0, 0))],
        out_shape=[jax.ShapeDtypeStruct(((n_tiles + 1) * tl, D_MODEL), jnp.float32),
                   jax.ShapeDtypeStruct((batch, width - 1, c), jnp.float32)],
        scratch_shapes=[pltpu.VMEM((HIST_PAD + tl, c), jnp.float32)],
        compiler_params=pltpu.CompilerParams(dimension_semantics=("arbitrary",),
                                             vmem_limit_bytes=_vmem_limit(48 * 1024 * 1024)),
        name="prompt_token_a" if is_a else "prompt_token_b",
    )(x, w_in, cw, cb, ng, nb, k, v, w_out, g1, b1, sample_rows)


def _sample_mix_body(is_a, n_seq, x_ref, w_in_ref, hist_ref, cw_ref, cb_ref, ng_ref, nb_ref,
                     mix_ref, q_ref, new_hist_ref):
    c = MIX_WIDTH
    width = CONV_A_WIDTH if is_a else CONV_B_WIDTH
    n_hist = width - 1
    n_pos = x_ref.shape[0] // n_seq
    u = _dot(_bf(x_ref[...]), w_in_ref[...])
    if is_a:
        conv_in = u[:, :c] * _sigmoid(u[:, c:2 * c])
        q_ref[...] = u[:, 2 * c:]
    else:
        conv_in = u[:, c:2 * c] * u[:, 2 * c:3 * c]
        q_ref[...] = u[:, 3 * c:]

    def full_row(j):
        if j < n_hist:
            return hist_ref[j]
        return conv_in[(j - n_hist) * n_seq:(j - n_hist + 1) * n_seq, :]

    for l in range(n_pos):
        conv = cw_ref[0:1, :] * full_row(l)
        for t in range(1, width):
            conv = conv + cw_ref[t:t + 1, :] * full_row(l + t)
        rows = slice(l * n_seq, (l + 1) * n_seq)
        if is_a:
            mix_ref[rows, :] = _silu(_layer_norm(conv + cb_ref[...], ng_ref[...], nb_ref[...]))
        else:
            mix_ref[rows, :] = u[rows, :c] * conv
    for j in range(n_hist):
        new_hist_ref[j] = full_row(j + n_pos)


def _sample_mix(is_a, n_seq, rows, x, x_block, w_in, hist, cw, cb, ng, nb):
    c = MIX_WIDTH
    small = (w_in, hist, cw, cb, ng, nb)
    return pl.pallas_call(
        functools.partial(_sample_mix_body, is_a, n_seq),
        grid=(1,),
        in_specs=[pl.BlockSpec((rows, D_MODEL), lambda i: (x_block, 0))] + [_full(a.shape) for a in small],
        out_specs=[_full((rows, c)), _full((rows, XATTN_WIDTH)), _full(hist.shape)],
        out_shape=[jax.ShapeDtypeStruct((rows, c), jnp.float32),
                   jax.ShapeDtypeStruct((rows, XATTN_WIDTH), jnp.float32),
                   jax.ShapeDtypeStruct(hist.shape, jnp.float32)],
        compiler_params=pltpu.CompilerParams(dimension_semantics=("arbitrary",),
                                             vmem_limit_bytes=_vmem_limit(48 * 1024 * 1024)),
        name="sample_mix_a" if is_a else "sample_mix_b",
    )(x, *small)


def _sample_attn_body(n_seq, q_ref, k_ref, v_ref, o_ref, bias_ref):
    bb = k_ref.shape[1]
    n_pos = q_ref.shape[0] // n_seq
    first = pl.multiple_of(pl.program_id(0) * bb, bb)

    @pl.when(pl.program_id(0) == 0)
    def _():
        r = lax.broadcasted_iota(jnp.int32, bias_ref.shape, 0)
        col = lax.broadcasted_iota(jnp.int32, bias_ref.shape, 1)
        valid = ((r % N_XHEADS) == (col // (n_pos * bb))) & ((r // (N_MEM * N_XHEADS)) == (col % bb))
        bias_ref[...] = jnp.where(valid, 0.0, NEG_INF)

    q = jnp.concatenate([q_ref[pl.ds(l * n_seq + first, bb), :] for l in range(n_pos)], axis=0)
    nq = n_pos * bb
    n_rows = bb * N_MEM * N_XHEADS
    k_rows = k_ref[0].reshape(n_rows, XHEAD_DIM)
    v_rows = v_ref[0].reshape(n_rows, XHEAD_DIM)
    q_heads = jnp.concatenate([q[:, h * XHEAD_DIM:(h + 1) * XHEAD_DIM] for h in range(N_XHEADS)], axis=0)
    s = _dot_nt(_bf(k_rows), _bf(q_heads)) * (XHEAD_DIM ** -0.5) + bias_ref[...]
    e = jnp.exp(s - jnp.max(s, axis=0, keepdims=True))
    p = e / jnp.sum(e, axis=0, keepdims=True)
    o_heads = _dot_tn(_bf(p), _bf(v_rows))
    o = jnp.concatenate([o_heads[h * nq:(h + 1) * nq] for h in range(N_XHEADS)], axis=1)
    for l in range(n_pos):
        o_ref[pl.ds(l * n_seq + first, bb), :] = o[l * bb:(l + 1) * bb, :]


def _sample_attention(layer, n_seq, q, mem_k, mem_v):
    rows = q.shape[0]
    bb = SAMPLE_BATCH_BLOCK
    mem_spec = pl.BlockSpec((1, bb, N_MEM, N_XHEADS, XHEAD_DIM), lambda i: (layer, i, 0, 0, 0))
    return pl.pallas_call(
        functools.partial(_sample_attn_body, n_seq),
        grid=(n_seq // bb,),
        in_specs=[pl.BlockSpec((rows, XATTN_WIDTH), lambda i: (0, 0)), mem_spec, mem_spec],
        out_specs=pl.BlockSpec((rows, XATTN_WIDTH), lambda i: (0, 0)),
        out_shape=jax.ShapeDtypeStruct((rows, XATTN_WIDTH), jnp.float32),
        scratch_shapes=[pltpu.VMEM((bb * N_MEM * N_XHEADS, N_XHEADS * (rows // n_seq) * bb), jnp.float32)],
        compiler_params=pltpu.CompilerParams(dimension_semantics=("arbitrary",),
                                             vmem_limit_bytes=_vmem_limit(40 * 1024 * 1024)),
        name="sample_attention",
    )(q, mem_k, mem_v)


def _sample_out_body(x_ref, mix_ref, attn_ref, w_out_ref, g1_ref, b1_ref, y_ref):
    cat = jnp.concatenate([mix_ref[...], attn_ref[...]], axis=-1)
    out = _dot(_bf(cat), w_out_ref[...])
    y_ref[...] = _layer_norm(DEEPNORM_ALPHA * x_ref[...] + out, g1_ref[...], b1_ref[...])


def _sample_out(rows, x, x_block, mix, attn, w_out, g1, b1):
    small = (mix, attn, w_out, g1, b1)
    return pl.pallas_call(
        _sample_out_body,
        grid=(1,),
        in_specs=[pl.BlockSpec((rows, D_MODEL), lambda i: (x_block, 0))] + [_full(a.shape) for a in small],
        out_specs=_full((rows, D_MODEL)),
        out_shape=jax.ShapeDtypeStruct((rows, D_MODEL), jnp.float32),
        compiler_params=pltpu.CompilerParams(dimension_semantics=("arbitrary",),
                                             vmem_limit_bytes=_vmem_limit(32 * 1024 * 1024)),
        name="sample_out",
    )(x, *small)


def _first_index_of(mask, index, n):
    cand = jnp.where(mask, index, float(n))
    while cand.ndim > 2:
        cand = jnp.min(cand, axis=0)
    return jnp.min(cand, axis=0, keepdims=True)


def _max_all(x):
    while x.ndim > 2:
        x = jnp.max(x, axis=0)
    return jnp.max(x, axis=0, keepdims=True)


def _routing_stages(xb, wr_ref, bias_ref, out):
    tm = xb.shape[0]
    scores = _sigmoid(_dot_nt(wr_ref[...], xb))
    biased = scores + bias_ref[...]
    grp = biased.reshape(N_GROUPS, GROUP_SIZE, tm)

    within = lax.broadcasted_iota(jnp.int32, grp.shape, 1).astype(jnp.float32)
    top1 = jnp.max(grp, axis=1, keepdims=True)
    first = jnp.min(jnp.where(grp == top1, within, float(GROUP_SIZE)), axis=1, keepdims=True)
    top2 = jnp.max(jnp.where(within == first, NEG_INF, grp), axis=1, keepdims=True)
    grp_score = (top1 + top2).reshape(N_GROUPS, tm)
    yield

    gidx = lax.broadcasted_iota(jnp.int32, grp_score.shape, 0).astype(jnp.float32)
    grp_sel = jnp.zeros(grp_score.shape, jnp.float32)
    for _ in range(TOPK_GROUPS):
        best = jnp.max(grp_score, axis=0, keepdims=True)
        pick = gidx == _first_index_of(grp_score == best, gidx, N_GROUPS)
        grp_sel = jnp.where(pick, 1.0, grp_sel)
        grp_score = jnp.where(pick, NEG_INF, grp_score)
    yield

    eidx = (lax.broadcasted_iota(jnp.int32, grp.shape, 0) * GROUP_SIZE
            + lax.broadcasted_iota(jnp.int32, grp.shape, 1)).astype(jnp.float32)
    cand = jnp.where(grp_sel.reshape(N_GROUPS, 1, tm) > 0.0, grp, NEG_INF)
    chosen = jnp.zeros(grp.shape, jnp.float32)
    for k in range(TOP_K):
        best = _max_all(cand).reshape(1, 1, tm)
        pick = eidx == _first_index_of(cand == best, eidx, N_EXPERTS).reshape(1, 1, tm)
        chosen = jnp.where(pick, 1.0, chosen)
        cand = jnp.where(pick, NEG_INF, cand)
        if k % 2 == 1:
            yield

    w = jnp.where(chosen > 0.0, scores.reshape(grp.shape), 0.0)
    total = jnp.sum(jnp.sum(w, axis=0), axis=0, keepdims=True).reshape(1, 1, tm)
    gates = (w / total * ROUTED_SCALE).reshape(N_EXPERTS, tm)
    chosen = chosen.reshape(N_EXPERTS, tm)
    yield

    earlier = (lax.broadcasted_iota(jnp.int32, (tm, tm), 0) < lax.broadcasted_iota(jnp.int32, (tm, tm), 1))
    pos = _dot(_bf(chosen), jnp.where(earlier, 1.0, 0.0).astype(jnp.bfloat16))
    routed = chosen > 0.0
    in_window = routed & (pos < float(SLOT_WINDOW))
    out["sel"] = jnp.where(in_window, pos, -1.0)
    out["gate"] = jnp.where(in_window, gates, 0.0)
    out["over"] = jnp.where(routed & (pos >= float(SLOT_WINDOW)), gates, 0.0)
    yield


def _slot_matches(sel_row, first_slot=0, n_slots=None):
    tm = sel_row.shape[1]
    n_slots = SLOT_WINDOW if n_slots is None else n_slots
    slot = (lax.broadcasted_iota(jnp.int32, (n_slots, tm), 0) + first_slot).astype(jnp.float32)
    return jnp.broadcast_to(sel_row, (n_slots, tm)) == slot


def _slot_onehot(sel, first_expert):
    rows = [_slot_matches(sel[e:e + 1, :]) for e in range(first_expert, first_expert + EXPERT_CHUNK)]
    return jnp.where(jnp.concatenate(rows, axis=0), 1.0, 0.0).astype(jnp.bfloat16)


def _route_body(x_ref, wr_ref, bias_ref, xs_ref, sel_ref, gate_ref, over_ref, flag_ref, fill_ref, xb_prev, sel_prev):
    tm = x_ref.shape[0]
    n_groups = SLOT_WINDOW // SLOT_GROUP

    @pl.when(pl.program_id(0) == 0)
    def _():
        xb_prev[...] = jnp.zeros(xb_prev.shape, jnp.bfloat16)
        sel_prev[...] = jnp.full(sel_prev.shape, -1.0, jnp.float32)

    xb_old, sel_old = xb_prev[...], sel_prev[...]
    xb = _bf(x_ref[...])
    out = {}
    stages = _routing_stages(xb, wr_ref, bias_ref, out)
    onehot = _slot_onehot(sel_old, 0)
    for first in range(0, N_EXPERTS, EXPERT_CHUNK):
        later = _slot_onehot(sel_old, first + EXPERT_CHUNK) if first + EXPERT_CHUNK < N_EXPERTS else None
        next(stages, None)
        slots = _bf(_dot(onehot, xb_old))
        xs_ref[first:first + EXPERT_CHUNK, :, 0] = slots.reshape(EXPERT_CHUNK, n_groups, SLOT_GROUP, D_MODEL)
        onehot = later
    for _ in stages:
        pass

    sel, over = out["sel"], out["over"]
    sel_ref[...] = sel
    gate_ref[...] = out["gate"]
    pad = jnp.zeros((GATE_LANES - N_EXPERTS, tm), jnp.float32)
    over_ref[...] = jnp.concatenate([over, pad], axis=0).T
    flag_ref[...] = jnp.broadcast_to(jnp.max(jnp.max(over, axis=0, keepdims=True), axis=1, keepdims=True),
                                     flag_ref.shape[1:])[None]
    fill_ref[...] = jnp.broadcast_to(jnp.max(sel, axis=1, keepdims=True), fill_ref.shape[1:])[None]
    xb_prev[...] = xb
    sel_prev[...] = sel


def _route_dispatch(x, w_router_t, bias_col):
    rows = x.shape[0]
    tm = TOKEN_TILE
    n_tiles = rows // tm
    n_groups = SLOT_WINDOW // SLOT_GROUP
    routed = lambda i: jnp.minimum(i, n_tiles - 1)
    return pl.pallas_call(
        _route_body,
        grid=(n_tiles + 1,),
        in_specs=[pl.BlockSpec((tm, D_MODEL), lambda i: (routed(i), 0)),
                  _full((N_EXPERTS, D_MODEL)), _full((N_EXPERTS, 1))],
        out_specs=[pl.BlockSpec((N_EXPERTS, n_groups, 1, SLOT_GROUP, D_MODEL),
                                lambda i: (0, 0, jnp.maximum(i - 1, 0), 0, 0)),
                   pl.BlockSpec((N_EXPERTS, tm), lambda i: (0, routed(i))),
                   pl.BlockSpec((N_EXPERTS, tm), lambda i: (0, routed(i))),
                   pl.BlockSpec((tm, GATE_LANES), lambda i: (routed(i), 0)),
                   pl.BlockSpec((1, 8, V7X_LANES), lambda i: (routed(i), 0, 0)),
                   pl.BlockSpec((1, N_EXPERTS, V7X_LANES), lambda i: (routed(i), 0, 0))],
        out_shape=[jax.ShapeDtypeStruct((N_EXPERTS, n_groups, n_tiles, SLOT_GROUP, D_MODEL), jnp.bfloat16),
                   jax.ShapeDtypeStruct((N_EXPERTS, rows), jnp.float32),
                   jax.ShapeDtypeStruct((N_EXPERTS, rows), jnp.float32),
                   jax.ShapeDtypeStruct((rows, GATE_LANES), jnp.float32),
                   jax.ShapeDtypeStruct((n_tiles, 8, V7X_LANES), jnp.float32),
                   jax.ShapeDtypeStruct((n_tiles, N_EXPERTS, V7X_LANES), jnp.float32)],
        scratch_shapes=[pltpu.VMEM((tm, D_MODEL), jnp.bfloat16), pltpu.VMEM((N_EXPERTS, tm), jnp.float32)],
        compiler_params=pltpu.CompilerParams(dimension_semantics=("arbitrary",),
                                             vmem_limit_bytes=_vmem_limit(48 * 1024 * 1024)),
        name="route_dispatch",
    )(x, w_router_t, bias_col)


def _expert_body(need_ref, last_e_ref, last_c_ref, xm_ref, xt_ref, sel_ref, gate_ref, wg_ref, wu_ref, wd_ref,
                 ym_ref, yt_ref, wg_bf, wu_bf, wd_bf):
    del last_e_ref, last_c_ref

    @pl.when(pl.program_id(1) == 0)
    def _():
        wg_bf[...] = _bf(wg_ref[0, 0])
        wu_bf[...] = _bf(wu_ref[0, 0])
        wd_bf[...] = _bf(wd_ref[0, 0])

    row = pl.ds(pl.program_id(0) % V7X_SUBLANES, 1)
    n_main = xm_ref.shape[1]
    n_tiles = xm_ref.shape[2]
    tm = sel_ref.shape[1] // n_tiles

    def run(x_ref, y_ref, first_group):
        n_g = x_ref.shape[1]
        rows = n_tiles * SLOT_GROUP
        proj = None
        for g in range(n_g + 1):
            nxt = None
            if g < n_g:
                x = x_ref[0, g].reshape(rows, D_MODEL)
                nxt = (_dot(x, wg_bf[...]), _dot(x, wu_bf[...]))
            if proj is not None:
                slot_gates = []
                for t in range(n_tiles):
                    cols = slice(t * tm, (t + 1) * tm)
                    match = _slot_matches(sel_ref[row, cols], (first_group + g - 1) * SLOT_GROUP, SLOT_GROUP)
                    slot_gates.append(jnp.sum(jnp.where(match, gate_ref[row, cols], 0.0), axis=-1, keepdims=True))
                gate = jnp.concatenate(slot_gates, axis=0)
                h = _silu(proj[0]) * proj[1]
                y = _bf(_dot(_bf(h), wd_bf[...]) * gate)
                y_ref[0, g - 1] = y.reshape(n_tiles, SLOT_GROUP, D_MODEL)
            proj = nxt

    run(xm_ref, ym_ref, 0)

    @pl.when(need_ref[pl.program_id(0) * pl.num_programs(1) + pl.program_id(1)] > 0)
    def _():
        run(xt_ref, yt_ref, n_main)


def _slot_group_split():
    n_groups = SLOT_WINDOW // SLOT_GROUP
    n_main = min(SLOT_GROUPS_ALWAYS, n_groups - 1)
    n_tail = n_groups - n_main
    assert n_main % n_tail == 0, "the trailing groups must form one block of the group axis"
    return n_main, n_tail


def _experts(layer, xs, sel, gate, fill, w_gate, w_up, w_down):
    n_e, _, n_tiles_all, _, _ = xs.shape
    n_chunks = EXPERT_ROW_CHUNKS
    n_tiles = n_tiles_all // n_chunks
    tokens = sel.shape[1] // n_chunks
    n_main, n_tail = _slot_group_split()

    need = (jnp.max(fill.reshape(n_e, n_chunks, n_tiles), axis=-1) >= n_main * SLOT_GROUP).astype(jnp.int32)
    steps = jnp.arange(n_e * n_chunks, dtype=jnp.int32)
    last = jnp.maximum(lax.cummax(jnp.where(need.reshape(-1) > 0, steps, -1)), 0)
    last_e, last_c = last // n_chunks, last % n_chunks

    w_in_spec = pl.BlockSpec((1, 1, D_MODEL, EXPERT_FF), lambda e, c, *_: (layer, e, 0, 0))
    route_spec = pl.BlockSpec((V7X_SUBLANES, tokens), lambda e, c, *_: (e // V7X_SUBLANES, c))
    main_spec = pl.BlockSpec((1, n_main, n_tiles, SLOT_GROUP, D_MODEL), lambda e, c, *_: (e, 0, c, 0, 0))

    def tail_map(group_block):
        return lambda e, c, nd, le, lc: (le[e * n_chunks + c], group_block, lc[e * n_chunks + c], 0, 0)

    tail_block = (1, n_tail, n_tiles, SLOT_GROUP, D_MODEL)
    ym, yt = pl.pallas_call(
        _expert_body,
        grid_spec=pltpu.PrefetchScalarGridSpec(
            num_scalar_prefetch=3,
            grid=(n_e, n_chunks),
            in_specs=[main_spec, pl.BlockSpec(tail_block, tail_map(n_main // n_tail)),
                      route_spec, route_spec, w_in_spec, w_in_spec,
                      pl.BlockSpec((1, 1, EXPERT_FF, D_MODEL), lambda e, c, *_: (layer, e, 0, 0))],
            out_specs=[main_spec, pl.BlockSpec(tail_block, tail_map(0))],
            scratch_shapes=[pltpu.VMEM((D_MODEL, EXPERT_FF), jnp.bfloat16),
                            pltpu.VMEM((D_MODEL, EXPERT_FF), jnp.bfloat16),
                            pltpu.VMEM((EXPERT_FF, D_MODEL), jnp.bfloat16)],
        ),
        out_shape=[jax.ShapeDtypeStruct((n_e, n_main, n_tiles_all, SLOT_GROUP, D_MODEL), jnp.bfloat16),
                   jax.ShapeDtypeStruct((n_e, n_tail, n_tiles_all, SLOT_GROUP, D_MODEL), jnp.bfloat16)],
        compiler_params=pltpu.CompilerParams(dimension_semantics=("arbitrary", "arbitrary"),
                                             vmem_limit_bytes=_vmem_limit(40 * 1024 * 1024)),
        name="experts",
    )(need.reshape(-1), last_e, last_c, xs, xs, sel, gate, w_gate, w_up, w_down)
    return ym, yt, need


def _combine_body(has_extra, n_prompt_tiles, tiles_per_chunk, tile_need_ref, last_ref, need_ref,
                  x_ref, sel_ref, ym_ref, yt_ref, wsg_ref, wsu_ref, wsd_ref, g2_ref, b2_ref, *rest):
    del last_ref
    rest = list(rest)
    extra_ref = rest.pop(0) if has_extra else None
    out_refs = [rest.pop(0)] if n_prompt_tiles is None else [rest.pop(0), rest.pop(0)]
    wsg_bf, wsu_bf, wsd_bf, acc_ref = rest
    tile = pl.program_id(0)
    n_main, n_tail = ym_ref.shape[1], yt_ref.shape[1]
    n_chunks = need_ref.shape[0] // N_EXPERTS

    @pl.when(tile == 0)
    def _():
        wsg_bf[...] = _bf(wsg_ref[0])
        wsu_bf[...] = _bf(wsu_ref[0])
        wsd_bf[...] = _bf(wsd_ref[0])

    def onehot(first_expert, first_group, n_g):
        rows = [_slot_matches(sel_ref[e:e + 1, :], first_group * SLOT_GROUP, n_g * SLOT_GROUP)
                for e in range(first_expert, first_expert + COMBINE_CHUNK)]
        return jnp.where(jnp.concatenate(rows, axis=0), 1.0, 0.0).astype(jnp.bfloat16)

    x = x_ref[...]
    xb = _bf(x)
    ahead = onehot(0, 0, n_main)
    y = _dot(_bf(_silu(_dot(xb, wsg_bf[...])) * _dot(xb, wsu_bf[...])), wsd_bf[...])
    for first in range(0, N_EXPERTS, COMBINE_CHUNK):
        current = ahead
        if first + COMBINE_CHUNK < N_EXPERTS:
            ahead = onehot(first + COMBINE_CHUNK, 0, n_main)
        ys = ym_ref[first:first + COMBINE_CHUNK, :, 0].reshape(COMBINE_CHUNK * n_main * SLOT_GROUP, D_MODEL)
        y = y + _dot_tn(current, ys)
    if has_extra:
        y = y + extra_ref[...]
    acc_ref[...] = y

    @pl.when(tile_need_ref[tile] > 0)
    def _():
        chunk = tile // tiles_per_chunk
        part = jnp.zeros(acc_ref.shape, jnp.float32)
        for first in range(0, N_EXPERTS, COMBINE_CHUNK):
            rows = []
            for e in range(first, first + COMBINE_CHUNK):
                ye = yt_ref[e, :, 0].reshape(n_tail * SLOT_GROUP, D_MODEL)
                rows.append(jnp.where(need_ref[e * n_chunks + chunk] > 0, ye, jnp.zeros_like(ye)))
            part = part + _dot_tn(onehot(first, n_main, n_tail), jnp.concatenate(rows, axis=0))
        acc_ref[...] += part

    y = _layer_norm(DEEPNORM_ALPHA * x + acc_ref[...], g2_ref[...], b2_ref[...])

    if n_prompt_tiles is None:
        out_refs[0][...] = y
    else:
        @pl.when(tile < n_prompt_tiles)
        def _():
            out_refs[0][...] = y

        @pl.when(tile >= n_prompt_tiles)
        def _():
            out_refs[1][...] = y


def _combine(layer, x, sel, fill, ym, yt, need, ws_gate, ws_up, ws_down, g2, b2, extra=None, prompt_rows=None):
    rows = x.shape[0]
    tm = TOKEN_TILE
    n_tiles = rows // tm
    has_extra = extra is not None
    n_main, n_tail = ym.shape[1], yt.shape[1]
    tile_need = (jnp.max(fill, axis=0) >= n_main * SLOT_GROUP).astype(jnp.int32)
    tiles = jnp.arange(n_tiles, dtype=jnp.int32)
    last = jnp.maximum(lax.cummax(jnp.where(tile_need > 0, tiles, -1)), 0)

    in_specs = [pl.BlockSpec((tm, D_MODEL), lambda i, *_: (i, 0)),
                pl.BlockSpec((N_EXPERTS, tm), lambda i, *_: (0, i)),
                pl.BlockSpec((N_EXPERTS, n_main, 1, SLOT_GROUP, D_MODEL), lambda i, *_: (0, 0, i, 0, 0)),
                pl.BlockSpec((N_EXPERTS, n_tail, 1, SLOT_GROUP, D_MODEL), lambda i, tn, la, nd: (0, 0, la[i], 0, 0)),
                pl.BlockSpec((1, D_MODEL, EXPERT_FF), lambda i, *_: (layer, 0, 0)),
                pl.BlockSpec((1, D_MODEL, EXPERT_FF), lambda i, *_: (layer, 0, 0)),
                pl.BlockSpec((1, EXPERT_FF, D_MODEL), lambda i, *_: (layer, 0, 0)),
                pl.BlockSpec((1, D_MODEL), lambda i, *_: (0, 0)), pl.BlockSpec((1, D_MODEL), lambda i, *_: (0, 0))]
    args = [x, sel, ym, yt, ws_gate, ws_up, ws_down, g2, b2]
    if has_extra:
        in_specs.append(pl.BlockSpec((tm, D_MODEL), lambda i, *_: (i, 0)))
        args.append(extra)
    if prompt_rows is None:
        n_prompt_tiles = None
        out_specs = pl.BlockSpec((tm, D_MODEL), lambda i, *_: (i, 0))
        out_shape = jax.ShapeDtypeStruct((rows, D_MODEL), jnp.float32)
    else:
        n_prompt_tiles = prompt_rows // tm
        out_specs = [pl.BlockSpec((tm, D_MODEL), lambda i, *_: (jnp.minimum(i, n_prompt_tiles - 1), 0)),
                     pl.BlockSpec((tm, D_MODEL), lambda i, *_: (jnp.maximum(i - n_prompt_tiles, 0), 0))]
        out_shape = [jax.ShapeDtypeStruct((prompt_rows, D_MODEL), jnp.float32),
                     jax.ShapeDtypeStruct((rows - prompt_rows, D_MODEL), jnp.float32)]
    return pl.pallas_call(
        functools.partial(_combine_body, has_extra, n_prompt_tiles, n_tiles // need.shape[1]),
        grid_spec=pltpu.PrefetchScalarGridSpec(
            num_scalar_prefetch=3,
            grid=(n_tiles,),
            in_specs=in_specs,
            out_specs=out_specs,
            scratch_shapes=[pltpu.VMEM((D_MODEL, EXPERT_FF), jnp.bfloat16),
                            pltpu.VMEM((D_MODEL, EXPERT_FF), jnp.bfloat16),
                            pltpu.VMEM((EXPERT_FF, D_MODEL), jnp.bfloat16),
                            pltpu.VMEM((tm, D_MODEL), jnp.float32)],
        ),
        out_shape=out_shape,
        compiler_params=pltpu.CompilerParams(dimension_semantics=("arbitrary",),
                                             vmem_limit_bytes=_vmem_limit(48 * 1024 * 1024)),
        name="combine_extra" if has_extra else "combine",
    )(tile_need, last, need.reshape(-1), *args)


def _dense_body(x_ref, gate_ref, wg_ref, wu_ref, wd_ref, y_ref, xb_ref):
    e = pl.program_id(1)

    @pl.when(e == 0)
    def _():
        xb_ref[...] = _bf(x_ref[...])
        y_ref[...] = jnp.zeros(y_ref.shape, jnp.float32)

    xb = xb_ref[...]
    h = _silu(_dot(xb, _bf(wg_ref[0, 0]))) * _dot(xb, _bf(wu_ref[0, 0]))
    down = _dot(_bf(h), _bf(wd_ref[0, 0]))
    lane = lax.broadcasted_iota(jnp.int32, gate_ref.shape, 1)
    gate = jnp.sum(jnp.where(lane == e, gate_ref[...], 0.0), axis=-1, keepdims=True)
    y_ref[...] += down * gate


def _dense_experts(layer, x, gates, w_gate, w_up, w_down):
    rows = x.shape[0]
    tm = DENSE_TILE
    w_in_spec = pl.BlockSpec((1, 1, D_MODEL, EXPERT_FF), lambda i, e: (layer, e, 0, 0))
    return pl.pallas_call(
        _dense_body,
        grid=(rows // tm, N_EXPERTS),
        in_specs=[pl.BlockSpec((tm, D_MODEL), lambda i, e: (i, 0)),
                  pl.BlockSpec((tm, GATE_LANES), lambda i, e: (i, 0)),
                  w_in_spec, w_in_spec,
                  pl.BlockSpec((1, 1, EXPERT_FF, D_MODEL), lambda i, e: (layer, e, 0, 0))],
        out_specs=pl.BlockSpec((tm, D_MODEL), lambda i, e: (i, 0)),
        out_shape=jax.ShapeDtypeStruct((rows, D_MODEL), jnp.float32),
        scratch_shapes=[pltpu.VMEM((tm, D_MODEL), jnp.bfloat16)],
        compiler_params=pltpu.CompilerParams(dimension_semantics=("arbitrary", "arbitrary"),
                                             vmem_limit_bytes=_vmem_limit(40 * 1024 * 1024)),
        name="dense_overflow",
    )(x, gates, w_gate, w_up, w_down)


def _channel_sublayer(layer, x, w_router_t, bias_col, w_gate, w_up, w_down, ws_gate, ws_up, ws_down, g2, b2,
                      prompt_rows=None):
    xs, sel, gate, over, flags, fill = _route_dispatch(x, w_router_t, bias_col)
    fill = fill[:, :, 0].T
    ym, yt, need = _experts(layer, xs, sel, gate, fill, w_gate, w_up, w_down)
    rest = (x, sel, fill, ym, yt, need, ws_gate, ws_up, ws_down, g2, b2)

    def with_overflow():
        extra = _dense_experts(layer, x, over, w_gate, w_up, w_down)
        return _combine(layer, *rest, extra=extra, prompt_rows=prompt_rows)

    def without_overflow():
        return _combine(layer, *rest, prompt_rows=prompt_rows)

    return lax.cond(jnp.max(flags) > 0.0, with_overflow, without_overflow)


def kernel(x_prompt, x_sample, mem_prompt, cache_mem_k, cache_mem_v, state_conv_a, state_conv_b, w_in_a, conv_a_w, conv_a_b, norm_a_g, norm_a_b, w_in_b, conv_b_w, w_kv, w_out, ln1_g, ln1_b, w_router, router_bias, w_gate, w_up, w_down, ws_gate, ws_up, ws_down, ln2_g, ln2_b):
    batch, seq, d = x_prompt.shape
    n_seq, n_pos, _ = x_sample.shape
    c = MIX_WIDTH
    p_rows, s_rows = batch * seq, n_pos * n_seq
    s_block = p_rows // s_rows
    row = lambda a: a.reshape(1, -1)

    x_p = x_prompt.reshape(p_rows, d)
    x_s, x_s_block = x_sample.transpose(1, 0, 2).reshape(s_rows, d), 0
    k_all, v_all = _kv_projection(mem_prompt.reshape(batch * N_MEM, d), w_kv)
    k_p = k_all.reshape(DEPTH, batch, N_MEM, XATTN_WIDTH)
    v_p = v_all.reshape(DEPTH, batch, N_MEM, XATTN_WIDTH)

    conv_a_p, conv_b_p, conv_a_s, conv_b_s = [], [], [], []
    for i in range(DEPTH):
        j = i // N_MIXERS
        is_a = i % N_MIXERS == 0
        if is_a:
            w_in, cw = _bf(w_in_a[j]), conv_a_w[j]
            cb, ng, nb = row(conv_a_b[j]), row(norm_a_g[j]), row(norm_a_b[j])
            hist_s = state_conv_a[j]
        else:
            w_in, cw = _bf(w_in_b[j]), conv_b_w[j]
            cb = ng = nb = jnp.zeros((1, c), jnp.float32)
            hist_s = state_conv_b[j]
        w_o = _bf(w_out[i])
        g1, b1 = row(ln1_g[i]), row(ln1_b[i])

        mix, q, hist_s_new = _sample_mix(is_a, n_seq, s_rows, x_s, x_s_block, w_in, hist_s.transpose(1, 0, 2),
                                         cw, cb, ng, nb)
        attn = _sample_attention(i, n_seq, q, cache_mem_k, cache_mem_v)
        h_s = _sample_out(s_rows, x_s, x_s_block, mix, attn, w_o, g1, b1)
        h, hist_p_new = _prompt_token_sublayer(is_a, i, x_p, h_s, batch, seq, w_in, cw, cb, ng, nb,
                                               k_p, v_p, w_o, g1, b1)
        hist_s_new = hist_s_new.transpose(1, 0, 2)
        if is_a:
            conv_a_p.append(hist_p_new)
            conv_a_s.append(hist_s_new)
        else:
            conv_b_p.append(hist_p_new)
            conv_b_s.append(hist_s_new)

        last = i == DEPTH - 1
        h = _channel_sublayer(i, h, _bf(w_router[i].T), router_bias[i].reshape(N_EXPERTS, 1),
                              w_gate, w_up, w_down, ws_gate, ws_up, ws_down, row(ln2_g[i]), row(ln2_b[i]),
                              prompt_rows=p_rows if last else None)
        if not last:
            x_p = h
            x_s, x_s_block = h, s_block

    y_p, y_s = h
    new_k = k_all.reshape(DEPTH, batch, N_MEM, N_XHEADS, XHEAD_DIM)
    new_v = v_all.reshape(DEPTH, batch, N_MEM, N_XHEADS, XHEAD_DIM)
    return (y_p.reshape(batch, seq, d), y_s.reshape(n_pos, n_seq, d).transpose(1, 0, 2), new_k, new_v,
            jnp.stack(conv_a_p), jnp.stack(conv_b_p), jnp.stack(conv_a_s), jnp.stack(conv_b_s))
```

```python
import functools

import jax
import jax.numpy as jnp
from jax import lax
from jax.experimental import pallas as pl
from jax.experimental.pallas import tpu as pltpu

D_MODEL = 1024
DEPTH = 2
N_MIXERS = 2
MIX_WIDTH = D_MODEL // 2
N_MEM = 256
N_XHEADS = 4
XHEAD_DIM = MIX_WIDTH // N_XHEADS
XATTN_WIDTH = N_XHEADS * XHEAD_DIM
CONV_A_WIDTH = 31
CONV_B_WIDTH = 3
N_EXPERTS = 64
TOP_K = 8
N_GROUPS = 8
GROUP_SIZE = N_EXPERTS // N_GROUPS
TOPK_GROUPS = 4
EXPERT_FF = D_MODEL // 4
ROUTED_SCALE = 2.5
LN_EPS = 1e-5
DEEPNORM_ALPHA = (2 * DEPTH) ** 0.25

V7X_LANES = 128
V7X_SUBLANES = 8
V7X_VMEM_BYTES = 64 * 1024 * 1024

HIST_PAD = 32
PROMPT_SEQ_TILE = 512
PROMPT_ROW_GROUPS = 1
SAMPLE_BATCH_BLOCK = 8
TOKEN_TILE = 256
SLOT_WINDOW = 64
SLOT_GROUP = 16
SLOT_GROUPS_ALWAYS = 3
EXPERT_CHUNK = 8
COMBINE_CHUNK = 16
EXPERT_ROW_CHUNKS = 1
GATE_LANES = V7X_LANES
DENSE_TILE = 512
NEG_INF = float("-inf")


def _vmem_limit(nbytes):
    return int(min(max(nbytes, 16 * 1024 * 1024), V7X_VMEM_BYTES - 8 * 1024 * 1024))


def _bf(x):
    return x.astype(jnp.bfloat16)


def _dot(a, b):
    return jnp.dot(a, b, preferred_element_type=jnp.float32)


def _dot_nt(a, b):
    return lax.dot_general(a, b, (((1,), (1,)), ((), ())), preferred_element_type=jnp.float32)


def _dot_tn(a, b):
    return lax.dot_general(a, b, (((0,), (0,)), ((), ())), preferred_element_type=jnp.float32)


def _sigmoid(x):
    return 1.0 / (1.0 + jnp.exp(-x))


def _silu(x):
    return x * _sigmoid(x)


def _layer_norm(x, g, b):
    mu = jnp.mean(x, axis=-1, keepdims=True)
    xc = x - mu
    var = jnp.mean(xc * xc, axis=-1, keepdims=True)
    return xc * lax.rsqrt(var + LN_EPS) * g + b


def _memory_attention(q, k_head, v_head):
    outs = []
    for h in range(N_XHEADS):
        sl = slice(h * XHEAD_DIM, (h + 1) * XHEAD_DIM)
        s = _dot_nt(_bf(q[:, sl]), _bf(k_head(h))) * (XHEAD_DIM ** -0.5)
        e = jnp.exp(s - jnp.max(s, axis=-1, keepdims=True))
        p = e / jnp.sum(e, axis=-1, keepdims=True)
        outs.append(_dot(_bf(p), _bf(v_head(h))))
    return jnp.concatenate(outs, axis=-1)


def _full(shape):
    return pl.BlockSpec(shape, lambda *_: tuple(0 for _ in shape))


def _kv_body(mem_ref, w_ref, k_ref, v_ref):
    kv = _dot(_bf(mem_ref[...]), _bf(w_ref[0]))
    k_ref[0] = kv[:, :XATTN_WIDTH]
    v_ref[0] = kv[:, XATTN_WIDTH:]


def _kv_projection(mem2d, w_kv):
    rows = mem2d.shape[0]
    tm = 512
    out = jax.ShapeDtypeStruct((DEPTH, rows, XATTN_WIDTH), jnp.float32)
    return pl.pallas_call(
        _kv_body,
        grid=(DEPTH, rows // tm),
        in_specs=[pl.BlockSpec((tm, D_MODEL), lambda i, m: (m, 0)),
                  pl.BlockSpec((1, D_MODEL, 2 * XATTN_WIDTH), lambda i, m: (i, 0, 0))],
        out_specs=[pl.BlockSpec((1, tm, XATTN_WIDTH), lambda i, m: (i, m, 0)),
                   pl.BlockSpec((1, tm, XATTN_WIDTH), lambda i, m: (i, m, 0))],
        out_shape=[out, out],
        compiler_params=pltpu.CompilerParams(dimension_semantics=("arbitrary", "arbitrary"),
                                             vmem_limit_bytes=_vmem_limit(32 * 1024 * 1024)),
        name="kv_projection",
    )(mem2d, w_kv)


def _prompt_token_body(is_a, nl, n_tiles, x_ref, w_in_ref, cw_ref, cb_ref, ng_ref, nb_ref, k_ref, v_ref, w_out_ref,
                       g1_ref, b1_ref, sample_ref, y_ref, hist_ref, buf_ref):
    c = MIX_WIDTH
    tl = x_ref.shape[0]
    width = CONV_A_WIDTH if is_a else CONV_B_WIDTH
    step = pl.program_id(0)
    seq_step = step % nl

    @pl.when(step == n_tiles)
    def _():
        y_ref[...] = sample_ref[...]

    @pl.when((step < n_tiles) & (seq_step == 0))
    def _():
        buf_ref[pl.ds(0, HIST_PAD), :] = jnp.zeros((HIST_PAD, c), jnp.float32)

    @pl.when(step < n_tiles)
    def _():
        rg = tl // PROMPT_ROW_GROUPS
        head = lambda ref: lambda h: ref[0, 0, :, h * XHEAD_DIM:(h + 1) * XHEAD_DIM]

        def group_stages(rows):
            x = x_ref[rows, :]
            u = _dot(_bf(x), w_in_ref[...])
            conv_in = u[:, :c] * _sigmoid(u[:, c:2 * c]) if is_a else u[:, c:2 * c] * u[:, 2 * c:3 * c]
            buf_ref[pl.ds(HIST_PAD + rows.start, rg), :] = conv_in
            yield

            base = HIST_PAD - (width - 1) + rows.start
            conv = None
            for phase in range(V7X_SUBLANES):
                taps = [t for t in range(width) if (base + t) % V7X_SUBLANES == phase]
                if not taps:
                    continue
                n = rg if phase == 0 else rg + V7X_SUBLANES
                part = None
                for t in taps:
                    term = cw_ref[t:t + 1, :] * buf_ref[pl.ds(base + t - phase, n), :]
                    part = term if part is None else part + term
                part = part[phase:phase + rg, :]
                conv = part if conv is None else conv + part

            if is_a:
                mix = _silu(_layer_norm(conv + cb_ref[...], ng_ref[...], nb_ref[...]))
                q = u[:, 2 * c:]
            else:
                mix = u[:, :c] * conv
                q = u[:, 3 * c:]
            yield
            attn = _memory_attention(q, head(k_ref), head(v_ref))
            yield
            out = _dot(_bf(jnp.concatenate([mix, attn], axis=-1)), w_out_ref[...])
            y_ref[rows, :] = _layer_norm(DEEPNORM_ALPHA * x + out, g1_ref[...], b1_ref[...])
            yield

        gens = [group_stages(slice(g * rg, (g + 1) * rg)) for g in range(PROMPT_ROW_GROUPS)]
        for gen in gens:
            next(gen)
        later_stages = 3
        for tick in range(later_stages + PROMPT_ROW_GROUPS - 1):
            for g, gen in enumerate(gens):
                if 0 <= tick - g < later_stages:
                    next(gen)

    @pl.when((step < n_tiles) & (seq_step == nl - 1))
    def _():
        hist_ref[0] = buf_ref[pl.ds(HIST_PAD + tl - (width - 1), width - 1), :]

    @pl.when(step < n_tiles)
    def _():
        buf_ref[pl.ds(0, HIST_PAD), :] = buf_ref[pl.ds(tl, HIST_PAD), :]


def _prompt_token_sublayer(is_a, layer, x, sample_rows, batch, seq, w_in, cw, cb, ng, nb, k, v, w_out, g1, b1):
    tl = PROMPT_SEQ_TILE
    assert sample_rows.shape == (tl, D_MODEL)
    nl = seq // tl
    n_tiles = batch * nl
    c = MIX_WIDTH
    width = CONV_A_WIDTH if is_a else CONV_B_WIDTH
    n_in = w_in.shape[1]
    tile = lambda s: jnp.minimum(s, n_tiles - 1)
    mem_spec = pl.BlockSpec((1, 1, N_MEM, XATTN_WIDTH), lambda s: (layer, tile(s) // nl, 0, 0))
    return pl.pallas_call(
        functools.partial(_prompt_token_body, is_a, nl, n_tiles),
        grid=(n_tiles + 1,),
        in_specs=[pl.BlockSpec((tl, D_MODEL), lambda s: (tile(s), 0)),
                  _full((D_MODEL, n_in)), _full((width, c)), _full((1, c)), _full((1, c)), _full((1, c)),
                  mem_spec, mem_spec,
                  _full((c + XATTN_WIDTH, D_MODEL)), _full((1, D_MODEL)), _full((1, D_MODEL)),
                  _full((tl, D_MODEL))],
        out_specs=[pl.BlockSpec((tl, D_MODEL), lambda s: (s, 0)),
                   pl.BlockSpec((1, width - 1, c), lambda s: (tile(s) // nl, 0, 0))],
        out_shape=[jax.ShapeDtypeStruct(((n_tiles + 1) * tl, D_MODEL), jnp.float32),
                   jax.ShapeDtypeStruct((batch, width - 1, c), jnp.float32)],
        scratch_shapes=[pltpu.VMEM((HIST_PAD + tl, c), jnp.float32)],
        compiler_params=pltpu.CompilerParams(dimension_semantics=("arbitrary",),
                                             vmem_limit_bytes=_vmem_limit(48 * 1024 * 1024)),
        name="prompt_token_a" if is_a else "prompt_token_b",
    )(x, w_in, cw, cb, ng, nb, k, v, w_out, g1, b1, sample_rows)


def _sample_mix_body(is_a, n_seq, x_ref, w_in_ref, hist_ref, cw_ref, cb_ref, ng_ref, nb_ref,
                     mix_ref, q_ref, new_hist_ref):
    c = MIX_WIDTH
    width = CONV_A_WIDTH if is_a else CONV_B_WIDTH
    n_hist = width - 1
    n_pos = x_ref.shape[0] // n_seq
    u = _dot(_bf(x_ref[...]), w_in_ref[...])
    if is_a:
        conv_in = u[:, :c] * _sigmoid(u[:, c:2 * c])
        q_ref[...] = u[:, 2 * c:]
    else:
        conv_in = u[:, c:2 * c] * u[:, 2 * c:3 * c]
        q_ref[...] = u[:, 3 * c:]

    def full_row(j):
        if j < n_hist:
            return hist_ref[j]
        return conv_in[(j - n_hist) * n_seq:(j - n_hist + 1) * n_seq, :]

    for l in range(n_pos):
        conv = cw_ref[0:1, :] * full_row(l)
        for t in range(1, width):
            conv = conv + cw_ref[t:t + 1, :] * full_row(l + t)
        rows = slice(l * n_seq, (l + 1) * n_seq)
        if is_a:
            mix_ref[rows, :] = _silu(_layer_norm(conv + cb_ref[...], ng_ref[...], nb_ref[...]))
        else:
            mix_ref[rows, :] = u[rows, :c] * conv
    for j in range(n_hist):
        new_hist_ref[j] = full_row(j + n_pos)


def _sample_mix(is_a, n_seq, rows, x, x_block, w_in, hist, cw, cb, ng, nb):
    c = MIX_WIDTH
    small = (w_in, hist, cw, cb, ng, nb)
    return pl.pallas_call(
        functools.partial(_sample_mix_body, is_a, n_seq),
        grid=(1,),
        in_specs=[pl.BlockSpec((rows, D_MODEL), lambda i: (x_block, 0))] + [_full(a.shape) for a in small],
        out_specs=[_full((rows, c)), _full((rows, XATTN_WIDTH)), _full(hist.shape)],
        out_shape=[jax.ShapeDtypeStruct((rows, c), jnp.float32),
                   jax.ShapeDtypeStruct((rows, XATTN_WIDTH), jnp.float32),
                   jax.ShapeDtypeStruct(hist.shape, jnp.float32)],
        compiler_params=pltpu.CompilerParams(dimension_semantics=("arbitrary",),
                                             vmem_limit_bytes=_vmem_limit(48 * 1024 * 1024)),
        name="sample_mix_a" if is_a else "sample_mix_b",
    )(x, *small)


def _sample_attn_body(n_seq, q_ref, k_ref, v_ref, o_ref, bias_ref):
    bb = k_ref.shape[1]
    n_pos = q_ref.shape[0] // n_seq
    first = pl.multiple_of(pl.program_id(0) * bb, bb)

    @pl.when(pl.program_id(0) == 0)
    def _():
        r = lax.broadcasted_iota(jnp.int32, bias_ref.shape, 0)
        col = lax.broadcasted_iota(jnp.int32, bias_ref.shape, 1)
        valid = ((r % N_XHEADS) == (col // (n_pos * bb))) & ((r // (N_MEM * N_XHEADS)) == (col % bb))
        bias_ref[...] = jnp.where(valid, 0.0, NEG_INF)

    q = jnp.concatenate([q_ref[pl.ds(l * n_seq + first, bb), :] for l in range(n_pos)], axis=0)
    nq = n_pos * bb
    n_rows = bb * N_MEM * N_XHEADS
    k_rows = k_ref[0].reshape(n_rows, XHEAD_DIM)
    v_rows = v_ref[0].reshape(n_rows, XHEAD_DIM)
    q_heads = jnp.concatenate([q[:, h * XHEAD_DIM:(h + 1) * XHEAD_DIM] for h in range(N_XHEADS)], axis=0)
    s = _dot_nt(_bf(k_rows), _bf(q_heads)) * (XHEAD_DIM ** -0.5) + bias_ref[...]
    e = jnp.exp(s - jnp.max(s, axis=0, keepdims=True))
    p = e / jnp.sum(e, axis=0, keepdims=True)
    o_heads = _dot_tn(_bf(p), _bf(v_rows))
    o = jnp.concatenate([o_heads[h * nq:(h + 1) * nq] for h in range(N_XHEADS)], axis=1)
    for l in range(n_pos):
        o_ref[pl.ds(l * n_seq + first, bb), :] = o[l * bb:(l + 1) * bb, :]


def _sample_attention(layer, n_seq, q, mem_k, mem_v):
    rows = q.shape[0]
    bb = SAMPLE_BATCH_BLOCK
    mem_spec = pl.BlockSpec((1, bb, N_MEM, N_XHEADS, XHEAD_DIM), lambda i: (layer, i, 0, 0, 0))
    return pl.pallas_call(
        functools.partial(_sample_attn_body, n_seq),
        grid=(n_seq // bb,),
        in_specs=[pl.BlockSpec((rows, XATTN_WIDTH), lambda i: (0, 0)), mem_spec, mem_spec],
        out_specs=pl.BlockSpec((rows, XATTN_WIDTH), lambda i: (0, 0)),
        out_shape=jax.ShapeDtypeStruct((rows, XATTN_WIDTH), jnp.float32),
        scratch_shapes=[pltpu.VMEM((bb * N_MEM * N_XHEADS, N_XHEADS * (rows // n_seq) * bb), jnp.float32)],
        compiler_params=pltpu.CompilerParams(dimension_semantics=("arbitrary",),
                                             vmem_limit_bytes=_vmem_limit(40 * 1024 * 1024)),
        name="sample_attention",
    )(q, mem_k, mem_v)


def _sample_out_body(x_ref, mix_ref, attn_ref, w_out_ref, g1_ref, b1_ref, y_ref):
    cat = jnp.concatenate([mix_ref[...], attn_ref[...]], axis=-1)
    out = _dot(_bf(cat), w_out_ref[...])
    y_ref[...] = _layer_norm(DEEPNORM_ALPHA * x_ref[...] + out, g1_ref[...], b1_ref[...])


def _sample_out(rows, x, x_block, mix, attn, w_out, g1, b1):
    small = (mix, attn, w_out, g1, b1)
    return pl.pallas_call(
        _sample_out_body,
        grid=(1,),
        in_specs=[pl.BlockSpec((rows, D_MODEL), lambda i: (x_block, 0))] + [_full(a.shape) for a in small],
        out_specs=_full((rows, D_MODEL)),
        out_shape=jax.ShapeDtypeStruct((rows, D_MODEL), jnp.float32),
        compiler_params=pltpu.CompilerParams(dimension_semantics=("arbitrary",),
                                             vmem_limit_bytes=_vmem_limit(32 * 1024 * 1024)),
        name="sample_out",
    )(x, *small)


def _first_index_of(mask, index, n):
    cand = jnp.where(mask, index, float(n))
    while cand.ndim > 2:
        cand = jnp.min(cand, axis=0)
    return jnp.min(cand, axis=0, keepdims=True)


def _max_all(x):
    while x.ndim > 2:
        x = jnp.max(x, axis=0)
    return jnp.max(x, axis=0, keepdims=True)


def _routing_stages(xb, wr_ref, bias_ref, out):
    tm = xb.shape[0]
    scores = _sigmoid(_dot_nt(wr_ref[...], xb))
    biased = scores + bias_ref[...]
    grp = biased.reshape(N_GROUPS, GROUP_SIZE, tm)

    within = lax.broadcasted_iota(jnp.int32, grp.shape, 1).astype(jnp.float32)
    top1 = jnp.max(grp, axis=1, keepdims=True)
    first = jnp.min(jnp.where(grp == top1, within, float(GROUP_SIZE)), axis=1, keepdims=True)
    top2 = jnp.max(jnp.where(within == first, NEG_INF, grp), axis=1, keepdims=True)
    grp_score = (top1 + top2).reshape(N_GROUPS, tm)
    yield

    gidx = lax.broadcasted_iota(jnp.int32, grp_score.shape, 0).astype(jnp.float32)
    grp_sel = jnp.zeros(grp_score.shape, jnp.float32)
    for _ in range(TOPK_GROUPS):
        best = jnp.max(grp_score, axis=0, keepdims=True)
        pick = gidx == _first_index_of(grp_score == best, gidx, N_GROUPS)
        grp_sel = jnp.where(pick, 1.0, grp_sel)
        grp_score = jnp.where(pick, NEG_INF, grp_score)
    yield

    eidx = (lax.broadcasted_iota(jnp.int32, grp.shape, 0) * GROUP_SIZE
            + lax.broadcasted_iota(jnp.int32, grp.shape, 1)).astype(jnp.float32)
    cand = jnp.where(grp_sel.reshape(N_GROUPS, 1, tm) > 0.0, grp, NEG_INF)
    chosen = jnp.zeros(grp.shape, jnp.float32)
    for k in range(TOP_K):
        best = _max_all(cand).reshape(1, 1, tm)
        pick = eidx == _first_index_of(cand == best, eidx, N_EXPERTS).reshape(1, 1, tm)
        chosen = jnp.where(pick, 1.0, chosen)
        cand = jnp.where(pick, NEG_INF, cand)
        if k % 2 == 1:
            yield

    w = jnp.where(chosen > 0.0, scores.reshape(grp.shape), 0.0)
    total = jnp.sum(jnp.sum(w, axis=0), axis=0, keepdims=True).reshape(1, 1, tm)
    gates = (w / total * ROUTED_SCALE).reshape(N_EXPERTS, tm)
    chosen = chosen.reshape(N_EXPERTS, tm)
    yield

    earlier = (lax.broadcasted_iota(jnp.int32, (tm, tm), 0) < lax.broadcasted_iota(jnp.int32, (tm, tm), 1))
    pos = _dot(_bf(chosen), jnp.where(earlier, 1.0, 0.0).astype(jnp.bfloat16))
    routed = chosen > 0.0
    in_window = routed & (pos < float(SLOT_WINDOW))
    out["sel"] = jnp.where(in_window, pos, -1.0)
    out["gate"] = jnp.where(in_window, gates, 0.0)
    out["over"] = jnp.where(routed & (pos >= float(SLOT_WINDOW)), gates, 0.0)
    yield


def _slot_matches(sel_row, first_slot=0, n_slots=None):
    tm = sel_row.shape[1]
    n_slots = SLOT_WINDOW if n_slots is None else n_slots
    slot = (lax.broadcasted_iota(jnp.int32, (n_slots, tm), 0) + first_slot).astype(jnp.float32)
    return jnp.broadcast_to(sel_row, (n_slots, tm)) == slot


def _slot_onehot(sel, first_expert):
    rows = [_slot_matches(sel[e:e + 1, :]) for e in range(first_expert, first_expert + EXPERT_CHUNK)]
    return jnp.where(jnp.concatenate(rows, axis=0), 1.0, 0.0).astype(jnp.bfloat16)


def _route_body(x_ref, wr_ref, bias_ref, xs_ref, sel_ref, gate_ref, over_ref, flag_ref, fill_ref, xb_prev, sel_prev):
    tm = x_ref.shape[0]
    n_groups = SLOT_WINDOW // SLOT_GROUP

    @pl.when(pl.program_id(0) == 0)
    def _():
        xb_prev[...] = jnp.zeros(xb_prev.shape, jnp.bfloat16)
        sel_prev[...] = jnp.full(sel_prev.shape, -1.0, jnp.float32)

    xb_old, sel_old = xb_prev[...], sel_prev[...]
    xb = _bf(x_ref[...])
    out = {}
    stages = _routing_stages(xb, wr_ref, bias_ref, out)
    onehot = _slot_onehot(sel_old, 0)
    for first in range(0, N_EXPERTS, EXPERT_CHUNK):
        later = _slot_onehot(sel_old, first + EXPERT_CHUNK) if first + EXPERT_CHUNK < N_EXPERTS else None
        next(stages, None)
        slots = _bf(_dot(onehot, xb_old))
        xs_ref[first:first + EXPERT_CHUNK, :, 0] = slots.reshape(EXPERT_CHUNK, n_groups, SLOT_GROUP, D_MODEL)
        onehot = later
    for _ in stages:
        pass

    sel, over = out["sel"], out["over"]
    sel_ref[...] = sel
    gate_ref[...] = out["gate"]
    pad = jnp.zeros((GATE_LANES - N_EXPERTS, tm), jnp.float32)
    over_ref[...] = jnp.concatenate([over, pad], axis=0).T
    flag_ref[...] = jnp.broadcast_to(jnp.max(jnp.max(over, axis=0, keepdims=True), axis=1, keepdims=True),
                                     flag_ref.shape[1:])[None]
    fill_ref[...] = jnp.broadcast_to(jnp.max(sel, axis=1, keepdims=True), fill_ref.shape[1:])[None]
    xb_prev[...] = xb
    sel_prev[...] = sel


def _route_dispatch(x, w_router_t, bias_col):
    rows = x.shape[0]
    tm = TOKEN_TILE
    n_tiles = rows // tm
    n_groups = SLOT_WINDOW // SLOT_GROUP
    routed = lambda i: jnp.minimum(i, n_tiles - 1)
    return pl.pallas_call(
        _route_body,
        grid=(n_tiles + 1,),
        in_specs=[pl.BlockSpec((tm, D_MODEL), lambda i: (routed(i), 0)),
                  _full((N_EXPERTS, D_MODEL)), _full((N_EXPERTS, 1))],
        out_specs=[pl.BlockSpec((N_EXPERTS, n_groups, 1, SLOT_GROUP, D_MODEL),
                                lambda i: (0, 0, jnp.maximum(i - 1, 0), 0, 0)),
                   pl.BlockSpec((N_EXPERTS, tm), lambda i: (0, routed(i))),
                   pl.BlockSpec((N_EXPERTS, tm), lambda i: (0, routed(i))),
                   pl.BlockSpec((tm, GATE_LANES), lambda i: (routed(i), 0)),
                   pl.BlockSpec((1, 8, V7X_LANES), lambda i: (routed(i), 0, 0)),
                   pl.BlockSpec((1, N_EXPERTS, V7X_LANES), lambda i: (routed(i), 0, 0))],
        out_shape=[jax.ShapeDtypeStruct((N_EXPERTS, n_groups, n_tiles, SLOT_GROUP, D_MODEL), jnp.bfloat16),
                   jax.ShapeDtypeStruct((N_EXPERTS, rows), jnp.float32),
                   jax.ShapeDtypeStruct((N_EXPERTS, rows), jnp.float32),
                   jax.ShapeDtypeStruct((rows, GATE_LANES), jnp.float32),
                   jax.ShapeDtypeStruct((n_tiles, 8, V7X_LANES), jnp.float32),
                   jax.ShapeDtypeStruct((n_tiles, N_EXPERTS, V7X_LANES), jnp.float32)],
        scratch_shapes=[pltpu.VMEM((tm, D_MODEL), jnp.bfloat16), pltpu.VMEM((N_EXPERTS, tm), jnp.float32)],
        compiler_params=pltpu.CompilerParams(dimension_semantics=("arbitrary",),
                                             vmem_limit_bytes=_vmem_limit(48 * 1024 * 1024)),
        name="route_dispatch",
    )(x, w_router_t, bias_col)


def _expert_body(need_ref, last_e_ref, last_c_ref, xm_ref, xt_ref, sel_ref, gate_ref, wg_ref, wu_ref, wd_ref,
                 ym_ref, yt_ref, wg_bf, wu_bf, wd_bf):
    del last_e_ref, last_c_ref

    @pl.when(pl.program_id(1) == 0)
    def _():
        wg_bf[...] = _bf(wg_ref[0, 0])
        wu_bf[...] = _bf(wu_ref[0, 0])
        wd_bf[...] = _bf(wd_ref[0, 0])

    row = pl.ds(pl.program_id(0) % V7X_SUBLANES, 1)
    n_main = xm_ref.shape[1]
    n_tiles = xm_ref.shape[2]
    tm = sel_ref.shape[1] // n_tiles

    def run(x_ref, y_ref, first_group):
        n_g = x_ref.shape[1]
        rows = n_tiles * SLOT_GROUP
        proj = None
        for g in range(n_g + 1):
            nxt = None
            if g < n_g:
                x = x_ref[0, g].reshape(rows, D_MODEL)
                nxt = (_dot(x, wg_bf[...]), _dot(x, wu_bf[...]))
            if proj is not None:
                slot_gates = []
                for t in range(n_tiles):
                    cols = slice(t * tm, (t + 1) * tm)
                    match = _slot_matches(sel_ref[row, cols], (first_group + g - 1) * SLOT_GROUP, SLOT_GROUP)
                    slot_gates.append(jnp.sum(jnp.where(match, gate_ref[row, cols], 0.0), axis=-1, keepdims=True))
                gate = jnp.concatenate(slot_gates, axis=0)
                h = _silu(proj[0]) * proj[1]
                y = _bf(_dot(_bf(h), wd_bf[...]) * gate)
                y_ref[0, g - 1] = y.reshape(n_tiles, SLOT_GROUP, D_MODEL)
            proj = nxt

    run(xm_ref, ym_ref, 0)

    @pl.when(need_ref[pl.program_id(0) * pl.num_programs(1) + pl.program_id(1)] > 0)
    def _():
        run(xt_ref, yt_ref, n_main)


def _slot_group_split():
    n_groups = SLOT_WINDOW // SLOT_GROUP
    n_main = min(SLOT_GROUPS_ALWAYS, n_groups - 1)
    n_tail = n_groups - n_main
    assert n_main % n_tail == 0, "the trailing groups must form one block of the group axis"
    return n_main, n_tail


def _experts(layer, xs, sel, gate, fill, w_gate, w_up, w_down):
    n_e, _, n_tiles_all, _, _ = xs.shape
    n_chunks = EXPERT_ROW_CHUNKS
    n_tiles = n_tiles_all // n_chunks
    tokens = sel.shape[1] // n_chunks
    n_main, n_tail = _slot_group_split()

    need = (jnp.max(fill.reshape(n_e, n_chunks, n_tiles), axis=-1) >= n_main * SLOT_GROUP).astype(jnp.int32)
    steps = jnp.arange(n_e * n_chunks, dtype=jnp.int32)
    last = jnp.maximum(lax.cummax(jnp.where(need.reshape(-1) > 0, steps, -1)), 0)
    last_e, last_c = last // n_chunks, last % n_chunks

    w_in_spec = pl.BlockSpec((1, 1, D_MODEL, EXPERT_FF), lambda e, c, *_: (layer, e, 0, 0))
    route_spec = pl.BlockSpec((V7X_SUBLANES, tokens), lambda e, c, *_: (e // V7X_SUBLANES, c))
    main_spec = pl.BlockSpec((1, n_main, n_tiles, SLOT_GROUP, D_MODEL), lambda e, c, *_: (e, 0, c, 0, 0))

    def tail_map(group_block):
        return lambda e, c, nd, le, lc: (le[e * n_chunks + c], group_block, lc[e * n_chunks + c], 0, 0)

    tail_block = (1, n_tail, n_tiles, SLOT_GROUP, D_MODEL)
    ym, yt = pl.pallas_call(
        _expert_body,
        grid_spec=pltpu.PrefetchScalarGridSpec(
            num_scalar_prefetch=3,
            grid=(n_e, n_chunks),
            in_specs=[main_spec, pl.BlockSpec(tail_block, tail_map(n_main // n_tail)),
                      route_spec, route_spec, w_in_spec, w_in_spec,
                      pl.BlockSpec((1, 1, EXPERT_FF, D_MODEL), lambda e, c, *_: (layer, e, 0, 0))],
            out_specs=[main_spec, pl.BlockSpec(tail_block, tail_map(0))],
            scratch_shapes=[pltpu.VMEM((D_MODEL, EXPERT_FF), jnp.bfloat16),
                            pltpu.VMEM((D_MODEL, EXPERT_FF), jnp.bfloat16),
                            pltpu.VMEM((EXPERT_FF, D_MODEL), jnp.bfloat16)],
        ),
        out_shape=[jax.ShapeDtypeStruct((n_e, n_main, n_tiles_all, SLOT_GROUP, D_MODEL), jnp.bfloat16),
                   jax.ShapeDtypeStruct((n_e, n_tail, n_tiles_all, SLOT_GROUP, D_MODEL), jnp.bfloat16)],
        compiler_params=pltpu.CompilerParams(dimension_semantics=("arbitrary", "arbitrary"),
                                             vmem_limit_bytes=_vmem_limit(56 * 1024 * 1024)),
        name="experts",
    )(need.reshape(-1), last_e, last_c, xs, xs, sel, gate, w_gate, w_up, w_down)
    return ym, yt, need


def _combine_body(has_extra, n_prompt_tiles, tiles_per_chunk, tile_need_ref, last_ref, need_ref,
                  x_ref, sel_ref, ym_ref, yt_ref, wsg_ref, wsu_ref, wsd_ref, g2_ref, b2_ref, *rest):
    del last_ref
    rest = list(rest)
    extra_ref = rest.pop(0) if has_extra else None
    out_refs = [rest.pop(0)] if n_prompt_tiles is None else [rest.pop(0), rest.pop(0)]
    wsg_bf, wsu_bf, wsd_bf, acc_ref = rest
    tile = pl.program_id(0)
    n_main, n_tail = ym_ref.shape[1], yt_ref.shape[1]
    n_chunks = need_ref.shape[0] // N_EXPERTS

    @pl.when(tile == 0)
    def _():
        wsg_bf[...] = _bf(wsg_ref[0])
        wsu_bf[...] = _bf(wsu_ref[0])
        wsd_bf[...] = _bf(wsd_ref[0])

    def onehot(first_expert, first_group, n_g):
        rows = [_slot_matches(sel_ref[e:e + 1, :], first_group * SLOT_GROUP, n_g * SLOT_GROUP)
                for e in range(first_expert, first_expert + COMBINE_CHUNK)]
        return jnp.where(jnp.concatenate(rows, axis=0), 1.0, 0.0).astype(jnp.bfloat16)

    x = x_ref[...]
    xb = _bf(x)
    ahead = onehot(0, 0, n_main)
    y = _dot(_bf(_silu(_dot(xb, wsg_bf[...])) * _dot(xb, wsu_bf[...])), wsd_bf[...])
    for first in range(0, N_EXPERTS, COMBINE_CHUNK):
        current = ahead
        if first + COMBINE_CHUNK < N_EXPERTS:
            ahead = onehot(first + COMBINE_CHUNK, 0, n_main)
        ys = ym_ref[first:first + COMBINE_CHUNK, :, 0].reshape(COMBINE_CHUNK * n_main * SLOT_GROUP, D_MODEL)
        y = y + _dot_tn(current, ys)
    if has_extra:
        y = y + extra_ref[...]
    acc_ref[...] = y

    @pl.when(tile_need_ref[tile] > 0)
    def _():
        chunk = tile // tiles_per_chunk
        part = jnp.zeros(acc_ref.shape, jnp.float32)
        for first in range(0, N_EXPERTS, COMBINE_CHUNK):
            rows = []
            for e in range(first, first + COMBINE_CHUNK):
                ye = yt_ref[e, :, 0].reshape(n_tail * SLOT_GROUP, D_MODEL)
                rows.append(jnp.where(need_ref[e * n_chunks + chunk] > 0, ye, jnp.zeros_like(ye)))
            part = part + _dot_tn(onehot(first, n_main, n_tail), jnp.concatenate(rows, axis=0))
        acc_ref[...] += part

    y = _layer_norm(DEEPNORM_ALPHA * x + acc_ref[...], g2_ref[...], b2_ref[...])

    if n_prompt_tiles is None:
        out_refs[0][...] = y
    else:
        @pl.when(tile < n_prompt_tiles)
        def _():
            out_refs[0][...] = y

        @pl.when(tile >= n_prompt_tiles)
        def _():
            out_refs[1][...] = y


def _combine(layer, x, sel, fill, ym, yt, need, ws_gate, ws_up, ws_down, g2, b2, extra=None, prompt_rows=None):
    rows = x.shape[0]
    tm = TOKEN_TILE
    n_tiles = rows // tm
    has_extra = extra is not None
    n_main, n_tail = ym.shape[1], yt.shape[1]
    tile_need = (jnp.max(fill, axis=0) >= n_main * SLOT_GROUP).astype(jnp.int32)
    tiles = jnp.arange(n_tiles, dtype=jnp.int32)
    last = jnp.maximum(lax.cummax(jnp.where(tile_need > 0, tiles, -1)), 0)

    in_specs = [pl.BlockSpec((tm, D_MODEL), lambda i, *_: (i, 0)),
                pl.BlockSpec((N_EXPERTS, tm), lambda i, *_: (0, i)),
                pl.BlockSpec((N_EXPERTS, n_main, 1, SLOT_GROUP, D_MODEL), lambda i, *_: (0, 0, i, 0, 0)),
                pl.BlockSpec((N_EXPERTS, n_tail, 1, SLOT_GROUP, D_MODEL), lambda i, tn, la, nd: (0, 0, la[i], 0, 0)),
                pl.BlockSpec((1, D_MODEL, EXPERT_FF), lambda i, *_: (layer, 0, 0)),
                pl.BlockSpec((1, D_MODEL, EXPERT_FF), lambda i, *_: (layer, 0, 0)),
                pl.BlockSpec((1, EXPERT_FF, D_MODEL), lambda i, *_: (layer, 0, 0)),
                pl.BlockSpec((1, D_MODEL), lambda i, *_: (0, 0)), pl.BlockSpec((1, D_MODEL), lambda i, *_: (0, 0))]
    args = [x, sel, ym, yt, ws_gate, ws_up, ws_down, g2, b2]
    if has_extra:
        in_specs.append(pl.BlockSpec((tm, D_MODEL), lambda i, *_: (i, 0)))
        args.append(extra)
    if prompt_rows is None:
        n_prompt_tiles = None
        out_specs = pl.BlockSpec((tm, D_MODEL), lambda i, *_: (i, 0))
        out_shape = jax.ShapeDtypeStruct((rows, D_MODEL), jnp.float32)
    else:
        n_prompt_tiles = prompt_rows // tm
        out_specs = [pl.BlockSpec((tm, D_MODEL), lambda i, *_: (jnp.minimum(i, n_prompt_tiles - 1), 0)),
                     pl.BlockSpec((tm, D_MODEL), lambda i, *_: (jnp.maximum(i - n_prompt_tiles, 0), 0))]
        out_shape = [jax.ShapeDtypeStruct((prompt_rows, D_MODEL), jnp.float32),
                     jax.ShapeDtypeStruct((rows - prompt_rows, D_MODEL), jnp.float32)]
    return pl.pallas_call(
        functools.partial(_combine_body, has_extra, n_prompt_tiles, n_tiles // need.shape[1]),
        grid_spec=pltpu.PrefetchScalarGridSpec(
            num_scalar_prefetch=3,
            grid=(n_tiles,),
            in_specs=in_specs,
            out_specs=out_specs,
            scratch_shapes=[pltpu.VMEM((D_MODEL, EXPERT_FF), jnp.bfloat16),
                            pltpu.VMEM((D_MODEL, EXPERT_FF), jnp.bfloat16),
                            pltpu.VMEM((EXPERT_FF, D_MODEL), jnp.bfloat16),
                            pltpu.VMEM((tm, D_MODEL), jnp.float32)],
        ),
        out_shape=out_shape,
        compiler_params=pltpu.CompilerParams(dimension_semantics=("arbitrary",),
                                             vmem_limit_bytes=_vmem_limit(48 * 1024 * 1024)),
        name="combine_extra" if has_extra else "combine",
    )(tile_need, last, need.reshape(-1), *args)


def _dense_body(x_ref, gate_ref, wg_ref, wu_ref, wd_ref, y_ref, xb_ref):
    e = pl.program_id(1)

    @pl.when(e == 0)
    def _():
        xb_ref[...] = _bf(x_ref[...])
        y_ref[...] = jnp.zeros(y_ref.shape, jnp.float32)

    xb = xb_ref[...]
    h = _silu(_dot(xb, _bf(wg_ref[0, 0]))) * _dot(xb, _bf(wu_ref[0, 0]))
    down = _dot(_bf(h), _bf(wd_ref[0, 0]))
    lane = lax.broadcasted_iota(jnp.int32, gate_ref.shape, 1)
    gate = jnp.sum(jnp.where(lane == e, gate_ref[...], 0.0), axis=-1, keepdims=True)
    y_ref[...] += down * gate


def _dense_experts(layer, x, gates, w_gate, w_up, w_down):
    rows = x.shape[0]
    tm = DENSE_TILE
    w_in_spec = pl.BlockSpec((1, 1, D_MODEL, EXPERT_FF), lambda i, e: (layer, e, 0, 0))
    return pl.pallas_call(
        _dense_body,
        grid=(rows // tm, N_EXPERTS),
        in_specs=[pl.BlockSpec((tm, D_MODEL), lambda i, e: (i, 0)),
                  pl.BlockSpec((tm, GATE_LANES), lambda i, e: (i, 0)),
                  w_in_spec, w_in_spec,
                  pl.BlockSpec((1, 1, EXPERT_FF, D_MODEL), lambda i, e: (layer, e, 0, 0))],
        out_specs=pl.BlockSpec((tm, D_MODEL), lambda i, e: (i, 0)),
        out_shape=jax.ShapeDtypeStruct((rows, D_MODEL), jnp.float32),
        scratch_shapes=[pltpu.VMEM((tm, D_MODEL), jnp.bfloat16)],
        compiler_params=pltpu.CompilerParams(dimension_semantics=("arbitrary", "arbitrary"),
                                             vmem_limit_bytes=_vmem_limit(40 * 1024 * 1024)),
        name="dense_overflow",
    )(x, gates, w_gate, w_up, w_down)


def _channel_sublayer(layer, x, w_router_t, bias_col, w_gate, w_up, w_down, ws_gate, ws_up, ws_down, g2, b2,
                      prompt_rows=None):
    xs, sel, gate, over, flags, fill = _route_dispatch(x, w_router_t, bias_col)
    fill = fill[:, :, 0].T
    ym, yt, need = _experts(layer, xs, sel, gate, fill, w_gate, w_up, w_down)
    rest = (x, sel, fill, ym, yt, need, ws_gate, ws_up, ws_down, g2, b2)

    def with_overflow():
        extra = _dense_experts(layer, x, over, w_gate, w_up, w_down)
        return _combine(layer, *rest, extra=extra, prompt_rows=prompt_rows)

    def without_overflow():
        return _combine(layer, *rest, prompt_rows=prompt_rows)

    return lax.cond(jnp.max(flags) > 0.0, with_overflow, without_overflow)


def kernel(x_prompt, x_sample, mem_prompt, cache_mem_k, cache_mem_v, state_conv_a, state_conv_b, w_in_a, conv_a_w, conv_a_b, norm_a_g, norm_a_b, w_in_b, conv_b_w, w_kv, w_out, ln1_g, ln1_b, w_router, router_bias, w_gate, w_up, w_down, ws_gate, ws_up, ws_down, ln2_g, ln2_b):
    batch, seq, d = x_prompt.shape
    n_seq, n_pos, _ = x_sample.shape
    c = MIX_WIDTH
    p_rows, s_rows = batch * seq, n_pos * n_seq
    s_block = p_rows // s_rows
    row = lambda a: a.reshape(1, -1)

    x_p = x_prompt.reshape(p_rows, d)
    x_s, x_s_block = x_sample.transpose(1, 0, 2).reshape(s_rows, d), 0
    k_all, v_all = _kv_projection(mem_prompt.reshape(batch * N_MEM, d), w_kv)
    k_p = k_all.reshape(DEPTH, batch, N_MEM, XATTN_WIDTH)
    v_p = v_all.reshape(DEPTH, batch, N_MEM, XATTN_WIDTH)

    conv_a_p, conv_b_p, conv_a_s, conv_b_s = [], [], [], []
    for i in range(DEPTH):
        j = i // N_MIXERS
        is_a = i % N_MIXERS == 0
        if is_a:
            w_in, cw = _bf(w_in_a[j]), conv_a_w[j]
            cb, ng, nb = row(conv_a_b[j]), row(norm_a_g[j]), row(norm_a_b[j])
            hist_s = state_conv_a[j]
        else:
            w_in, cw = _bf(w_in_b[j]), conv_b_w[j]
            cb = ng = nb = jnp.zeros((1, c), jnp.float32)
            hist_s = state_conv_b[j]
        w_o = _bf(w_out[i])
        g1, b1 = row(ln1_g[i]), row(ln1_b[i])

        mix, q, hist_s_new = _sample_mix(is_a, n_seq, s_rows, x_s, x_s_block, w_in, hist_s.transpose(1, 0, 2),
                                         cw, cb, ng, nb)
        attn = _sample_attention(i, n_seq, q, cache_mem_k, cache_mem_v)
        h_s = _sample_out(s_rows, x_s, x_s_block, mix, attn, w_o, g1, b1)
        h, hist_p_new = _prompt_token_sublayer(is_a, i, x_p, h_s, batch, seq, w_in, cw, cb, ng, nb,
                                               k_p, v_p, w_o, g1, b1)
        hist_s_new = hist_s_new.transpose(1, 0, 2)
        if is_a:
            conv_a_p.append(hist_p_new)
            conv_a_s.append(hist_s_new)
        else:
            conv_b_p.append(hist_p_new)
            conv_b_s.append(hist_s_new)

        last = i == DEPTH - 1
        h = _channel_sublayer(i, h, _bf(w_router[i].T), router_bias[i].reshape(N_EXPERTS, 1),
                              w_gate, w_up, w_down, ws_gate, ws_up, ws_down, row(ln2_g[i]), row(ln2_b[i]),
                              prompt_rows=p_rows if last else None)
        if not last:
            x_p = h
            x_s, x_s_block = h, s_block

    y_p, y_s = h
    new_k = k_all.reshape(DEPTH, batch, N_MEM, N_XHEADS, XHEAD_DIM)
    new_v = v_all.reshape(DEPTH, batch, N_MEM, N_XHEADS, XHEAD_DIM)
    return (y_p.reshape(batch, seq, d), y_s.reshape(n_pos, n_seq, d).transpose(1, 0, 2), new_k, new_v,
            jnp.stack(conv_a_p), jnp.stack(conv_b_p), jnp.stack(conv_a_s), jnp.stack(conv_b_s))
```

```python
import functools

import jax
import jax.numpy as jnp
from jax import lax
from jax.experimental import pallas as pl
from jax.experimental.pallas import tpu as pltpu

D_MODEL = 1024
DEPTH = 2
N_MIXERS = 2
MIX_WIDTH = D_MODEL // 2
N_MEM = 256
N_XHEADS = 4
XHEAD_DIM = MIX_WIDTH // N_XHEADS
XATTN_WIDTH = N_XHEADS * XHEAD_DIM
CONV_A_WIDTH = 31
CONV_B_WIDTH = 3
N_EXPERTS = 64
TOP_K = 8
N_GROUPS = 8
GROUP_SIZE = N_EXPERTS // N_GROUPS
TOPK_GROUPS = 4
EXPERT_FF = D_MODEL // 4
ROUTED_SCALE = 2.5
LN_EPS = 1e-5
DEEPNORM_ALPHA = (2 * DEPTH) ** 0.25

V7X_LANES = 128
V7X_SUBLANES = 8
V7X_VMEM_BYTES = 64 * 1024 * 1024

HIST_PAD = 32
PROMPT_SEQ_TILE = 512
PROMPT_ROW_GROUPS = 1
SAMPLE_BATCH_BLOCK = 8
TOKEN_TILE = 256
SLOT_WINDOW = 64
SLOT_GROUP = 16
SLOT_GROUPS_ALWAYS = 3
EXPERT_CHUNK = 8
ROUTE_TILES_PER_STEP = 2
COMBINE_TILES_PER_STEP = 2
COMBINE_CHUNK = 16
EXPERT_ROW_CHUNKS = 1
GATE_LANES = V7X_LANES
DENSE_TILE = 512
NEG_INF = float("-inf")


def _vmem_limit(nbytes):
    return int(min(max(nbytes, 16 * 1024 * 1024), V7X_VMEM_BYTES - 8 * 1024 * 1024))


def _bf(x):
    return x.astype(jnp.bfloat16)


def _dot(a, b):
    return jnp.dot(a, b, preferred_element_type=jnp.float32)


def _dot_nt(a, b):
    return lax.dot_general(a, b, (((1,), (1,)), ((), ())), preferred_element_type=jnp.float32)


def _dot_tn(a, b):
    return lax.dot_general(a, b, (((0,), (0,)), ((), ())), preferred_element_type=jnp.float32)


def _sigmoid(x):
    return 1.0 / (1.0 + jnp.exp(-x))


def _silu(x):
    return x * _sigmoid(x)


def _layer_norm(x, g, b):
    mu = jnp.mean(x, axis=-1, keepdims=True)
    xc = x - mu
    var = jnp.mean(xc * xc, axis=-1, keepdims=True)
    return xc * lax.rsqrt(var + LN_EPS) * g + b


def _memory_attention(q, k_head, v_head):
    outs = []
    for h in range(N_XHEADS):
        sl = slice(h * XHEAD_DIM, (h + 1) * XHEAD_DIM)
        s = _dot_nt(_bf(q[:, sl]), _bf(k_head(h))) * (XHEAD_DIM ** -0.5)
        e = jnp.exp(s - jnp.max(s, axis=-1, keepdims=True))
        p = e / jnp.sum(e, axis=-1, keepdims=True)
        outs.append(_dot(_bf(p), _bf(v_head(h))))
    return jnp.concatenate(outs, axis=-1)


def _full(shape):
    return pl.BlockSpec(shape, lambda *_: tuple(0 for _ in shape))


def _kv_body(mem_ref, w_ref, k_ref, v_ref):
    kv = _dot(_bf(mem_ref[...]), _bf(w_ref[0]))
    k_ref[0] = kv[:, :XATTN_WIDTH]
    v_ref[0] = kv[:, XATTN_WIDTH:]


def _kv_projection(mem2d, w_kv):
    rows = mem2d.shape[0]
    tm = 512
    out = jax.ShapeDtypeStruct((DEPTH, rows, XATTN_WIDTH), jnp.float32)
    return pl.pallas_call(
        _kv_body,
        grid=(DEPTH, rows // tm),
        in_specs=[pl.BlockSpec((tm, D_MODEL), lambda i, m: (m, 0)),
                  pl.BlockSpec((1, D_MODEL, 2 * XATTN_WIDTH), lambda i, m: (i, 0, 0))],
        out_specs=[pl.BlockSpec((1, tm, XATTN_WIDTH), lambda i, m: (i, m, 0)),
                   pl.BlockSpec((1, tm, XATTN_WIDTH), lambda i, m: (i, m, 0))],
        out_shape=[out, out],
        compiler_params=pltpu.CompilerParams(dimension_semantics=("arbitrary", "arbitrary"),
                                             vmem_limit_bytes=_vmem_limit(32 * 1024 * 1024)),
        name="kv_projection",
    )(mem2d, w_kv)


def _prompt_token_body(is_a, nl, n_tiles, x_ref, w_in_ref, cw_ref, cb_ref, ng_ref, nb_ref, k_ref, v_ref, w_out_ref,
                       g1_ref, b1_ref, sample_ref, y_ref, hist_ref, buf_ref):
    c = MIX_WIDTH
    tl = x_ref.shape[0]
    width = CONV_A_WIDTH if is_a else CONV_B_WIDTH
    step = pl.program_id(0)
    seq_step = step % nl

    @pl.when(step == n_tiles)
    def _():
        y_ref[...] = sample_ref[...]

    @pl.when((step < n_tiles) & (seq_step == 0))
    def _():
        buf_ref[pl.ds(0, HIST_PAD), :] = jnp.zeros((HIST_PAD, c), jnp.float32)

    @pl.when(step < n_tiles)
    def _():
        rg = tl // PROMPT_ROW_GROUPS
        head = lambda ref: lambda h: ref[0, 0, :, h * XHEAD_DIM:(h + 1) * XHEAD_DIM]

        def group_stages(rows):
            x = x_ref[rows, :]
            u = _dot(_bf(x), w_in_ref[...])
            conv_in = u[:, :c] * _sigmoid(u[:, c:2 * c]) if is_a else u[:, c:2 * c] * u[:, 2 * c:3 * c]
            buf_ref[pl.ds(HIST_PAD + rows.start, rg), :] = conv_in
            yield

            base = HIST_PAD - (width - 1) + rows.start
            conv = None
            for phase in range(V7X_SUBLANES):
                taps = [t for t in range(width) if (base + t) % V7X_SUBLANES == phase]
                if not taps:
                    continue
                n = rg if phase == 0 else rg + V7X_SUBLANES
                part = None
                for t in taps:
                    term = cw_ref[t:t + 1, :] * buf_ref[pl.ds(base + t - phase, n), :]
                    part = term if part is None else part + term
                part = part[phase:phase + rg, :]
                conv = part if conv is None else conv + part

            if is_a:
                mix = _silu(_layer_norm(conv + cb_ref[...], ng_ref[...], nb_ref[...]))
                q = u[:, 2 * c:]
            else:
                mix = u[:, :c] * conv
                q = u[:, 3 * c:]
            yield
            attn = _memory_attention(q, head(k_ref), head(v_ref))
            yield
            out = _dot(_bf(jnp.concatenate([mix, attn], axis=-1)), w_out_ref[...])
            y_ref[rows, :] = _layer_norm(DEEPNORM_ALPHA * x + out, g1_ref[...], b1_ref[...])
            yield

        gens = [group_stages(slice(g * rg, (g + 1) * rg)) for g in range(PROMPT_ROW_GROUPS)]
        for gen in gens:
            next(gen)
        later_stages = 3
        for tick in range(later_stages + PROMPT_ROW_GROUPS - 1):
            for g, gen in enumerate(gens):
                if 0 <= tick - g < later_stages:
                    next(gen)

    @pl.when((step < n_tiles) & (seq_step == nl - 1))
    def _():
        hist_ref[0] = buf_ref[pl.ds(HIST_PAD + tl - (width - 1), width - 1), :]

    @pl.when(step < n_tiles)
    def _():
        buf_ref[pl.ds(0, HIST_PAD), :] = buf_ref[pl.ds(tl, HIST_PAD), :]


def _prompt_token_sublayer(is_a, layer, x, sample_rows, batch, seq, w_in, cw, cb, ng, nb, k, v, w_out, g1, b1):
    tl = PROMPT_SEQ_TILE
    assert sample_rows.shape == (tl, D_MODEL)
    nl = seq // tl
    n_tiles = batch * nl
    c = MIX_WIDTH
    width = CONV_A_WIDTH if is_a else CONV_B_WIDTH
    n_in = w_in.shape[1]
    tile = lambda s: jnp.minimum(s, n_tiles - 1)
    mem_spec = pl.BlockSpec((1, 1, N_MEM, XATTN_WIDTH), lambda s: (layer, tile(s) // nl, 0, 0))
    return pl.pallas_call(
        functools.partial(_prompt_token_body, is_a, nl, n_tiles),
        grid=(n_tiles + 1,),
        in_specs=[pl.BlockSpec((tl, D_MODEL), lambda s: (tile(s), 0)),
                  _full((D_MODEL, n_in)), _full((width, c)), _full((1, c)), _full((1, c)), _full((1, c)),
                  mem_spec, mem_spec,
                  _full((c + XATTN_WIDTH, D_MODEL)), _full((1, D_MODEL)), _full((1, D_MODEL)),
                  _full((tl, D_MODEL))],
        out_specs=[pl.BlockSpec((tl, D_MODEL), lambda s: (s, 0)),
                   pl.BlockSpec((1, width - 1, c), lambda s: (tile(s) // nl, 0, 0))],
        out_shape=[jax.ShapeDtypeStruct(((n_tiles + 1) * tl, D_MODEL), jnp.float32),
                   jax.ShapeDtypeStruct((batch, width - 1, c), jnp.float32)],
        scratch_shapes=[pltpu.VMEM((HIST_PAD + tl, c), jnp.float32)],
        compiler_params=pltpu.CompilerParams(dimension_semantics=("arbitrary",),
                                             vmem_limit_bytes=_vmem_limit(48 * 1024 * 1024)),
        name="prompt_token_a" if is_a else "prompt_token_b",
    )(x, w_in, cw, cb, ng, nb, k, v, w_out, g1, b1, sample_rows)


def _sample_mix_body(is_a, n_seq, x_ref, w_in_ref, hist_ref, cw_ref, cb_ref, ng_ref, nb_ref,
                     mix_ref, q_ref, new_hist_ref):
    c = MIX_WIDTH
    width = CONV_A_WIDTH if is_a else CONV_B_WIDTH
    n_hist = width - 1
    n_pos = x_ref.shape[0] // n_seq
    u = _dot(_bf(x_ref[...]), w_in_ref[...])
    if is_a:
        conv_in = u[:, :c] * _sigmoid(u[:, c:2 * c])
        q_ref[...] = u[:, 2 * c:]
    else:
        conv_in = u[:, c:2 * c] * u[:, 2 * c:3 * c]
        q_ref[...] = u[:, 3 * c:]

    def full_row(j):
        if j < n_hist:
            return hist_ref[j]
        return conv_in[(j - n_hist) * n_seq:(j - n_hist + 1) * n_seq, :]

    for l in range(n_pos):
        conv = cw_ref[0:1, :] * full_row(l)
        for t in range(1, width):
            conv = conv + cw_ref[t:t + 1, :] * full_row(l + t)
        rows = slice(l * n_seq, (l + 1) * n_seq)
        if is_a:
            mix_ref[rows, :] = _silu(_layer_norm(conv + cb_ref[...], ng_ref[...], nb_ref[...]))
        else:
            mix_ref[rows, :] = u[rows, :c] * conv
    for j in range(n_hist):
        new_hist_ref[j] = full_row(j + n_pos)


def _sample_mix(is_a, n_seq, rows, x, x_block, w_in, hist, cw, cb, ng, nb):
    c = MIX_WIDTH
    small = (w_in, hist, cw, cb, ng, nb)
    return pl.pallas_call(
        functools.partial(_sample_mix_body, is_a, n_seq),
        grid=(1,),
        in_specs=[pl.BlockSpec((rows, D_MODEL), lambda i: (x_block, 0))] + [_full(a.shape) for a in small],
        out_specs=[_full((rows, c)), _full((rows, XATTN_WIDTH)), _full(hist.shape)],
        out_shape=[jax.ShapeDtypeStruct((rows, c), jnp.float32),
                   jax.ShapeDtypeStruct((rows, XATTN_WIDTH), jnp.float32),
                   jax.ShapeDtypeStruct(hist.shape, jnp.float32)],
        compiler_params=pltpu.CompilerParams(dimension_semantics=("arbitrary",),
                                             vmem_limit_bytes=_vmem_limit(48 * 1024 * 1024)),
        name="sample_mix_a" if is_a else "sample_mix_b",
    )(x, *small)


def _sample_attn_body(n_seq, q_ref, k_ref, v_ref, o_ref, bias_ref):
    bb = k_ref.shape[1]
    n_pos = q_ref.shape[0] // n_seq
    first = pl.multiple_of(pl.program_id(0) * bb, bb)

    @pl.when(pl.program_id(0) == 0)
    def _():
        r = lax.broadcasted_iota(jnp.int32, bias_ref.shape, 0)
        col = lax.broadcasted_iota(jnp.int32, bias_ref.shape, 1)
        valid = ((r % N_XHEADS) == (col // (n_pos * bb))) & ((r // (N_MEM * N_XHEADS)) == (col % bb))
        bias_ref[...] = jnp.where(valid, 0.0, NEG_INF)

    q = jnp.concatenate([q_ref[pl.ds(l * n_seq + first, bb), :] for l in range(n_pos)], axis=0)
    nq = n_pos * bb
    n_rows = bb * N_MEM * N_XHEADS
    k_rows = k_ref[0].reshape(n_rows, XHEAD_DIM)
    v_rows = v_ref[0].reshape(n_rows, XHEAD_DIM)
    q_heads = jnp.concatenate([q[:, h * XHEAD_DIM:(h + 1) * XHEAD_DIM] for h in range(N_XHEADS)], axis=0)
    s = _dot_nt(_bf(k_rows), _bf(q_heads)) * (XHEAD_DIM ** -0.5) + bias_ref[...]
    e = jnp.exp(s - jnp.max(s, axis=0, keepdims=True))
    p = e / jnp.sum(e, axis=0, keepdims=True)
    o_heads = _dot_tn(_bf(p), _bf(v_rows))
    o = jnp.concatenate([o_heads[h * nq:(h + 1) * nq] for h in range(N_XHEADS)], axis=1)
    for l in range(n_pos):
        o_ref[pl.ds(l * n_seq + first, bb), :] = o[l * bb:(l + 1) * bb, :]


def _sample_attention(layer, n_seq, q, mem_k, mem_v):
    rows = q.shape[0]
    bb = SAMPLE_BATCH_BLOCK
    mem_spec = pl.BlockSpec((1, bb, N_MEM, N_XHEADS, XHEAD_DIM), lambda i: (layer, i, 0, 0, 0))
    return pl.pallas_call(
        functools.partial(_sample_attn_body, n_seq),
        grid=(n_seq // bb,),
        in_specs=[pl.BlockSpec((rows, XATTN_WIDTH), lambda i: (0, 0)), mem_spec, mem_spec],
        out_specs=pl.BlockSpec((rows, XATTN_WIDTH), lambda i: (0, 0)),
        out_shape=jax.ShapeDtypeStruct((rows, XATTN_WIDTH), jnp.float32),
        scratch_shapes=[pltpu.VMEM((bb * N_MEM * N_XHEADS, N_XHEADS * (rows // n_seq) * bb), jnp.float32)],
        compiler_params=pltpu.CompilerParams(dimension_semantics=("arbitrary",),
                                             vmem_limit_bytes=_vmem_limit(40 * 1024 * 1024)),
        name="sample_attention",
    )(q, mem_k, mem_v)


def _sample_out_body(x_ref, mix_ref, attn_ref, w_out_ref, g1_ref, b1_ref, y_ref):
    cat = jnp.concatenate([mix_ref[...], attn_ref[...]], axis=-1)
    out = _dot(_bf(cat), w_out_ref[...])
    y_ref[...] = _layer_norm(DEEPNORM_ALPHA * x_ref[...] + out, g1_ref[...], b1_ref[...])


def _sample_out(rows, x, x_block, mix, attn, w_out, g1, b1):
    small = (mix, attn, w_out, g1, b1)
    return pl.pallas_call(
        _sample_out_body,
        grid=(1,),
        in_specs=[pl.BlockSpec((rows, D_MODEL), lambda i: (x_block, 0))] + [_full(a.shape) for a in small],
        out_specs=_full((rows, D_MODEL)),
        out_shape=jax.ShapeDtypeStruct((rows, D_MODEL), jnp.float32),
        compiler_params=pltpu.CompilerParams(dimension_semantics=("arbitrary",),
                                             vmem_limit_bytes=_vmem_limit(32 * 1024 * 1024)),
        name="sample_out",
    )(x, *small)


def _first_index_of(mask, index, n):
    cand = jnp.where(mask, index, float(n))
    while cand.ndim > 2:
        cand = jnp.min(cand, axis=0)
    return jnp.min(cand, axis=0, keepdims=True)


def _max_all(x):
    while x.ndim > 2:
        x = jnp.max(x, axis=0)
    return jnp.max(x, axis=0, keepdims=True)


def _routing_stages(xb, wr_ref, bias_ref, out):
    tm = xb.shape[0]
    scores = _sigmoid(_dot_nt(wr_ref[...], xb))
    biased = scores + bias_ref[...]
    grp = biased.reshape(N_GROUPS, GROUP_SIZE, tm)

    within = lax.broadcasted_iota(jnp.int32, grp.shape, 1).astype(jnp.float32)
    top1 = jnp.max(grp, axis=1, keepdims=True)
    first = jnp.min(jnp.where(grp == top1, within, float(GROUP_SIZE)), axis=1, keepdims=True)
    top2 = jnp.max(jnp.where(within == first, NEG_INF, grp), axis=1, keepdims=True)
    grp_score = (top1 + top2).reshape(N_GROUPS, tm)
    yield

    gidx = lax.broadcasted_iota(jnp.int32, grp_score.shape, 0).astype(jnp.float32)
    grp_sel = jnp.zeros(grp_score.shape, jnp.float32)
    for _ in range(TOPK_GROUPS):
        best = jnp.max(grp_score, axis=0, keepdims=True)
        pick = gidx == _first_index_of(grp_score == best, gidx, N_GROUPS)
        grp_sel = jnp.where(pick, 1.0, grp_sel)
        grp_score = jnp.where(pick, NEG_INF, grp_score)
    yield

    eidx = (lax.broadcasted_iota(jnp.int32, grp.shape, 0) * GROUP_SIZE
            + lax.broadcasted_iota(jnp.int32, grp.shape, 1)).astype(jnp.float32)
    cand = jnp.where(grp_sel.reshape(N_GROUPS, 1, tm) > 0.0, grp, NEG_INF)
    chosen = jnp.zeros(grp.shape, jnp.float32)
    for k in range(TOP_K):
        best = _max_all(cand).reshape(1, 1, tm)
        pick = eidx == _first_index_of(cand == best, eidx, N_EXPERTS).reshape(1, 1, tm)
        chosen = jnp.where(pick, 1.0, chosen)
        cand = jnp.where(pick, NEG_INF, cand)
        if k % 2 == 1:
            yield

    w = jnp.where(chosen > 0.0, scores.reshape(grp.shape), 0.0)
    total = jnp.sum(jnp.sum(w, axis=0), axis=0, keepdims=True).reshape(1, 1, tm)
    gates = (w / total * ROUTED_SCALE).reshape(N_EXPERTS, tm)
    chosen = chosen.reshape(N_EXPERTS, tm)
    yield

    earlier = (lax.broadcasted_iota(jnp.int32, (tm, tm), 0) < lax.broadcasted_iota(jnp.int32, (tm, tm), 1))
    pos = _dot(_bf(chosen), jnp.where(earlier, 1.0, 0.0).astype(jnp.bfloat16))
    routed = chosen > 0.0
    in_window = routed & (pos < float(SLOT_WINDOW))
    out["sel"] = jnp.where(in_window, pos, -1.0)
    out["gate"] = jnp.where(in_window, gates, 0.0)
    out["over"] = jnp.where(routed & (pos >= float(SLOT_WINDOW)), gates, 0.0)
    yield


def _slot_matches(sel_row, first_slot=0, n_slots=None):
    tm = sel_row.shape[1]
    n_slots = SLOT_WINDOW if n_slots is None else n_slots
    slot = (lax.broadcasted_iota(jnp.int32, (n_slots, tm), 0) + first_slot).astype(jnp.float32)
    return jnp.broadcast_to(sel_row, (n_slots, tm)) == slot


def _slot_onehot(sel, first_expert):
    rows = [_slot_matches(sel[e:e + 1, :]) for e in range(first_expert, first_expert + EXPERT_CHUNK)]
    return jnp.where(jnp.concatenate(rows, axis=0), 1.0, 0.0).astype(jnp.bfloat16)


def _route_body(x_ref, wr_ref, bias_ref, xs_ref, sel_ref, gate_ref, over_ref, flag_ref, fill_ref, xb_prev, sel_prev):
    tm = TOKEN_TILE
    n_sub = x_ref.shape[0] // tm
    n_groups = SLOT_WINDOW // SLOT_GROUP

    @pl.when(pl.program_id(0) == 0)
    def _():
        xb_prev[...] = jnp.zeros(xb_prev.shape, jnp.bfloat16)
        sel_prev[...] = jnp.full(sel_prev.shape, -1.0, jnp.float32)

    lane = lax.broadcasted_iota(jnp.int32, (N_EXPERTS, V7X_LANES), 1)
    fill = jnp.full((N_EXPERTS, V7X_LANES), -1.0, jnp.float32)
    any_over = None
    for t in range(n_sub):
        cols = slice(t * tm, (t + 1) * tm)
        xb_old, sel_old = xb_prev[cols, :], sel_prev[:, cols]
        xb = _bf(x_ref[cols, :])
        out = {}
        stages = _routing_stages(xb, wr_ref, bias_ref, out)
        onehot = _slot_onehot(sel_old, 0)
        for first in range(0, N_EXPERTS, EXPERT_CHUNK):
            later = _slot_onehot(sel_old, first + EXPERT_CHUNK) if first + EXPERT_CHUNK < N_EXPERTS else None
            next(stages, None)
            slots = _bf(_dot(onehot, xb_old))
            xs_ref[first:first + EXPERT_CHUNK, :, t] = slots.reshape(EXPERT_CHUNK, n_groups, SLOT_GROUP, D_MODEL)
            onehot = later
        for _ in stages:
            pass

        sel, over = out["sel"], out["over"]
        sel_ref[:, cols] = sel
        gate_ref[:, cols] = out["gate"]
        pad = jnp.zeros((GATE_LANES - N_EXPERTS, tm), jnp.float32)
        over_ref[cols, :] = jnp.concatenate([over, pad], axis=0).T
        tile_over = jnp.max(jnp.max(over, axis=0, keepdims=True), axis=1, keepdims=True)
        any_over = tile_over if any_over is None else jnp.maximum(any_over, tile_over)
        fill = jnp.where(lane == t, jnp.max(sel, axis=1, keepdims=True), fill)
        xb_prev[cols, :] = xb
        sel_prev[:, cols] = sel
    flag_ref[...] = jnp.broadcast_to(any_over, flag_ref.shape[1:])[None]
    fill_ref[...] = fill[None]


def _route_dispatch(x, w_router_t, bias_col):
    rows = x.shape[0]
    n_sub = ROUTE_TILES_PER_STEP
    tm = TOKEN_TILE * n_sub
    n_tiles = rows // tm
    n_groups = SLOT_WINDOW // SLOT_GROUP
    routed = lambda i: jnp.minimum(i, n_tiles - 1)
    return pl.pallas_call(
        _route_body,
        grid=(n_tiles + 1,),
        in_specs=[pl.BlockSpec((tm, D_MODEL), lambda i: (routed(i), 0)),
                  _full((N_EXPERTS, D_MODEL)), _full((N_EXPERTS, 1))],
        out_specs=[pl.BlockSpec((N_EXPERTS, n_groups, n_sub, SLOT_GROUP, D_MODEL),
                                lambda i: (0, 0, jnp.maximum(i - 1, 0), 0, 0)),
                   pl.BlockSpec((N_EXPERTS, tm), lambda i: (0, routed(i))),
                   pl.BlockSpec((N_EXPERTS, tm), lambda i: (0, routed(i))),
                   pl.BlockSpec((tm, GATE_LANES), lambda i: (routed(i), 0)),
                   pl.BlockSpec((1, 8, V7X_LANES), lambda i: (routed(i), 0, 0)),
                   pl.BlockSpec((1, N_EXPERTS, V7X_LANES), lambda i: (routed(i), 0, 0))],
        out_shape=[jax.ShapeDtypeStruct((N_EXPERTS, n_groups, n_tiles * n_sub, SLOT_GROUP, D_MODEL), jnp.bfloat16),
                   jax.ShapeDtypeStruct((N_EXPERTS, rows), jnp.float32),
                   jax.ShapeDtypeStruct((N_EXPERTS, rows), jnp.float32),
                   jax.ShapeDtypeStruct((rows, GATE_LANES), jnp.float32),
                   jax.ShapeDtypeStruct((n_tiles, 8, V7X_LANES), jnp.float32),
                   jax.ShapeDtypeStruct((n_tiles, N_EXPERTS, V7X_LANES), jnp.float32)],
        scratch_shapes=[pltpu.VMEM((tm, D_MODEL), jnp.bfloat16), pltpu.VMEM((N_EXPERTS, tm), jnp.float32)],
        compiler_params=pltpu.CompilerParams(dimension_semantics=("arbitrary",),
                                             vmem_limit_bytes=_vmem_limit(56 * 1024 * 1024)),
        name="route_dispatch",
    )(x, w_router_t, bias_col)


def _expert_body(need_ref, last_e_ref, last_c_ref, xm_ref, xt_ref, sel_ref, gate_ref, wg_ref, wu_ref, wd_ref,
                 ym_ref, yt_ref, wg_bf, wu_bf, wd_bf):
    del last_e_ref, last_c_ref

    @pl.when(pl.program_id(1) == 0)
    def _():
        wg_bf[...] = _bf(wg_ref[0, 0])
        wu_bf[...] = _bf(wu_ref[0, 0])
        wd_bf[...] = _bf(wd_ref[0, 0])

    row = pl.ds(pl.program_id(0) % V7X_SUBLANES, 1)
    n_main = xm_ref.shape[1]
    n_tiles = xm_ref.shape[2]
    tm = sel_ref.shape[1] // n_tiles

    def run(x_ref, y_ref, first_group):
        n_g = x_ref.shape[1]
        rows = n_tiles * SLOT_GROUP
        proj = None
        for g in range(n_g + 1):
            nxt = None
            if g < n_g:
                x = x_ref[0, g].reshape(rows, D_MODEL)
                nxt = (_dot(x, wg_bf[...]), _dot(x, wu_bf[...]))
            if proj is not None:
                slot_gates = []
                for t in range(n_tiles):
                    cols = slice(t * tm, (t + 1) * tm)
                    match = _slot_matches(sel_ref[row, cols], (first_group + g - 1) * SLOT_GROUP, SLOT_GROUP)
                    slot_gates.append(jnp.sum(jnp.where(match, gate_ref[row, cols], 0.0), axis=-1, keepdims=True))
                gate = jnp.concatenate(slot_gates, axis=0)
                h = _silu(proj[0]) * proj[1]
                y = _bf(_dot(_bf(h), wd_bf[...]) * gate)
                y_ref[0, g - 1] = y.reshape(n_tiles, SLOT_GROUP, D_MODEL)
            proj = nxt

    run(xm_ref, ym_ref, 0)

    @pl.when(need_ref[pl.program_id(0) * pl.num_programs(1) + pl.program_id(1)] > 0)
    def _():
        run(xt_ref, yt_ref, n_main)


def _slot_group_split():
    n_groups = SLOT_WINDOW // SLOT_GROUP
    n_main = min(SLOT_GROUPS_ALWAYS, n_groups - 1)
    n_tail = n_groups - n_main
    assert n_main % n_tail == 0, "the trailing groups must form one block of the group axis"
    return n_main, n_tail


def _experts(layer, xs, sel, gate, fill, w_gate, w_up, w_down):
    n_e, _, n_tiles_all, _, _ = xs.shape
    n_chunks = EXPERT_ROW_CHUNKS
    n_tiles = n_tiles_all // n_chunks
    tokens = sel.shape[1] // n_chunks
    n_main, n_tail = _slot_group_split()

    need = (jnp.max(fill.reshape(n_e, n_chunks, n_tiles), axis=-1) >= n_main * SLOT_GROUP).astype(jnp.int32)
    steps = jnp.arange(n_e * n_chunks, dtype=jnp.int32)
    last = jnp.maximum(lax.cummax(jnp.where(need.reshape(-1) > 0, steps, -1)), 0)
    last_e, last_c = last // n_chunks, last % n_chunks

    w_in_spec = pl.BlockSpec((1, 1, D_MODEL, EXPERT_FF), lambda e, c, *_: (layer, e, 0, 0))
    route_spec = pl.BlockSpec((V7X_SUBLANES, tokens), lambda e, c, *_: (e // V7X_SUBLANES, c))
    main_spec = pl.BlockSpec((1, n_main, n_tiles, SLOT_GROUP, D_MODEL), lambda e, c, *_: (e, 0, c, 0, 0))

    def tail_map(group_block):
        return lambda e, c, nd, le, lc: (le[e * n_chunks + c], group_block, lc[e * n_chunks + c], 0, 0)

    tail_block = (1, n_tail, n_tiles, SLOT_GROUP, D_MODEL)
    ym, yt = pl.pallas_call(
        _expert_body,
        grid_spec=pltpu.PrefetchScalarGridSpec(
            num_scalar_prefetch=3,
            grid=(n_e, n_chunks),
            in_specs=[main_spec, pl.BlockSpec(tail_block, tail_map(n_main // n_tail)),
                      route_spec, route_spec, w_in_spec, w_in_spec,
                      pl.BlockSpec((1, 1, EXPERT_FF, D_MODEL), lambda e, c, *_: (layer, e, 0, 0))],
            out_specs=[main_spec, pl.BlockSpec(tail_block, tail_map(0))],
            scratch_shapes=[pltpu.VMEM((D_MODEL, EXPERT_FF), jnp.bfloat16),
                            pltpu.VMEM((D_MODEL, EXPERT_FF), jnp.bfloat16),
                            pltpu.VMEM((EXPERT_FF, D_MODEL), jnp.bfloat16)],
        ),
        out_shape=[jax.ShapeDtypeStruct((n_e, n_main, n_tiles_all, SLOT_GROUP, D_MODEL), jnp.bfloat16),
                   jax.ShapeDtypeStruct((n_e, n_tail, n_tiles_all, SLOT_GROUP, D_MODEL), jnp.bfloat16)],
        compiler_params=pltpu.CompilerParams(dimension_semantics=("arbitrary", "arbitrary"),
                                             vmem_limit_bytes=_vmem_limit(56 * 1024 * 1024)),
        name="experts",
    )(need.reshape(-1), last_e, last_c, xs, xs, sel, gate, w_gate, w_up, w_down)
    return ym, yt, need


def _combine_body(has_extra, n_prompt_tiles, tiles_per_chunk, tile_need_ref, last_ref, need_ref,
                  x_ref, sel_ref, ym_ref, yt_ref, wsg_ref, wsu_ref, wsd_ref, g2_ref, b2_ref, *rest):
    del last_ref
    rest = list(rest)
    extra_ref = rest.pop(0) if has_extra else None
    out_refs = [rest.pop(0)] if n_prompt_tiles is None else [rest.pop(0), rest.pop(0)]
    wsg_bf, wsu_bf, wsd_bf, acc_ref = rest
    step = pl.program_id(0)
    tm = TOKEN_TILE
    n_sub = x_ref.shape[0] // tm
    n_main, n_tail = ym_ref.shape[1], yt_ref.shape[1]
    n_chunks = need_ref.shape[0] // N_EXPERTS

    @pl.when(step == 0)
    def _():
        wsg_bf[...] = _bf(wsg_ref[0])
        wsu_bf[...] = _bf(wsu_ref[0])
        wsd_bf[...] = _bf(wsd_ref[0])

    for t in range(n_sub):
        rows_t = slice(t * tm, (t + 1) * tm)
        tile = step * n_sub + t

        def onehot(first_expert, first_group, n_g):
            rows = [_slot_matches(sel_ref[e:e + 1, rows_t], first_group * SLOT_GROUP, n_g * SLOT_GROUP)
                    for e in range(first_expert, first_expert + COMBINE_CHUNK)]
            return jnp.where(jnp.concatenate(rows, axis=0), 1.0, 0.0).astype(jnp.bfloat16)

        xb = _bf(x_ref[rows_t, :])
        ahead = onehot(0, 0, n_main)
        y = _dot(_bf(_silu(_dot(xb, wsg_bf[...])) * _dot(xb, wsu_bf[...])), wsd_bf[...])
        for first in range(0, N_EXPERTS, COMBINE_CHUNK):
            current = ahead
            if first + COMBINE_CHUNK < N_EXPERTS:
                ahead = onehot(first + COMBINE_CHUNK, 0, n_main)
            ys = ym_ref[first:first + COMBINE_CHUNK, :, t].reshape(COMBINE_CHUNK * n_main * SLOT_GROUP, D_MODEL)
            y = y + _dot_tn(current, ys)
        if has_extra:
            y = y + extra_ref[rows_t, :]
        acc_ref[rows_t, :] = y

        @pl.when(tile_need_ref[tile] > 0)
        def _():
            chunk = tile // tiles_per_chunk
            part = jnp.zeros((tm, D_MODEL), jnp.float32)
            for first in range(0, N_EXPERTS, COMBINE_CHUNK):
                rows = []
                for e in range(first, first + COMBINE_CHUNK):
                    ye = yt_ref[e, :, t].reshape(n_tail * SLOT_GROUP, D_MODEL)
                    rows.append(jnp.where(need_ref[e * n_chunks + chunk] > 0, ye, jnp.zeros_like(ye)))
                part = part + _dot_tn(onehot(first, n_main, n_tail), jnp.concatenate(rows, axis=0))
            acc_ref[rows_t, :] += part

    y = _layer_norm(DEEPNORM_ALPHA * x_ref[...] + acc_ref[...], g2_ref[...], b2_ref[...])

    if n_prompt_tiles is None:
        out_refs[0][...] = y
    else:
        @pl.when(step < n_prompt_tiles)
        def _():
            out_refs[0][...] = y

        @pl.when(step >= n_prompt_tiles)
        def _():
            out_refs[1][...] = y


def _combine(layer, x, sel, fill, ym, yt, need, ws_gate, ws_up, ws_down, g2, b2, extra=None, prompt_rows=None):
    rows = x.shape[0]
    n_tiles = rows // TOKEN_TILE
    n_sub = COMBINE_TILES_PER_STEP
    tm = TOKEN_TILE * n_sub
    n_steps = n_tiles // n_sub
    has_extra = extra is not None
    n_main, n_tail = ym.shape[1], yt.shape[1]
    tile_need = (jnp.max(fill, axis=0) >= n_main * SLOT_GROUP).astype(jnp.int32)
    step_need = jnp.max(tile_need.reshape(n_steps, n_sub), axis=1)
    steps = jnp.arange(n_steps, dtype=jnp.int32)
    last = jnp.maximum(lax.cummax(jnp.where(step_need > 0, steps, -1)), 0)

    in_specs = [pl.BlockSpec((tm, D_MODEL), lambda i, *_: (i, 0)),
                pl.BlockSpec((N_EXPERTS, tm), lambda i, *_: (0, i)),
                pl.BlockSpec((N_EXPERTS, n_main, n_sub, SLOT_GROUP, D_MODEL), lambda i, *_: (0, 0, i, 0, 0)),
                pl.BlockSpec((N_EXPERTS, n_tail, n_sub, SLOT_GROUP, D_MODEL),
                             lambda i, tn, la, nd: (0, 0, la[i], 0, 0)),
                pl.BlockSpec((1, D_MODEL, EXPERT_FF), lambda i, *_: (layer, 0, 0)),
                pl.BlockSpec((1, D_MODEL, EXPERT_FF), lambda i, *_: (layer, 0, 0)),
                pl.BlockSpec((1, EXPERT_FF, D_MODEL), lambda i, *_: (layer, 0, 0)),
                pl.BlockSpec((1, D_MODEL), lambda i, *_: (0, 0)), pl.BlockSpec((1, D_MODEL), lambda i, *_: (0, 0))]
    args = [x, sel, ym, yt, ws_gate, ws_up, ws_down, g2, b2]
    if has_extra:
        in_specs.append(pl.BlockSpec((tm, D_MODEL), lambda i, *_: (i, 0)))
        args.append(extra)
    if prompt_rows is None:
        n_prompt_tiles = None
        out_specs = pl.BlockSpec((tm, D_MODEL), lambda i, *_: (i, 0))
        out_shape = jax.ShapeDtypeStruct((rows, D_MODEL), jnp.float32)
    else:
        n_prompt_tiles = prompt_rows // tm
        out_specs = [pl.BlockSpec((tm, D_MODEL), lambda i, *_: (jnp.minimum(i, n_prompt_tiles - 1), 0)),
                     pl.BlockSpec((tm, D_MODEL), lambda i, *_: (jnp.maximum(i - n_prompt_tiles, 0), 0))]
        out_shape = [jax.ShapeDtypeStruct((prompt_rows, D_MODEL), jnp.float32),
                     jax.ShapeDtypeStruct((rows - prompt_rows, D_MODEL), jnp.float32)]
    return pl.pallas_call(
        functools.partial(_combine_body, has_extra, n_prompt_tiles, n_tiles // need.shape[1]),
        grid_spec=pltpu.PrefetchScalarGridSpec(
            num_scalar_prefetch=3,
            grid=(n_steps,),
            in_specs=in_specs,
            out_specs=out_specs,
            scratch_shapes=[pltpu.VMEM((D_MODEL, EXPERT_FF), jnp.bfloat16),
                            pltpu.VMEM((D_MODEL, EXPERT_FF), jnp.bfloat16),
                            pltpu.VMEM((EXPERT_FF, D_MODEL), jnp.bfloat16),
                            pltpu.VMEM((tm, D_MODEL), jnp.float32)],
        ),
        out_shape=out_shape,
        compiler_params=pltpu.CompilerParams(dimension_semantics=("arbitrary",),
                                             vmem_limit_bytes=_vmem_limit(56 * 1024 * 1024)),
        name="combine_extra" if has_extra else "combine",
    )(tile_need, last, need.reshape(-1), *args)


def _dense_body(x_ref, gate_ref, wg_ref, wu_ref, wd_ref, y_ref, xb_ref):
    e = pl.program_id(1)

    @pl.when(e == 0)
    def _():
        xb_ref[...] = _bf(x_ref[...])
        y_ref[...] = jnp.zeros(y_ref.shape, jnp.float32)

    xb = xb_ref[...]
    h = _silu(_dot(xb, _bf(wg_ref[0, 0]))) * _dot(xb, _bf(wu_ref[0, 0]))
    down = _dot(_bf(h), _bf(wd_ref[0, 0]))
    lane = lax.broadcasted_iota(jnp.int32, gate_ref.shape, 1)
    gate = jnp.sum(jnp.where(lane == e, gate_ref[...], 0.0), axis=-1, keepdims=True)
    y_ref[...] += down * gate


def _dense_experts(layer, x, gates, w_gate, w_up, w_down):
    rows = x.shape[0]
    tm = DENSE_TILE
    w_in_spec = pl.BlockSpec((1, 1, D_MODEL, EXPERT_FF), lambda i, e: (layer, e, 0, 0))
    return pl.pallas_call(
        _dense_body,
        grid=(rows // tm, N_EXPERTS),
        in_specs=[pl.BlockSpec((tm, D_MODEL), lambda i, e: (i, 0)),
                  pl.BlockSpec((tm, GATE_LANES), lambda i, e: (i, 0)),
                  w_in_spec, w_in_spec,
                  pl.BlockSpec((1, 1, EXPERT_FF, D_MODEL), lambda i, e: (layer, e, 0, 0))],
        out_specs=pl.BlockSpec((tm, D_MODEL), lambda i, e: (i, 0)),
        out_shape=jax.ShapeDtypeStruct((rows, D_MODEL), jnp.float32),
        scratch_shapes=[pltpu.VMEM((tm, D_MODEL), jnp.bfloat16)],
        compiler_params=pltpu.CompilerParams(dimension_semantics=("arbitrary", "arbitrary"),
                                             vmem_limit_bytes=_vmem_limit(40 * 1024 * 1024)),
        name="dense_overflow",
    )(x, gates, w_gate, w_up, w_down)


def _channel_sublayer(layer, x, w_router_t, bias_col, w_gate, w_up, w_down, ws_gate, ws_up, ws_down, g2, b2,
                      prompt_rows=None):
    xs, sel, gate, over, flags, fill = _route_dispatch(x, w_router_t, bias_col)
    fill = fill[:, :, :ROUTE_TILES_PER_STEP].transpose(1, 0, 2).reshape(N_EXPERTS, -1)
    ym, yt, need = _experts(layer, xs, sel, gate, fill, w_gate, w_up, w_down)
    rest = (x, sel, fill, ym, yt, need, ws_gate, ws_up, ws_down, g2, b2)

    def with_overflow():
        extra = _dense_experts(layer, x, over, w_gate, w_up, w_down)
        return _combine(layer, *rest, extra=extra, prompt_rows=prompt_rows)

    def without_overflow():
        return _combine(layer, *rest, prompt_rows=prompt_rows)

    return lax.cond(jnp.max(flags) > 0.0, with_overflow, without_overflow)


def kernel(x_prompt, x_sample, mem_prompt, cache_mem_k, cache_mem_v, state_conv_a, state_conv_b, w_in_a, conv_a_w, conv_a_b, norm_a_g, norm_a_b, w_in_b, conv_b_w, w_kv, w_out, ln1_g, ln1_b, w_router, router_bias, w_gate, w_up, w_down, ws_gate, ws_up, ws_down, ln2_g, ln2_b):
    batch, seq, d = x_prompt.shape
    n_seq, n_pos, _ = x_sample.shape
    c = MIX_WIDTH
    p_rows, s_rows = batch * seq, n_pos * n_seq
    s_block = p_rows // s_rows
    row = lambda a: a.reshape(1, -1)

    x_p = x_prompt.reshape(p_rows, d)
    x_s, x_s_block = x_sample.transpose(1, 0, 2).reshape(s_rows, d), 0
    k_all, v_all = _kv_projection(mem_prompt.reshape(batch * N_MEM, d), w_kv)
    k_p = k_all.reshape(DEPTH, batch, N_MEM, XATTN_WIDTH)
    v_p = v_all.reshape(DEPTH, batch, N_MEM, XATTN_WIDTH)

    conv_a_p, conv_b_p, conv_a_s, conv_b_s = [], [], [], []
    for i in range(DEPTH):
        j = i // N_MIXERS
        is_a = i % N_MIXERS == 0
        if is_a:
            w_in, cw = _bf(w_in_a[j]), conv_a_w[j]
            cb, ng, nb = row(conv_a_b[j]), row(norm_a_g[j]), row(norm_a_b[j])
            hist_s = state_conv_a[j]
        else:
            w_in, cw = _bf(w_in_b[j]), conv_b_w[j]
            cb = ng = nb = jnp.zeros((1, c), jnp.float32)
            hist_s = state_conv_b[j]
        w_o = _bf(w_out[i])
        g1, b1 = row(ln1_g[i]), row(ln1_b[i])

        mix, q, hist_s_new = _sample_mix(is_a, n_seq, s_rows, x_s, x_s_block, w_in, hist_s.transpose(1, 0, 2),
                                         cw, cb, ng, nb)
        attn = _sample_attention(i, n_seq, q, cache_mem_k, cache_mem_v)
        h_s = _sample_out(s_rows, x_s, x_s_block, mix, attn, w_o, g1, b1)
        h, hist_p_new = _prompt_token_sublayer(is_a, i, x_p, h_s, batch, seq, w_in, cw, cb, ng, nb,
                                               k_p, v_p, w_o, g1, b1)
        hist_s_new = hist_s_new.transpose(1, 0, 2)
        if is_a:
            conv_a_p.append(hist_p_new)
            conv_a_s.append(hist_s_new)
        else:
            conv_b_p.append(hist_p_new)
            conv_b_s.append(hist_s_new)

        last = i == DEPTH - 1
        h = _channel_sublayer(i, h, _bf(w_router[i].T), router_bias[i].reshape(N_EXPERTS, 1),
                              w_gate, w_up, w_down, ws_gate, ws_up, ws_down, row(ln2_g[i]), row(ln2_b[i]),
                              prompt_rows=p_rows if last else None)
        if not last:
            x_p = h
            x_s, x_s_block = h, s_block

    y_p, y_s = h
    new_k = k_all.reshape(DEPTH, batch, N_MEM, N_XHEADS, XHEAD_DIM)
    new_v = v_all.reshape(DEPTH, batch, N_MEM, N_XHEADS, XHEAD_DIM)
    return (y_p.reshape(batch, seq, d), y_s.reshape(n_pos, n_seq, d).transpose(1, 0, 2), new_k, new_v,
            jnp.stack(conv_a_p), jnp.stack(conv_b_p), jnp.stack(conv_a_s), jnp.stack(conv_b_s))
```

```python
import functools

import jax
import jax.numpy as jnp
from jax import lax
from jax.experimental import pallas as pl
from jax.experimental.pallas import tpu as pltpu

D_MODEL = 1024
DEPTH = 2
N_MIXERS = 2
MIX_WIDTH = D_MODEL // 2
N_MEM = 256
N_XHEADS = 4
XHEAD_DIM = MIX_WIDTH // N_XHEADS
XATTN_WIDTH = N_XHEADS * XHEAD_DIM
CONV_A_WIDTH = 31
CONV_B_WIDTH = 3
N_EXPERTS = 64
TOP_K = 8
N_GROUPS = 8
GROUP_SIZE = N_EXPERTS // N_GROUPS
TOPK_GROUPS = 4
EXPERT_FF = D_MODEL // 4
ROUTED_SCALE = 2.5
LN_EPS = 1e-5
DEEPNORM_ALPHA = (2 * DEPTH) ** 0.25

V7X_LANES = 128
V7X_SUBLANES = 8
V7X_VMEM_BYTES = 64 * 1024 * 1024

HIST_PAD = 32
PROMPT_SEQ_TILE = 512
PROMPT_ROW_GROUPS = 1
SAMPLE_BATCH_BLOCK = 8
TOKEN_TILE = 256
SLOT_WINDOW = 64
SLOT_GROUP = 16
SLOT_GROUPS_ALWAYS = 3
EXPERT_CHUNK = 8
ROUTE_TILES_PER_STEP = 2
COMBINE_TILES_PER_STEP = 2
COMBINE_CHUNK = 16
EXPERT_ROW_CHUNKS = 1
GATE_LANES = V7X_LANES
DENSE_TILE = 512
NEG_INF = float("-inf")


def _vmem_limit(nbytes):
    return int(min(max(nbytes, 16 * 1024 * 1024), V7X_VMEM_BYTES - 8 * 1024 * 1024))


def _bf(x):
    return x.astype(jnp.bfloat16)


def _dot(a, b):
    return jnp.dot(a, b, preferred_element_type=jnp.float32)


def _dot_nt(a, b):
    return lax.dot_general(a, b, (((1,), (1,)), ((), ())), preferred_element_type=jnp.float32)


def _dot_tn(a, b):
    return lax.dot_general(a, b, (((0,), (0,)), ((), ())), preferred_element_type=jnp.float32)


def _sigmoid(x):
    return 1.0 / (1.0 + jnp.exp(-x))


def _silu(x):
    return x * _sigmoid(x)


def _layer_norm(x, g, b):
    mu = jnp.mean(x, axis=-1, keepdims=True)
    xc = x - mu
    var = jnp.mean(xc * xc, axis=-1, keepdims=True)
    return xc * lax.rsqrt(var + LN_EPS) * g + b


def _memory_attention(q, k_head, v_head):
    outs = []
    for h in range(N_XHEADS):
        sl = slice(h * XHEAD_DIM, (h + 1) * XHEAD_DIM)
        s = _dot_nt(_bf(q[:, sl]), _bf(k_head(h))) * (XHEAD_DIM ** -0.5)
        e = jnp.exp(s - jnp.max(s, axis=-1, keepdims=True))
        p = e / jnp.sum(e, axis=-1, keepdims=True)
        outs.append(_dot(_bf(p), _bf(v_head(h))))
    return jnp.concatenate(outs, axis=-1)


def _full(shape):
    return pl.BlockSpec(shape, lambda *_: tuple(0 for _ in shape))


def _kv_body(mem_ref, w_ref, k_ref, v_ref):
    kv = _dot(_bf(mem_ref[...]), _bf(w_ref[0]))
    k_ref[0] = kv[:, :XATTN_WIDTH]
    v_ref[0] = kv[:, XATTN_WIDTH:]


def _kv_projection(mem2d, w_kv):
    rows = mem2d.shape[0]
    tm = 512
    out = jax.ShapeDtypeStruct((DEPTH, rows, XATTN_WIDTH), jnp.float32)
    return pl.pallas_call(
        _kv_body,
        grid=(DEPTH, rows // tm),
        in_specs=[pl.BlockSpec((tm, D_MODEL), lambda i, m: (m, 0)),
                  pl.BlockSpec((1, D_MODEL, 2 * XATTN_WIDTH), lambda i, m: (i, 0, 0))],
        out_specs=[pl.BlockSpec((1, tm, XATTN_WIDTH), lambda i, m: (i, m, 0)),
                   pl.BlockSpec((1, tm, XATTN_WIDTH), lambda i, m: (i, m, 0))],
        out_shape=[out, out],
        compiler_params=pltpu.CompilerParams(dimension_semantics=("arbitrary", "arbitrary"),
                                             vmem_limit_bytes=_vmem_limit(32 * 1024 * 1024)),
        name="kv_projection",
    )(mem2d, w_kv)


def _prompt_token_body(is_a, nl, n_tiles, x_ref, w_in_ref, cw_ref, cb_ref, ng_ref, nb_ref, k_ref, v_ref, w_out_ref,
                       g1_ref, b1_ref, sample_ref, y_ref, hist_ref, buf_ref):
    c = MIX_WIDTH
    tl = x_ref.shape[0]
    width = CONV_A_WIDTH if is_a else CONV_B_WIDTH
    step = pl.program_id(0)
    seq_step = step % nl

    @pl.when(step == n_tiles)
    def _():
        y_ref[...] = sample_ref[...]

    @pl.when((step < n_tiles) & (seq_step == 0))
    def _():
        buf_ref[pl.ds(0, HIST_PAD), :] = jnp.zeros((HIST_PAD, c), jnp.float32)

    @pl.when(step < n_tiles)
    def _():
        rg = tl // PROMPT_ROW_GROUPS
        head = lambda ref: lambda h: ref[0, 0, :, h * XHEAD_DIM:(h + 1) * XHEAD_DIM]

        def group_stages(rows):
            x = x_ref[rows, :]
            u = _dot(_bf(x), w_in_ref[...])
            conv_in = u[:, :c] * _sigmoid(u[:, c:2 * c]) if is_a else u[:, c:2 * c] * u[:, 2 * c:3 * c]
            buf_ref[pl.ds(HIST_PAD + rows.start, rg), :] = conv_in
            yield

            base = HIST_PAD - (width - 1) + rows.start
            conv = None
            for phase in range(V7X_SUBLANES):
                taps = [t for t in range(width) if (base + t) % V7X_SUBLANES == phase]
                if not taps:
                    continue
                n = rg if phase == 0 else rg + V7X_SUBLANES
                part = None
                for t in taps:
                    term = cw_ref[t:t + 1, :] * buf_ref[pl.ds(base + t - phase, n), :]
                    part = term if part is None else part + term
                part = part[phase:phase + rg, :]
                conv = part if conv is None else conv + part

            if is_a:
                mix = _silu(_layer_norm(conv + cb_ref[...], ng_ref[...], nb_ref[...]))
                q = u[:, 2 * c:]
            else:
                mix = u[:, :c] * conv
                q = u[:, 3 * c:]
            yield
            attn = _memory_attention(q, head(k_ref), head(v_ref))
            yield
            out = _dot(_bf(jnp.concatenate([mix, attn], axis=-1)), w_out_ref[...])
            y_ref[rows, :] = _layer_norm(DEEPNORM_ALPHA * x + out, g1_ref[...], b1_ref[...])
            yield

        gens = [group_stages(slice(g * rg, (g + 1) * rg)) for g in range(PROMPT_ROW_GROUPS)]
        for gen in gens:
            next(gen)
        later_stages = 3
        for tick in range(later_stages + PROMPT_ROW_GROUPS - 1):
            for g, gen in enumerate(gens):
                if 0 <= tick - g < later_stages:
                    next(gen)

    @pl.when((step < n_tiles) & (seq_step == nl - 1))
    def _():
        hist_ref[0] = buf_ref[pl.ds(HIST_PAD + tl - (width - 1), width - 1), :]

    @pl.when(step < n_tiles)
    def _():
        buf_ref[pl.ds(0, HIST_PAD), :] = buf_ref[pl.ds(tl, HIST_PAD), :]


def _prompt_token_sublayer(is_a, layer, x, sample_rows, batch, seq, w_in, cw, cb, ng, nb, k, v, w_out, g1, b1):
    tl = PROMPT_SEQ_TILE
    assert sample_rows.shape == (tl, D_MODEL)
    nl = seq // tl
    n_tiles = batch * nl
    c = MIX_WIDTH
    width = CONV_A_WIDTH if is_a else CONV_B_WIDTH
    n_in = w_in.shape[1]
    tile = lambda s: jnp.minimum(s, n_tiles - 1)
    mem_spec = pl.BlockSpec((1, 1, N_MEM, XATTN_WIDTH), lambda s: (layer, tile(s) // nl, 0, 0))
    return pl.pallas_call(
        functools.partial(_prompt_token_body, is_a, nl, n_tiles),
        grid=(n_tiles + 1,),
        in_specs=[pl.BlockSpec((tl, D_MODEL), lambda s: (tile(s), 0)),
                  _full((D_MODEL, n_in)), _full((width, c)), _full((1, c)), _full((1, c)), _full((1, c)),
                  mem_spec, mem_spec,
                  _full((c + XATTN_WIDTH, D_MODEL)), _full((1, D_MODEL)), _full((1, D_MODEL)),
                  _full((tl, D_MODEL))],
        out_specs=[pl.BlockSpec((tl, D_MODEL), lambda s: (s, 0)),
                   pl.BlockSpec((1, width - 1, c), lambda s: (tile(s) // nl, 0, 0))],
        out_shape=[jax.ShapeDtypeStruct(((n_tiles + 1) * tl, D_MODEL), jnp.float32),
                   jax.ShapeDtypeStruct((batch, width - 1, c), jnp.float32)],
        scratch_shapes=[pltpu.VMEM((HIST_PAD + tl, c), jnp.float32)],
        compiler_params=pltpu.CompilerParams(dimension_semantics=("arbitrary",),
                                             vmem_limit_bytes=_vmem_limit(48 * 1024 * 1024)),
        name="prompt_token_a" if is_a else "prompt_token_b",
    )(x, w_in, cw, cb, ng, nb, k, v, w_out, g1, b1, sample_rows)


def _sample_mix_body(is_a, n_seq, x_ref, w_in_ref, hist_ref, cw_ref, cb_ref, ng_ref, nb_ref,
                     mix_ref, q_ref, new_hist_ref):
    c = MIX_WIDTH
    width = CONV_A_WIDTH if is_a else CONV_B_WIDTH
    n_hist = width - 1
    n_pos = x_ref.shape[0] // n_seq
    u = _dot(_bf(x_ref[...]), w_in_ref[...])
    if is_a:
        conv_in = u[:, :c] * _sigmoid(u[:, c:2 * c])
        q_ref[...] = u[:, 2 * c:]
    else:
        conv_in = u[:, c:2 * c] * u[:, 2 * c:3 * c]
        q_ref[...] = u[:, 3 * c:]

    def full_row(j):
        if j < n_hist:
            return hist_ref[j]
        return conv_in[(j - n_hist) * n_seq:(j - n_hist + 1) * n_seq, :]

    for l in range(n_pos):
        conv = cw_ref[0:1, :] * full_row(l)
        for t in range(1, width):
            conv = conv + cw_ref[t:t + 1, :] * full_row(l + t)
        rows = slice(l * n_seq, (l + 1) * n_seq)
        if is_a:
            mix_ref[rows, :] = _silu(_layer_norm(conv + cb_ref[...], ng_ref[...], nb_ref[...]))
        else:
            mix_ref[rows, :] = u[rows, :c] * conv
    for j in range(n_hist):
        new_hist_ref[j] = full_row(j + n_pos)


def _sample_mix(is_a, n_seq, rows, x, x_block, w_in, hist, cw, cb, ng, nb):
    c = MIX_WIDTH
    small = (w_in, hist, cw, cb, ng, nb)
    return pl.pallas_call(
        functools.partial(_sample_mix_body, is_a, n_seq),
        grid=(1,),
        in_specs=[pl.BlockSpec((rows, D_MODEL), lambda i: (x_block, 0))] + [_full(a.shape) for a in small],
        out_specs=[_full((rows, c)), _full((rows, XATTN_WIDTH)), _full(hist.shape)],
        out_shape=[jax.ShapeDtypeStruct((rows, c), jnp.float32),
                   jax.ShapeDtypeStruct((rows, XATTN_WIDTH), jnp.float32),
                   jax.ShapeDtypeStruct(hist.shape, jnp.float32)],
        compiler_params=pltpu.CompilerParams(dimension_semantics=("arbitrary",),
                                             vmem_limit_bytes=_vmem_limit(48 * 1024 * 1024)),
        name="sample_mix_a" if is_a else "sample_mix_b",
    )(x, *small)


def _sample_attn_body(n_seq, q_ref, k_ref, v_ref, o_ref, bias_ref):
    bb = k_ref.shape[1]
    n_pos = q_ref.shape[0] // n_seq
    first = pl.multiple_of(pl.program_id(0) * bb, bb)

    @pl.when(pl.program_id(0) == 0)
    def _():
        r = lax.broadcasted_iota(jnp.int32, bias_ref.shape, 0)
        col = lax.broadcasted_iota(jnp.int32, bias_ref.shape, 1)
        valid = ((r % N_XHEADS) == (col // (n_pos * bb))) & ((r // (N_MEM * N_XHEADS)) == (col % bb))
        bias_ref[...] = jnp.where(valid, 0.0, NEG_INF)

    q = jnp.concatenate([q_ref[pl.ds(l * n_seq + first, bb), :] for l in range(n_pos)], axis=0)
    nq = n_pos * bb
    n_rows = bb * N_MEM * N_XHEADS
    k_rows = k_ref[0].reshape(n_rows, XHEAD_DIM)
    v_rows = v_ref[0].reshape(n_rows, XHEAD_DIM)
    q_heads = jnp.concatenate([q[:, h * XHEAD_DIM:(h + 1) * XHEAD_DIM] for h in range(N_XHEADS)], axis=0)
    s = _dot_nt(_bf(k_rows), _bf(q_heads)) * (XHEAD_DIM ** -0.5) + bias_ref[...]
    e = jnp.exp(s - jnp.max(s, axis=0, keepdims=True))
    p = e / jnp.sum(e, axis=0, keepdims=True)
    o_heads = _dot_tn(_bf(p), _bf(v_rows))
    o = jnp.concatenate([o_heads[h * nq:(h + 1) * nq] for h in range(N_XHEADS)], axis=1)
    for l in range(n_pos):
        o_ref[pl.ds(l * n_seq + first, bb), :] = o[l * bb:(l + 1) * bb, :]


def _sample_attention(layer, n_seq, q, mem_k, mem_v):
    rows = q.shape[0]
    bb = SAMPLE_BATCH_BLOCK
    mem_spec = pl.BlockSpec((1, bb, N_MEM, N_XHEADS, XHEAD_DIM), lambda i: (layer, i, 0, 0, 0))
    return pl.pallas_call(
        functools.partial(_sample_attn_body, n_seq),
        grid=(n_seq // bb,),
        in_specs=[pl.BlockSpec((rows, XATTN_WIDTH), lambda i: (0, 0)), mem_spec, mem_spec],
        out_specs=pl.BlockSpec((rows, XATTN_WIDTH), lambda i: (0, 0)),
        out_shape=jax.ShapeDtypeStruct((rows, XATTN_WIDTH), jnp.float32),
        scratch_shapes=[pltpu.VMEM((bb * N_MEM * N_XHEADS, N_XHEADS * (rows // n_seq) * bb), jnp.float32)],
        compiler_params=pltpu.CompilerParams(dimension_semantics=("arbitrary",),
                                             vmem_limit_bytes=_vmem_limit(40 * 1024 * 1024)),
        name="sample_attention",
    )(q, mem_k, mem_v)


def _sample_out_body(x_ref, mix_ref, attn_ref, w_out_ref, g1_ref, b1_ref, y_ref):
    cat = jnp.concatenate([mix_ref[...], attn_ref[...]], axis=-1)
    out = _dot(_bf(cat), w_out_ref[...])
    y_ref[...] = _layer_norm(DEEPNORM_ALPHA * x_ref[...] + out, g1_ref[...], b1_ref[...])


def _sample_out(rows, x, x_block, mix, attn, w_out, g1, b1):
    small = (mix, attn, w_out, g1, b1)
    return pl.pallas_call(
        _sample_out_body,
        grid=(1,),
        in_specs=[pl.BlockSpec((rows, D_MODEL), lambda i: (x_block, 0))] + [_full(a.shape) for a in small],
        out_specs=_full((rows, D_MODEL)),
        out_shape=jax.ShapeDtypeStruct((rows, D_MODEL), jnp.float32),
        compiler_params=pltpu.CompilerParams(dimension_semantics=("arbitrary",),
                                             vmem_limit_bytes=_vmem_limit(32 * 1024 * 1024)),
        name="sample_out",
    )(x, *small)


def _first_index_of(mask, index, n):
    cand = jnp.where(mask, index, float(n))
    while cand.ndim > 2:
        cand = jnp.min(cand, axis=0)
    return jnp.min(cand, axis=0, keepdims=True)


def _max_all(x):
    while x.ndim > 2:
        x = jnp.max(x, axis=0)
    return jnp.max(x, axis=0, keepdims=True)


def _routing_stages(xb, wr_ref, bias_ref, out):
    tm = xb.shape[0]
    scores = _sigmoid(_dot_nt(wr_ref[...], xb))
    biased = scores + bias_ref[...]
    grp = biased.reshape(N_GROUPS, GROUP_SIZE, tm)

    within = lax.broadcasted_iota(jnp.int32, grp.shape, 1).astype(jnp.float32)
    top1 = jnp.max(grp, axis=1, keepdims=True)
    first = jnp.min(jnp.where(grp == top1, within, float(GROUP_SIZE)), axis=1, keepdims=True)
    top2 = jnp.max(jnp.where(within == first, NEG_INF, grp), axis=1, keepdims=True)
    grp_score = (top1 + top2).reshape(N_GROUPS, tm)
    yield

    gidx = lax.broadcasted_iota(jnp.int32, grp_score.shape, 0).astype(jnp.float32)
    grp_sel = jnp.zeros(grp_score.shape, jnp.float32)
    for _ in range(TOPK_GROUPS):
        best = jnp.max(grp_score, axis=0, keepdims=True)
        pick = gidx == _first_index_of(grp_score == best, gidx, N_GROUPS)
        grp_sel = jnp.where(pick, 1.0, grp_sel)
        grp_score = jnp.where(pick, NEG_INF, grp_score)
    yield

    eidx = (lax.broadcasted_iota(jnp.int32, grp.shape, 0) * GROUP_SIZE
            + lax.broadcasted_iota(jnp.int32, grp.shape, 1)).astype(jnp.float32)
    cand = jnp.where(grp_sel.reshape(N_GROUPS, 1, tm) > 0.0, grp, NEG_INF)
    chosen = jnp.zeros(grp.shape, jnp.float32)
    for k in range(TOP_K):
        best = _max_all(cand).reshape(1, 1, tm)
        pick = eidx == _first_index_of(cand == best, eidx, N_EXPERTS).reshape(1, 1, tm)
        chosen = jnp.where(pick, 1.0, chosen)
        cand = jnp.where(pick, NEG_INF, cand)
        if k % 2 == 1:
            yield

    w = jnp.where(chosen > 0.0, scores.reshape(grp.shape), 0.0)
    total = jnp.sum(jnp.sum(w, axis=0), axis=0, keepdims=True).reshape(1, 1, tm)
    gates = (w / total * ROUTED_SCALE).reshape(N_EXPERTS, tm)
    chosen = chosen.reshape(N_EXPERTS, tm)
    yield

    earlier = (lax.broadcasted_iota(jnp.int32, (tm, tm), 0) < lax.broadcasted_iota(jnp.int32, (tm, tm), 1))
    pos = _dot(_bf(chosen), jnp.where(earlier, 1.0, 0.0).astype(jnp.bfloat16))
    routed = chosen > 0.0
    in_window = routed & (pos < float(SLOT_WINDOW))
    out["sel"] = jnp.where(in_window, pos, -1.0)
    out["gate"] = jnp.where(in_window, gates, 0.0)
    out["over"] = jnp.where(routed & (pos >= float(SLOT_WINDOW)), gates, 0.0)
    yield


def _slot_matches(sel_row, first_slot=0, n_slots=None):
    tm = sel_row.shape[1]
    n_slots = SLOT_WINDOW if n_slots is None else n_slots
    slot = (lax.broadcasted_iota(jnp.int32, (n_slots, tm), 0) + first_slot).astype(jnp.float32)
    return jnp.broadcast_to(sel_row, (n_slots, tm)) == slot


def _slot_onehot(sel, first_expert):
    rows = [_slot_matches(sel[e:e + 1, :]) for e in range(first_expert, first_expert + EXPERT_CHUNK)]
    return jnp.where(jnp.concatenate(rows, axis=0), 1.0, 0.0).astype(jnp.bfloat16)


def _route_body(x_ref, wr_ref, bias_ref, xs_ref, sel_ref, gate_ref, over_ref, flag_ref, fill_ref, xb_prev, sel_prev):
    tm = TOKEN_TILE
    n_sub = x_ref.shape[0] // tm
    n_groups = SLOT_WINDOW // SLOT_GROUP

    @pl.when(pl.program_id(0) == 0)
    def _():
        xb_prev[...] = jnp.zeros(xb_prev.shape, jnp.bfloat16)
        sel_prev[...] = jnp.full(sel_prev.shape, -1.0, jnp.float32)

    lane = lax.broadcasted_iota(jnp.int32, (N_EXPERTS, V7X_LANES), 1)
    fill = jnp.full((N_EXPERTS, V7X_LANES), -1.0, jnp.float32)
    any_over = None
    for t in range(n_sub):
        cols = slice(t * tm, (t + 1) * tm)
        xb_old, sel_old = xb_prev[cols, :], sel_prev[:, cols]
        xb = _bf(x_ref[cols, :])
        out = {}
        stages = _routing_stages(xb, wr_ref, bias_ref, out)
        onehot = _slot_onehot(sel_old, 0)
        for first in range(0, N_EXPERTS, EXPERT_CHUNK):
            later = _slot_onehot(sel_old, first + EXPERT_CHUNK) if first + EXPERT_CHUNK < N_EXPERTS else None
            next(stages, None)
            slots = _bf(_dot(onehot, xb_old))
            xs_ref[first:first + EXPERT_CHUNK, :, t] = slots.reshape(EXPERT_CHUNK, n_groups, SLOT_GROUP, D_MODEL)
            onehot = later
        for _ in stages:
            pass

        sel, over = out["sel"], out["over"]
        sel_ref[:, cols] = sel
        gate_ref[:, cols] = out["gate"]
        pad = jnp.zeros((GATE_LANES - N_EXPERTS, tm), jnp.float32)
        over_ref[cols, :] = jnp.concatenate([over, pad], axis=0).T
        tile_over = jnp.max(jnp.max(over, axis=0, keepdims=True), axis=1, keepdims=True)
        any_over = tile_over if any_over is None else jnp.maximum(any_over, tile_over)
        fill = jnp.where(lane == t, jnp.max(sel, axis=1, keepdims=True), fill)
        xb_prev[cols, :] = xb
        sel_prev[:, cols] = sel
    flag_ref[...] = jnp.broadcast_to(any_over, flag_ref.shape[1:])[None]
    fill_ref[...] = fill[None]


def _route_dispatch(x, w_router_t, bias_col):
    rows = x.shape[0]
    n_sub = ROUTE_TILES_PER_STEP
    tm = TOKEN_TILE * n_sub
    n_tiles = rows // tm
    n_groups = SLOT_WINDOW // SLOT_GROUP
    routed = lambda i: jnp.minimum(i, n_tiles - 1)
    return pl.pallas_call(
        _route_body,
        grid=(n_tiles + 1,),
        in_specs=[pl.BlockSpec((tm, D_MODEL), lambda i: (routed(i), 0)),
                  _full((N_EXPERTS, D_MODEL)), _full((N_EXPERTS, 1))],
        out_specs=[pl.BlockSpec((N_EXPERTS, n_groups, n_sub, SLOT_GROUP, D_MODEL),
                                lambda i: (0, 0, jnp.maximum(i - 1, 0), 0, 0)),
                   pl.BlockSpec((N_EXPERTS, tm), lambda i: (0, routed(i))),
                   pl.BlockSpec((N_EXPERTS, tm), lambda i: (0, routed(i))),
                   pl.BlockSpec((tm, GATE_LANES), lambda i: (routed(i), 0)),
                   pl.BlockSpec((1, 8, V7X_LANES), lambda i: (routed(i), 0, 0)),
                   pl.BlockSpec((1, N_EXPERTS, V7X_LANES), lambda i: (routed(i), 0, 0))],
        out_shape=[jax.ShapeDtypeStruct((N_EXPERTS, n_groups, n_tiles * n_sub, SLOT_GROUP, D_MODEL), jnp.bfloat16),
                   jax.ShapeDtypeStruct((N_EXPERTS, rows), jnp.float32),
                   jax.ShapeDtypeStruct((N_EXPERTS, rows), jnp.float32),
                   jax.ShapeDtypeStruct((rows, GATE_LANES), jnp.float32),
                   jax.ShapeDtypeStruct((n_tiles, 8, V7X_LANES), jnp.float32),
                   jax.ShapeDtypeStruct((n_tiles, N_EXPERTS, V7X_LANES), jnp.float32)],
        scratch_shapes=[pltpu.VMEM((tm, D_MODEL), jnp.bfloat16), pltpu.VMEM((N_EXPERTS, tm), jnp.float32)],
        compiler_params=pltpu.CompilerParams(dimension_semantics=("arbitrary",),
                                             vmem_limit_bytes=_vmem_limit(56 * 1024 * 1024)),
        name="route_dispatch",
    )(x, w_router_t, bias_col)


def _expert_body(need_ref, last_e_ref, last_c_ref, xm_ref, xt_ref, sel_ref, gate_ref, wg_ref, wu_ref, wd_ref,
                 ym_ref, yt_ref, wg_bf, wu_bf, wd_bf):
    del last_e_ref, last_c_ref

    @pl.when(pl.program_id(1) == 0)
    def _():
        wg_bf[...] = _bf(wg_ref[0, 0])
        wu_bf[...] = _bf(wu_ref[0, 0])
        wd_bf[...] = _bf(wd_ref[0, 0])

    row = pl.ds(pl.program_id(0) % V7X_SUBLANES, 1)
    n_main = xm_ref.shape[1]
    n_tiles = xm_ref.shape[2]
    tm = sel_ref.shape[1] // n_tiles

    def run(x_ref, y_ref, first_group):
        n_g = x_ref.shape[1]
        rows = n_tiles * SLOT_GROUP
        proj = None
        for g in range(n_g + 1):
            nxt = None
            if g < n_g:
                x = x_ref[0, g].reshape(rows, D_MODEL)
                nxt = (_dot(x, wg_bf[...]), _dot(x, wu_bf[...]))
            if proj is not None:
                slot_gates = []
                for t in range(n_tiles):
                    cols = slice(t * tm, (t + 1) * tm)
                    match = _slot_matches(sel_ref[row, cols], (first_group + g - 1) * SLOT_GROUP, SLOT_GROUP)
                    slot_gates.append(jnp.sum(jnp.where(match, gate_ref[row, cols], 0.0), axis=-1, keepdims=True))
                gate = jnp.concatenate(slot_gates, axis=0)
                h = _silu(proj[0]) * proj[1]
                y = _bf(_dot(_bf(h), wd_bf[...]) * gate)
                y_ref[0, g - 1] = y.reshape(n_tiles, SLOT_GROUP, D_MODEL)
            proj = nxt

    run(xm_ref, ym_ref, 0)

    needed = need_ref[pl.program_id(0) * pl.num_programs(1) + pl.program_id(1)] > 0

    @pl.when(needed)
    def _():
        run(xt_ref, yt_ref, n_main)

    @pl.when(jnp.logical_not(needed))
    def _():
        yt_ref[...] = jnp.zeros(yt_ref.shape, jnp.bfloat16)


def _slot_group_split():
    n_groups = SLOT_WINDOW // SLOT_GROUP
    n_main = min(SLOT_GROUPS_ALWAYS, n_groups - 1)
    n_tail = n_groups - n_main
    assert n_main % n_tail == 0, "the trailing groups must form one block of the group axis"
    return n_main, n_tail


def _experts(layer, xs, sel, gate, fill, w_gate, w_up, w_down):
    n_e, _, n_tiles_all, _, _ = xs.shape
    n_chunks = EXPERT_ROW_CHUNKS
    n_tiles = n_tiles_all // n_chunks
    tokens = sel.shape[1] // n_chunks
    n_main, n_tail = _slot_group_split()

    need = (jnp.max(fill.reshape(n_e, n_chunks, n_tiles), axis=-1) >= n_main * SLOT_GROUP).astype(jnp.int32)
    steps = jnp.arange(n_e * n_chunks, dtype=jnp.int32)
    last = jnp.maximum(lax.cummax(jnp.where(need.reshape(-1) > 0, steps, -1)), 0)
    last_e, last_c = last // n_chunks, last % n_chunks

    w_in_spec = pl.BlockSpec((1, 1, D_MODEL, EXPERT_FF), lambda e, c, *_: (layer, e, 0, 0))
    route_spec = pl.BlockSpec((V7X_SUBLANES, tokens), lambda e, c, *_: (e // V7X_SUBLANES, c))
    main_spec = pl.BlockSpec((1, n_main, n_tiles, SLOT_GROUP, D_MODEL), lambda e, c, *_: (e, 0, c, 0, 0))

    def tail_map(group_block):
        return lambda e, c, nd, le, lc: (le[e * n_chunks + c], group_block, lc[e * n_chunks + c], 0, 0)

    tail_block = (1, n_tail, n_tiles, SLOT_GROUP, D_MODEL)
    ym, yt = pl.pallas_call(
        _expert_body,
        grid_spec=pltpu.PrefetchScalarGridSpec(
            num_scalar_prefetch=3,
            grid=(n_e, n_chunks),
            in_specs=[main_spec, pl.BlockSpec(tail_block, tail_map(n_main // n_tail)),
                      route_spec, route_spec, w_in_spec, w_in_spec,
                      pl.BlockSpec((1, 1, EXPERT_FF, D_MODEL), lambda e, c, *_: (layer, e, 0, 0))],
            out_specs=[main_spec, pl.BlockSpec(tail_block, lambda e, c, *_: (e, 0, c, 0, 0))],
            scratch_shapes=[pltpu.VMEM((D_MODEL, EXPERT_FF), jnp.bfloat16),
                            pltpu.VMEM((D_MODEL, EXPERT_FF), jnp.bfloat16),
                            pltpu.VMEM((EXPERT_FF, D_MODEL), jnp.bfloat16)],
        ),
        out_shape=[jax.ShapeDtypeStruct((n_e, n_main, n_tiles_all, SLOT_GROUP, D_MODEL), jnp.bfloat16),
                   jax.ShapeDtypeStruct((n_e, n_tail, n_tiles_all, SLOT_GROUP, D_MODEL), jnp.bfloat16)],
        compiler_params=pltpu.CompilerParams(dimension_semantics=("arbitrary", "arbitrary"),
                                             vmem_limit_bytes=_vmem_limit(56 * 1024 * 1024)),
        name="experts",
    )(need.reshape(-1), last_e, last_c, xs, xs, sel, gate, w_gate, w_up, w_down)
    return ym, yt


def _combine_body(has_extra, n_prompt_tiles, tile_need_ref, last_ref,
                  x_ref, sel_ref, ym_ref, yt_ref, wsg_ref, wsu_ref, wsd_ref, g2_ref, b2_ref, *rest):
    del last_ref
    rest = list(rest)
    extra_ref = rest.pop(0) if has_extra else None
    out_refs = [rest.pop(0)] if n_prompt_tiles is None else [rest.pop(0), rest.pop(0)]
    wsg_bf, wsu_bf, wsd_bf, acc_ref = rest
    step = pl.program_id(0)
    tm = TOKEN_TILE
    n_sub = x_ref.shape[0] // tm
    n_main, n_tail = ym_ref.shape[1], yt_ref.shape[1]

    @pl.when(step == 0)
    def _():
        wsg_bf[...] = _bf(wsg_ref[0])
        wsu_bf[...] = _bf(wsu_ref[0])
        wsd_bf[...] = _bf(wsd_ref[0])

    for t in range(n_sub):
        rows_t = slice(t * tm, (t + 1) * tm)
        tile = step * n_sub + t

        def onehot(first_expert, first_group, n_g):
            rows = [_slot_matches(sel_ref[e:e + 1, rows_t], first_group * SLOT_GROUP, n_g * SLOT_GROUP)
                    for e in range(first_expert, first_expert + COMBINE_CHUNK)]
            return jnp.where(jnp.concatenate(rows, axis=0), 1.0, 0.0).astype(jnp.bfloat16)

        xb = _bf(x_ref[rows_t, :])
        ahead = onehot(0, 0, n_main)
        y = _dot(_bf(_silu(_dot(xb, wsg_bf[...])) * _dot(xb, wsu_bf[...])), wsd_bf[...])
        for first in range(0, N_EXPERTS, COMBINE_CHUNK):
            current = ahead
            if first + COMBINE_CHUNK < N_EXPERTS:
                ahead = onehot(first + COMBINE_CHUNK, 0, n_main)
            ys = ym_ref[first:first + COMBINE_CHUNK, :, t].reshape(COMBINE_CHUNK * n_main * SLOT_GROUP, D_MODEL)
            y = y + _dot_tn(current, ys)
        if has_extra:
            y = y + extra_ref[rows_t, :]
        acc_ref[rows_t, :] = y

        @pl.when(tile_need_ref[tile] > 0)
        def _():
            part = jnp.zeros((tm, D_MODEL), jnp.float32)
            for first in range(0, N_EXPERTS, COMBINE_CHUNK):
                ye = yt_ref[first:first + COMBINE_CHUNK, :, t].reshape(COMBINE_CHUNK * n_tail * SLOT_GROUP, D_MODEL)
                part = part + _dot_tn(onehot(first, n_main, n_tail), ye)
            acc_ref[rows_t, :] += part

    y = _layer_norm(DEEPNORM_ALPHA * x_ref[...] + acc_ref[...], g2_ref[...], b2_ref[...])

    if n_prompt_tiles is None:
        out_refs[0][...] = y
    else:
        @pl.when(step < n_prompt_tiles)
        def _():
            out_refs[0][...] = y

        @pl.when(step >= n_prompt_tiles)
        def _():
            out_refs[1][...] = y


def _combine(layer, x, sel, fill, ym, yt, ws_gate, ws_up, ws_down, g2, b2, extra=None, prompt_rows=None):
    rows = x.shape[0]
    n_tiles = rows // TOKEN_TILE
    n_sub = COMBINE_TILES_PER_STEP
    tm = TOKEN_TILE * n_sub
    n_steps = n_tiles // n_sub
    has_extra = extra is not None
    n_main, n_tail = ym.shape[1], yt.shape[1]
    tile_need = (jnp.max(fill, axis=0) >= n_main * SLOT_GROUP).astype(jnp.int32)
    step_need = jnp.max(tile_need.reshape(n_steps, n_sub), axis=1)
    steps = jnp.arange(n_steps, dtype=jnp.int32)
    last = jnp.maximum(lax.cummax(jnp.where(step_need > 0, steps, -1)), 0)

    in_specs = [pl.BlockSpec((tm, D_MODEL), lambda i, *_: (i, 0)),
                pl.BlockSpec((N_EXPERTS, tm), lambda i, *_: (0, i)),
                pl.BlockSpec((N_EXPERTS, n_main, n_sub, SLOT_GROUP, D_MODEL), lambda i, *_: (0, 0, i, 0, 0)),
                pl.BlockSpec((N_EXPERTS, n_tail, n_sub, SLOT_GROUP, D_MODEL),
                             lambda i, tn, la: (0, 0, la[i], 0, 0)),
                pl.BlockSpec((1, D_MODEL, EXPERT_FF), lambda i, *_: (layer, 0, 0)),
                pl.BlockSpec((1, D_MODEL, EXPERT_FF), lambda i, *_: (layer, 0, 0)),
                pl.BlockSpec((1, EXPERT_FF, D_MODEL), lambda i, *_: (layer, 0, 0)),
                pl.BlockSpec((1, D_MODEL), lambda i, *_: (0, 0)), pl.BlockSpec((1, D_MODEL), lambda i, *_: (0, 0))]
    args = [x, sel, ym, yt, ws_gate, ws_up, ws_down, g2, b2]
    if has_extra:
        in_specs.append(pl.BlockSpec((tm, D_MODEL), lambda i, *_: (i, 0)))
        args.append(extra)
    if prompt_rows is None:
        n_prompt_tiles = None
        out_specs = pl.BlockSpec((tm, D_MODEL), lambda i, *_: (i, 0))
        out_shape = jax.ShapeDtypeStruct((rows, D_MODEL), jnp.float32)
    else:
        n_prompt_tiles = prompt_rows // tm
        out_specs = [pl.BlockSpec((tm, D_MODEL), lambda i, *_: (jnp.minimum(i, n_prompt_tiles - 1), 0)),
                     pl.BlockSpec((tm, D_MODEL), lambda i, *_: (jnp.maximum(i - n_prompt_tiles, 0), 0))]
        out_shape = [jax.ShapeDtypeStruct((prompt_rows, D_MODEL), jnp.float32),
                     jax.ShapeDtypeStruct((rows - prompt_rows, D_MODEL), jnp.float32)]
    return pl.pallas_call(
        functools.partial(_combine_body, has_extra, n_prompt_tiles),
        grid_spec=pltpu.PrefetchScalarGridSpec(
            num_scalar_prefetch=2,
            grid=(n_steps,),
            in_specs=in_specs,
            out_specs=out_specs,
            scratch_shapes=[pltpu.VMEM((D_MODEL, EXPERT_FF), jnp.bfloat16),
                            pltpu.VMEM((D_MODEL, EXPERT_FF), jnp.bfloat16),
                            pltpu.VMEM((EXPERT_FF, D_MODEL), jnp.bfloat16),
                            pltpu.VMEM((tm, D_MODEL), jnp.float32)],
        ),
        out_shape=out_shape,
        compiler_params=pltpu.CompilerParams(dimension_semantics=("arbitrary",),
                                             vmem_limit_bytes=_vmem_limit(56 * 1024 * 1024)),
        name="combine_extra" if has_extra else "combine",
    )(tile_need, last, *args)


def _dense_body(x_ref, gate_ref, wg_ref, wu_ref, wd_ref, y_ref, xb_ref):
    e = pl.program_id(1)

    @pl.when(e == 0)
    def _():
        xb_ref[...] = _bf(x_ref[...])
        y_ref[...] = jnp.zeros(y_ref.shape, jnp.float32)

    xb = xb_ref[...]
    h = _silu(_dot(xb, _bf(wg_ref[0, 0]))) * _dot(xb, _bf(wu_ref[0, 0]))
    down = _dot(_bf(h), _bf(wd_ref[0, 0]))
    lane = lax.broadcasted_iota(jnp.int32, gate_ref.shape, 1)
    gate = jnp.sum(jnp.where(lane == e, gate_ref[...], 0.0), axis=-1, keepdims=True)
    y_ref[...] += down * gate


def _dense_experts(layer, x, gates, w_gate, w_up, w_down):
    rows = x.shape[0]
    tm = DENSE_TILE
    w_in_spec = pl.BlockSpec((1, 1, D_MODEL, EXPERT_FF), lambda i, e: (layer, e, 0, 0))
    return pl.pallas_call(
        _dense_body,
        grid=(rows // tm, N_EXPERTS),
        in_specs=[pl.BlockSpec((tm, D_MODEL), lambda i, e: (i, 0)),
                  pl.BlockSpec((tm, GATE_LANES), lambda i, e: (i, 0)),
                  w_in_spec, w_in_spec,
                  pl.BlockSpec((1, 1, EXPERT_FF, D_MODEL), lambda i, e: (layer, e, 0, 0))],
        out_specs=pl.BlockSpec((tm, D_MODEL), lambda i, e: (i, 0)),
        out_shape=jax.ShapeDtypeStruct((rows, D_MODEL), jnp.float32),
        scratch_shapes=[pltpu.VMEM((tm, D_MODEL), jnp.bfloat16)],
        compiler_params=pltpu.CompilerParams(dimension_semantics=("arbitrary", "arbitrary"),
                                             vmem_limit_bytes=_vmem_limit(40 * 1024 * 1024)),
        name="dense_overflow",
    )(x, gates, w_gate, w_up, w_down)


def _channel_sublayer(layer, x, w_router_t, bias_col, w_gate, w_up, w_down, ws_gate, ws_up, ws_down, g2, b2,
                      prompt_rows=None):
    xs, sel, gate, over, flags, fill = _route_dispatch(x, w_router_t, bias_col)
    fill = fill[:, :, :ROUTE_TILES_PER_STEP].transpose(1, 0, 2).reshape(N_EXPERTS, -1)
    ym, yt = _experts(layer, xs, sel, gate, fill, w_gate, w_up, w_down)
    rest = (x, sel, fill, ym, yt, ws_gate, ws_up, ws_down, g2, b2)

    def with_overflow():
        extra = _dense_experts(layer, x, over, w_gate, w_up, w_down)
        return _combine(layer, *rest, extra=extra, prompt_rows=prompt_rows)

    def without_overflow():
        return _combine(layer, *rest, prompt_rows=prompt_rows)

    return lax.cond(jnp.max(flags) > 0.0, with_overflow, without_overflow)


def kernel(x_prompt, x_sample, mem_prompt, cache_mem_k, cache_mem_v, state_conv_a, state_conv_b, w_in_a, conv_a_w, conv_a_b, norm_a_g, norm_a_b, w_in_b, conv_b_w, w_kv, w_out, ln1_g, ln1_b, w_router, router_bias, w_gate, w_up, w_down, ws_gate, ws_up, ws_down, ln2_g, ln2_b):
    batch, seq, d = x_prompt.shape
    n_seq, n_pos, _ = x_sample.shape
    c = MIX_WIDTH
    p_rows, s_rows = batch * seq, n_pos * n_seq
    s_block = p_rows // s_rows
    row = lambda a: a.reshape(1, -1)

    x_p = x_prompt.reshape(p_rows, d)
    x_s, x_s_block = x_sample.transpose(1, 0, 2).reshape(s_rows, d), 0
    k_all, v_all = _kv_projection(mem_prompt.reshape(batch * N_MEM, d), w_kv)
    k_p = k_all.reshape(DEPTH, batch, N_MEM, XATTN_WIDTH)
    v_p = v_all.reshape(DEPTH, batch, N_MEM, XATTN_WIDTH)

    conv_a_p, conv_b_p, conv_a_s, conv_b_s = [], [], [], []
    for i in range(DEPTH):
        j = i // N_MIXERS
        is_a = i % N_MIXERS == 0
        if is_a:
            w_in, cw = _bf(w_in_a[j]), conv_a_w[j]
            cb, ng, nb = row(conv_a_b[j]), row(norm_a_g[j]), row(norm_a_b[j])
            hist_s = state_conv_a[j]
        else:
            w_in, cw = _bf(w_in_b[j]), conv_b_w[j]
            cb = ng = nb = jnp.zeros((1, c), jnp.float32)
            hist_s = state_conv_b[j]
        w_o = _bf(w_out[i])
        g1, b1 = row(ln1_g[i]), row(ln1_b[i])

        mix, q, hist_s_new = _sample_mix(is_a, n_seq, s_rows, x_s, x_s_block, w_in, hist_s.transpose(1, 0, 2),
                                         cw, cb, ng, nb)
        attn = _sample_attention(i, n_seq, q, cache_mem_k, cache_mem_v)
        h_s = _sample_out(s_rows, x_s, x_s_block, mix, attn, w_o, g1, b1)
        h, hist_p_new = _prompt_token_sublayer(is_a, i, x_p, h_s, batch, seq, w_in, cw, cb, ng, nb,
                                               k_p, v_p, w_o, g1, b1)
        hist_s_new = hist_s_new.transpose(1, 0, 2)
        if is_a:
            conv_a_p.append(hist_p_new)
            conv_a_s.append(hist_s_new)
        else:
            conv_b_p.append(hist_p_new)
            conv_b_s.append(hist_s_new)

        last = i == DEPTH - 1
        h = _channel_sublayer(i, h, _bf(w_router[i].T), router_bias[i].reshape(N_EXPERTS, 1),
                              w_gate, w_up, w_down, ws_gate, ws_up, ws_down, row(ln2_g[i]), row(ln2_b[i]),
                              prompt_rows=p_rows if last else None)
        if not last:
            x_p = h
            x_s, x_s_block = h, s_block

    y_p, y_s = h
    new_k = k_all.reshape(DEPTH, batch, N_MEM, N_XHEADS, XHEAD_DIM)
    new_v = v_all.reshape(DEPTH, batch, N_MEM, N_XHEADS, XHEAD_DIM)
    return (y_p.reshape(batch, seq, d), y_s.reshape(n_pos, n_seq, d).transpose(1, 0, 2), new_k, new_v,
            jnp.stack(conv_a_p), jnp.stack(conv_b_p), jnp.stack(conv_a_s), jnp.stack(conv_b_s))
```

```python
import functools

import jax
import jax.numpy as jnp
from jax import lax
from jax.experimental import pallas as pl
from jax.experimental.pallas import tpu as pltpu

D_MODEL = 1024
DEPTH = 2
N_MIXERS = 2
MIX_WIDTH = D_MODEL // 2
N_MEM = 256
N_XHEADS = 4
XHEAD_DIM = MIX_WIDTH // N_XHEADS
XATTN_WIDTH = N_XHEADS * XHEAD_DIM
CONV_A_WIDTH = 31
CONV_B_WIDTH = 3
N_EXPERTS = 64
TOP_K = 8
N_GROUPS = 8
GROUP_SIZE = N_EXPERTS // N_GROUPS
TOPK_GROUPS = 4
EXPERT_FF = D_MODEL // 4
ROUTED_SCALE = 2.5
LN_EPS = 1e-5
DEEPNORM_ALPHA = (2 * DEPTH) ** 0.25

V7X_LANES = 128
V7X_SUBLANES = 8
V7X_VMEM_BYTES = 64 * 1024 * 1024

HIST_PAD = 32
PROMPT_SEQ_TILE = 512
SAMPLE_BATCH_BLOCK = 8
TOKEN_TILE = 256
SLOT_WINDOW = 64
SLOT_GROUP = 16
SLOT_GROUPS_ALWAYS = 3
EXPERT_CHUNK = 8
ROUTE_TILES_PER_STEP = 2
COMBINE_TILES_PER_STEP = 2
COMBINE_CHUNK = 16
EXPERT_ROW_CHUNKS = 1
GATE_LANES = V7X_LANES
DENSE_TILE = 512
NEG_INF = float("-inf")


def _vmem_limit(nbytes):
    return int(min(max(nbytes, 16 * 1024 * 1024), V7X_VMEM_BYTES - 8 * 1024 * 1024))


def _bf(x):
    return x.astype(jnp.bfloat16)


def _dot(a, b):
    return jnp.dot(a, b, preferred_element_type=jnp.float32)


def _dot_nt(a, b):
    return lax.dot_general(a, b, (((1,), (1,)), ((), ())), preferred_element_type=jnp.float32)


def _dot_tn(a, b):
    return lax.dot_general(a, b, (((0,), (0,)), ((), ())), preferred_element_type=jnp.float32)


def _sigmoid(x):
    return 1.0 / (1.0 + jnp.exp(-x))


def _silu(x):
    return x * _sigmoid(x)


def _layer_norm(x, g, b):
    mu = jnp.mean(x, axis=-1, keepdims=True)
    xc = x - mu
    var = jnp.mean(xc * xc, axis=-1, keepdims=True)
    return xc * lax.rsqrt(var + LN_EPS) * g + b


def _memory_attention(q, k_head, v_head):
    outs = []
    for h in range(N_XHEADS):
        sl = slice(h * XHEAD_DIM, (h + 1) * XHEAD_DIM)
        s = _dot_nt(_bf(q[:, sl]), _bf(k_head(h))) * (XHEAD_DIM ** -0.5)
        e = jnp.exp(s - jnp.max(s, axis=-1, keepdims=True))
        p = e / jnp.sum(e, axis=-1, keepdims=True)
        outs.append(_dot(_bf(p), _bf(v_head(h))))
    return jnp.concatenate(outs, axis=-1)


def _full(shape):
    return pl.BlockSpec(shape, lambda *_: tuple(0 for _ in shape))


def _kv_body(mem_ref, w_ref, k_ref, v_ref):
    kv = _dot(_bf(mem_ref[...]), _bf(w_ref[0]))
    k_ref[0] = kv[:, :XATTN_WIDTH]
    v_ref[0] = kv[:, XATTN_WIDTH:]


def _kv_projection(mem2d, w_kv):
    rows = mem2d.shape[0]
    tm = 512
    out = jax.ShapeDtypeStruct((DEPTH, rows, XATTN_WIDTH), jnp.float32)
    return pl.pallas_call(
        _kv_body,
        grid=(DEPTH, rows // tm),
        in_specs=[pl.BlockSpec((tm, D_MODEL), lambda i, m: (m, 0)),
                  pl.BlockSpec((1, D_MODEL, 2 * XATTN_WIDTH), lambda i, m: (i, 0, 0))],
        out_specs=[pl.BlockSpec((1, tm, XATTN_WIDTH), lambda i, m: (i, m, 0)),
                   pl.BlockSpec((1, tm, XATTN_WIDTH), lambda i, m: (i, m, 0))],
        out_shape=[out, out],
        compiler_params=pltpu.CompilerParams(dimension_semantics=("arbitrary", "arbitrary"),
                                             vmem_limit_bytes=_vmem_limit(32 * 1024 * 1024)),
        name="kv_projection",
    )(mem2d, w_kv)


def _prompt_token_body(is_a, nl, n_tiles, x_ref, w_in_ref, cw_ref, cb_ref, ng_ref, nb_ref, k_ref, v_ref, w_out_ref,
                       g1_ref, b1_ref, sample_ref, y_ref, hist_ref, buf_ref):
    c = MIX_WIDTH
    tl = x_ref.shape[0]
    width = CONV_A_WIDTH if is_a else CONV_B_WIDTH
    step = pl.program_id(0)
    seq_step = step % nl

    @pl.when(step == n_tiles)
    def _():
        y_ref[...] = sample_ref[...]

    @pl.when((step < n_tiles) & (seq_step == 0))
    def _():
        buf_ref[pl.ds(0, HIST_PAD), :] = jnp.zeros((HIST_PAD, c), jnp.float32)

    @pl.when(step < n_tiles)
    def _():
        x = x_ref[...]
        u = _dot(_bf(x), w_in_ref[...])
        conv_in = u[:, :c] * _sigmoid(u[:, c:2 * c]) if is_a else u[:, c:2 * c] * u[:, 2 * c:3 * c]
        buf_ref[pl.ds(HIST_PAD, tl), :] = conv_in

        base = HIST_PAD - (width - 1)
        conv = None
        for phase in range(V7X_SUBLANES):
            taps = [t for t in range(width) if (base + t) % V7X_SUBLANES == phase]
            if not taps:
                continue
            n = tl if phase == 0 else tl + V7X_SUBLANES
            part = None
            for t in taps:
                term = cw_ref[t:t + 1, :] * buf_ref[pl.ds(base + t - phase, n), :]
                part = term if part is None else part + term
            part = part[phase:phase + tl, :]
            conv = part if conv is None else conv + part

        if is_a:
            mix = _silu(_layer_norm(conv + cb_ref[...], ng_ref[...], nb_ref[...]))
            q = u[:, 2 * c:]
        else:
            mix = u[:, :c] * conv
            q = u[:, 3 * c:]
        head = lambda ref: lambda h: ref[0, 0, :, h * XHEAD_DIM:(h + 1) * XHEAD_DIM]
        attn = _memory_attention(q, head(k_ref), head(v_ref))
        out = _dot(_bf(jnp.concatenate([mix, attn], axis=-1)), w_out_ref[...])
        y_ref[...] = _layer_norm(DEEPNORM_ALPHA * x + out, g1_ref[...], b1_ref[...])

    @pl.when((step < n_tiles) & (seq_step == nl - 1))
    def _():
        hist_ref[0] = buf_ref[pl.ds(HIST_PAD + tl - (width - 1), width - 1), :]

    @pl.when(step < n_tiles)
    def _():
        buf_ref[pl.ds(0, HIST_PAD), :] = buf_ref[pl.ds(tl, HIST_PAD), :]


def _prompt_token_sublayer(is_a, layer, x, sample_rows, batch, seq, w_in, cw, cb, ng, nb, k, v, w_out, g1, b1):
    tl = PROMPT_SEQ_TILE
    assert sample_rows.shape == (tl, D_MODEL)
    nl = seq // tl
    n_tiles = batch * nl
    c = MIX_WIDTH
    width = CONV_A_WIDTH if is_a else CONV_B_WIDTH
    n_in = w_in.shape[1]
    tile = lambda s: jnp.minimum(s, n_tiles - 1)
    mem_spec = pl.BlockSpec((1, 1, N_MEM, XATTN_WIDTH), lambda s: (layer, tile(s) // nl, 0, 0))
    return pl.pallas_call(
        functools.partial(_prompt_token_body, is_a, nl, n_tiles),
        grid=(n_tiles + 1,),
        in_specs=[pl.BlockSpec((tl, D_MODEL), lambda s: (tile(s), 0)),
                  _full((D_MODEL, n_in)), _full((width, c)), _full((1, c)), _full((1, c)), _full((1, c)),
                  mem_spec, mem_spec,
                  _full((c + XATTN_WIDTH, D_MODEL)), _full((1, D_MODEL)), _full((1, D_MODEL)),
                  _full((tl, D_MODEL))],
        out_specs=[pl.BlockSpec((tl, D_MODEL), lambda s: (s, 0)),
                   pl.BlockSpec((1, width - 1, c), lambda s: (tile(s) // nl, 0, 0))],
        out_shape=[jax.ShapeDtypeStruct(((n_tiles + 1) * tl, D_MODEL), jnp.float32),
                   jax.ShapeDtypeStruct((batch, width - 1, c), jnp.float32)],
        scratch_shapes=[pltpu.VMEM((HIST_PAD + tl, c), jnp.float32)],
        compiler_params=pltpu.CompilerParams(dimension_semantics=("arbitrary",),
                                             vmem_limit_bytes=_vmem_limit(48 * 1024 * 1024)),
        name="prompt_token_a" if is_a else "prompt_token_b",
    )(x, w_in, cw, cb, ng, nb, k, v, w_out, g1, b1, sample_rows)


def _sample_mix_body(is_a, n_seq, x_ref, w_in_ref, hist_ref, cw_ref, cb_ref, ng_ref, nb_ref,
                     mix_ref, q_ref, new_hist_ref):
    c = MIX_WIDTH
    width = CONV_A_WIDTH if is_a else CONV_B_WIDTH
    n_hist = width - 1
    n_pos = x_ref.shape[0] // n_seq
    u = _dot(_bf(x_ref[...]), w_in_ref[...])
    if is_a:
        conv_in = u[:, :c] * _sigmoid(u[:, c:2 * c])
        q_ref[...] = u[:, 2 * c:]
    else:
        conv_in = u[:, c:2 * c] * u[:, 2 * c:3 * c]
        q_ref[...] = u[:, 3 * c:]

    def full_row(j):
        if j < n_hist:
            return hist_ref[j]
        return conv_in[(j - n_hist) * n_seq:(j - n_hist + 1) * n_seq, :]

    for l in range(n_pos):
        conv = cw_ref[0:1, :] * full_row(l)
        for t in range(1, width):
            conv = conv + cw_ref[t:t + 1, :] * full_row(l + t)
        rows = slice(l * n_seq, (l + 1) * n_seq)
        if is_a:
            mix_ref[rows, :] = _silu(_layer_norm(conv + cb_ref[...], ng_ref[...], nb_ref[...]))
        else:
            mix_ref[rows, :] = u[rows, :c] * conv
    for j in range(n_hist):
        new_hist_ref[j] = full_row(j + n_pos)


def _sample_mix(is_a, n_seq, rows, x, x_block, w_in, hist, cw, cb, ng, nb):
    c = MIX_WIDTH
    small = (w_in, hist, cw, cb, ng, nb)
    return pl.pallas_call(
        functools.partial(_sample_mix_body, is_a, n_seq),
        grid=(1,),
        in_specs=[pl.BlockSpec((rows, D_MODEL), lambda i: (x_block, 0))] + [_full(a.shape) for a in small],
        out_specs=[_full((rows, c)), _full((rows, XATTN_WIDTH)), _full(hist.shape)],
        out_shape=[jax.ShapeDtypeStruct((rows, c), jnp.float32),
                   jax.ShapeDtypeStruct((rows, XATTN_WIDTH), jnp.float32),
                   jax.ShapeDtypeStruct(hist.shape, jnp.float32)],
        compiler_params=pltpu.CompilerParams(dimension_semantics=("arbitrary",),
                                             vmem_limit_bytes=_vmem_limit(48 * 1024 * 1024)),
        name="sample_mix_a" if is_a else "sample_mix_b",
    )(x, *small)


def _sample_attn_body(n_seq, q_ref, k_ref, v_ref, o_ref, bias_ref):
    bb = k_ref.shape[1]
    n_pos = q_ref.shape[0] // n_seq
    first = pl.multiple_of(pl.program_id(0) * bb, bb)

    @pl.when(pl.program_id(0) == 0)
    def _():
        r = lax.broadcasted_iota(jnp.int32, bias_ref.shape, 0)
        col = lax.broadcasted_iota(jnp.int32, bias_ref.shape, 1)
        valid = ((r % N_XHEADS) == (col // (n_pos * bb))) & ((r // (N_MEM * N_XHEADS)) == (col % bb))
        bias_ref[...] = jnp.where(valid, 0.0, NEG_INF)

    q = jnp.concatenate([q_ref[pl.ds(l * n_seq + first, bb), :] for l in range(n_pos)], axis=0)
    nq = n_pos * bb
    n_rows = bb * N_MEM * N_XHEADS
    k_rows = k_ref[0].reshape(n_rows, XHEAD_DIM)
    v_rows = v_ref[0].reshape(n_rows, XHEAD_DIM)
    q_heads = jnp.concatenate([q[:, h * XHEAD_DIM:(h + 1) * XHEAD_DIM] for h in range(N_XHEADS)], axis=0)
    s = _dot_nt(_bf(k_rows), _bf(q_heads)) * (XHEAD_DIM ** -0.5) + bias_ref[...]
    e = jnp.exp(s - jnp.max(s, axis=0, keepdims=True))
    p = e / jnp.sum(e, axis=0, keepdims=True)
    o_heads = _dot_tn(_bf(p), _bf(v_rows))
    o = jnp.concatenate([o_heads[h * nq:(h + 1) * nq] for h in range(N_XHEADS)], axis=1)
    for l in range(n_pos):
        o_ref[pl.ds(l * n_seq + first, bb), :] = o[l * bb:(l + 1) * bb, :]


def _sample_attention(layer, n_seq, q, mem_k, mem_v):
    rows = q.shape[0]
    bb = SAMPLE_BATCH_BLOCK
    mem_spec = pl.BlockSpec((1, bb, N_MEM, N_XHEADS, XHEAD_DIM), lambda i: (layer, i, 0, 0, 0))
    return pl.pallas_call(
        functools.partial(_sample_attn_body, n_seq),
        grid=(n_seq // bb,),
        in_specs=[pl.BlockSpec((rows, XATTN_WIDTH), lambda i: (0, 0)), mem_spec, mem_spec],
        out_specs=pl.BlockSpec((rows, XATTN_WIDTH), lambda i: (0, 0)),
        out_shape=jax.ShapeDtypeStruct((rows, XATTN_WIDTH), jnp.float32),
        scratch_shapes=[pltpu.VMEM((bb * N_MEM * N_XHEADS, N_XHEADS * (rows // n_seq) * bb), jnp.float32)],
        compiler_params=pltpu.CompilerParams(dimension_semantics=("arbitrary",),
                                             vmem_limit_bytes=_vmem_limit(40 * 1024 * 1024)),
        name="sample_attention",
    )(q, mem_k, mem_v)


def _sample_out_body(x_ref, mix_ref, attn_ref, w_out_ref, g1_ref, b1_ref, y_ref):
    cat = jnp.concatenate([mix_ref[...], attn_ref[...]], axis=-1)
    out = _dot(_bf(cat), w_out_ref[...])
    y_ref[...] = _layer_norm(DEEPNORM_ALPHA * x_ref[...] + out, g1_ref[...], b1_ref[...])


def _sample_out(rows, x, x_block, mix, attn, w_out, g1, b1):
    small = (mix, attn, w_out, g1, b1)
    return pl.pallas_call(
        _sample_out_body,
        grid=(1,),
        in_specs=[pl.BlockSpec((rows, D_MODEL), lambda i: (x_block, 0))] + [_full(a.shape) for a in small],
        out_specs=_full((rows, D_MODEL)),
        out_shape=jax.ShapeDtypeStruct((rows, D_MODEL), jnp.float32),
        compiler_params=pltpu.CompilerParams(dimension_semantics=("arbitrary",),
                                             vmem_limit_bytes=_vmem_limit(32 * 1024 * 1024)),
        name="sample_out",
    )(x, *small)


def _first_index_of(mask, index, n):
    cand = jnp.where(mask, index, float(n))
    while cand.ndim > 2:
        cand = jnp.min(cand, axis=0)
    return jnp.min(cand, axis=0, keepdims=True)


def _max_all(x):
    while x.ndim > 2:
        x = jnp.max(x, axis=0)
    return jnp.max(x, axis=0, keepdims=True)


def _routing_stages(xb, wr_ref, bias_ref, out):
    tm = xb.shape[0]
    scores = _sigmoid(_dot_nt(wr_ref[...], xb))
    biased = scores + bias_ref[...]
    grp = biased.reshape(N_GROUPS, GROUP_SIZE, tm)

    within = lax.broadcasted_iota(jnp.int32, grp.shape, 1).astype(jnp.float32)
    top1 = jnp.max(grp, axis=1, keepdims=True)
    first = jnp.min(jnp.where(grp == top1, within, float(GROUP_SIZE)), axis=1, keepdims=True)
    top2 = jnp.max(jnp.where(within == first, NEG_INF, grp), axis=1, keepdims=True)
    grp_score = (top1 + top2).reshape(N_GROUPS, tm)
    yield

    gidx = lax.broadcasted_iota(jnp.int32, grp_score.shape, 0).astype(jnp.float32)
    grp_sel = jnp.zeros(grp_score.shape, jnp.float32)
    for _ in range(TOPK_GROUPS):
        best = jnp.max(grp_score, axis=0, keepdims=True)
        pick = gidx == _first_index_of(grp_score == best, gidx, N_GROUPS)
        grp_sel = jnp.where(pick, 1.0, grp_sel)
        grp_score = jnp.where(pick, NEG_INF, grp_score)
    yield

    eidx = (lax.broadcasted_iota(jnp.int32, grp.shape, 0) * GROUP_SIZE
            + lax.broadcasted_iota(jnp.int32, grp.shape, 1)).astype(jnp.float32)
    cand = jnp.where(grp_sel.reshape(N_GROUPS, 1, tm) > 0.0, grp, NEG_INF)
    chosen = jnp.zeros(grp.shape, jnp.float32)
    for k in range(TOP_K):
        best = _max_all(cand).reshape(1, 1, tm)
        pick = eidx == _first_index_of(cand == best, eidx, N_EXPERTS).reshape(1, 1, tm)
        chosen = jnp.where(pick, 1.0, chosen)
        cand = jnp.where(pick, NEG_INF, cand)
        if k % 2 == 1:
            yield

    w = jnp.where(chosen > 0.0, scores.reshape(grp.shape), 0.0)
    total = jnp.sum(jnp.sum(w, axis=0), axis=0, keepdims=True).reshape(1, 1, tm)
    gates = (w / total * ROUTED_SCALE).reshape(N_EXPERTS, tm)
    chosen = chosen.reshape(N_EXPERTS, tm)
    yield

    earlier = (lax.broadcasted_iota(jnp.int32, (tm, tm), 0) < lax.broadcasted_iota(jnp.int32, (tm, tm), 1))
    pos = _dot(_bf(chosen), jnp.where(earlier, 1.0, 0.0).astype(jnp.bfloat16))
    routed = chosen > 0.0
    in_window = routed & (pos < float(SLOT_WINDOW))
    out["sel"] = jnp.where(in_window, pos, -1.0)
    out["gate"] = jnp.where(in_window, gates, 0.0)
    out["over"] = jnp.where(routed & (pos >= float(SLOT_WINDOW)), gates, 0.0)
    yield


def _slot_matches(sel_row, first_slot=0, n_slots=None):
    tm = sel_row.shape[1]
    n_slots = SLOT_WINDOW if n_slots is None else n_slots
    slot = (lax.broadcasted_iota(jnp.int32, (n_slots, tm), 0) + first_slot).astype(jnp.float32)
    return jnp.broadcast_to(sel_row, (n_slots, tm)) == slot


def _slot_onehot(sel, first_expert):
    rows = [_slot_matches(sel[e:e + 1, :]) for e in range(first_expert, first_expert + EXPERT_CHUNK)]
    return jnp.where(jnp.concatenate(rows, axis=0), 1.0, 0.0).astype(jnp.bfloat16)


def _route_body(x_ref, wr_ref, bias_ref, xs_ref, sel_ref, gate_ref, over_ref, flag_ref, fill_ref, xb_prev, sel_prev):
    tm = TOKEN_TILE
    n_sub = x_ref.shape[0] // tm
    n_groups = SLOT_WINDOW // SLOT_GROUP

    @pl.when(pl.program_id(0) == 0)
    def _():
        xb_prev[...] = jnp.zeros(xb_prev.shape, jnp.bfloat16)
        sel_prev[...] = jnp.full(sel_prev.shape, -1.0, jnp.float32)

    lane = lax.broadcasted_iota(jnp.int32, (N_EXPERTS, V7X_LANES), 1)
    fill = jnp.full((N_EXPERTS, V7X_LANES), -1.0, jnp.float32)
    any_over = None
    for t in range(n_sub):
        cols = slice(t * tm, (t + 1) * tm)
        xb_old, sel_old = xb_prev[cols, :], sel_prev[:, cols]
        xb = _bf(x_ref[cols, :])
        out = {}
        stages = _routing_stages(xb, wr_ref, bias_ref, out)
        onehot = _slot_onehot(sel_old, 0)
        for first in range(0, N_EXPERTS, EXPERT_CHUNK):
            later = _slot_onehot(sel_old, first + EXPERT_CHUNK) if first + EXPERT_CHUNK < N_EXPERTS else None
            next(stages, None)
            slots = _bf(_dot(onehot, xb_old))
            xs_ref[first:first + EXPERT_CHUNK, :, t] = slots.reshape(EXPERT_CHUNK, n_groups, SLOT_GROUP, D_MODEL)
            onehot = later
        for _ in stages:
            pass

        sel, over = out["sel"], out["over"]
        sel_ref[:, cols] = sel
        gate_ref[:, cols] = out["gate"]
        pad = jnp.zeros((GATE_LANES - N_EXPERTS, tm), jnp.float32)
        over_ref[cols, :] = jnp.concatenate([over, pad], axis=0).T
        tile_over = jnp.max(jnp.max(over, axis=0, keepdims=True), axis=1, keepdims=True)
        any_over = tile_over if any_over is None else jnp.maximum(any_over, tile_over)
        fill = jnp.where(lane == t, jnp.max(sel, axis=1, keepdims=True), fill)
        xb_prev[cols, :] = xb
        sel_prev[:, cols] = sel
    flag_ref[...] = jnp.broadcast_to(any_over, flag_ref.shape[1:])[None]
    fill_ref[...] = fill[None]


def _route_dispatch(x, w_router_t, bias_col):
    rows = x.shape[0]
    n_sub = ROUTE_TILES_PER_STEP
    tm = TOKEN_TILE * n_sub
    n_tiles = rows // tm
    n_groups = SLOT_WINDOW // SLOT_GROUP
    routed = lambda i: jnp.minimum(i, n_tiles - 1)
    return pl.pallas_call(
        _route_body,
        grid=(n_tiles + 1,),
        in_specs=[pl.BlockSpec((tm, D_MODEL), lambda i: (routed(i), 0)),
                  _full((N_EXPERTS, D_MODEL)), _full((N_EXPERTS, 1))],
        out_specs=[pl.BlockSpec((N_EXPERTS, n_groups, n_sub, SLOT_GROUP, D_MODEL),
                                lambda i: (0, 0, jnp.maximum(i - 1, 0), 0, 0)),
                   pl.BlockSpec((N_EXPERTS, tm), lambda i: (0, routed(i))),
                   pl.BlockSpec((N_EXPERTS, tm), lambda i: (0, routed(i))),
                   pl.BlockSpec((tm, GATE_LANES), lambda i: (routed(i), 0)),
                   pl.BlockSpec((1, 8, V7X_LANES), lambda i: (routed(i), 0, 0)),
                   pl.BlockSpec((1, N_EXPERTS, V7X_LANES), lambda i: (routed(i), 0, 0))],
        out_shape=[jax.ShapeDtypeStruct((N_EXPERTS, n_groups, n_tiles * n_sub, SLOT_GROUP, D_MODEL), jnp.bfloat16),
                   jax.ShapeDtypeStruct((N_EXPERTS, rows), jnp.float32),
                   jax.ShapeDtypeStruct((N_EXPERTS, rows), jnp.float32),
                   jax.ShapeDtypeStruct((rows, GATE_LANES), jnp.float32),
                   jax.ShapeDtypeStruct((n_tiles, 8, V7X_LANES), jnp.float32),
                   jax.ShapeDtypeStruct((n_tiles, N_EXPERTS, V7X_LANES), jnp.float32)],
        scratch_shapes=[pltpu.VMEM((tm, D_MODEL), jnp.bfloat16), pltpu.VMEM((N_EXPERTS, tm), jnp.float32)],
        compiler_params=pltpu.CompilerParams(dimension_semantics=("arbitrary",),
                                             vmem_limit_bytes=_vmem_limit(56 * 1024 * 1024)),
        name="route_dispatch",
    )(x, w_router_t, bias_col)


def _expert_body(need_ref, last_e_ref, last_c_ref, xm_ref, xt_ref, sel_ref, gate_ref, wg_ref, wu_ref, wd_ref,
                 ym_ref, yt_ref, wg_bf, wu_bf, wd_bf):
    del last_e_ref, last_c_ref

    @pl.when(pl.program_id(1) == 0)
    def _():
        wg_bf[...] = _bf(wg_ref[0, 0])
        wu_bf[...] = _bf(wu_ref[0, 0])
        wd_bf[...] = _bf(wd_ref[0, 0])

    row = pl.ds(pl.program_id(0) % V7X_SUBLANES, 1)
    n_main = xm_ref.shape[1]
    n_tiles = xm_ref.shape[2]
    tm = sel_ref.shape[1] // n_tiles

    def run(x_ref, y_ref, first_group):
        n_g = x_ref.shape[1]
        rows = n_tiles * SLOT_GROUP
        proj = None
        for g in range(n_g + 1):
            nxt = None
            if g < n_g:
                x = x_ref[0, g].reshape(rows, D_MODEL)
                nxt = (_dot(x, wg_bf[...]), _dot(x, wu_bf[...]))
            if proj is not None:
                slot_gates = []
                for t in range(n_tiles):
                    cols = slice(t * tm, (t + 1) * tm)
                    match = _slot_matches(sel_ref[row, cols], (first_group + g - 1) * SLOT_GROUP, SLOT_GROUP)
                    slot_gates.append(jnp.sum(jnp.where(match, gate_ref[row, cols], 0.0), axis=-1, keepdims=True))
                gate = jnp.concatenate(slot_gates, axis=0)
                h = _silu(proj[0]) * proj[1]
                y = _bf(_dot(_bf(h), wd_bf[...]) * gate)
                y_ref[0, g - 1] = y.reshape(n_tiles, SLOT_GROUP, D_MODEL)
            proj = nxt

    run(xm_ref, ym_ref, 0)

    needed = need_ref[pl.program_id(0) * pl.num_programs(1) + pl.program_id(1)] > 0

    @pl.when(needed)
    def _():
        run(xt_ref, yt_ref, n_main)

    @pl.when(jnp.logical_not(needed))
    def _():
        yt_ref[...] = jnp.zeros(yt_ref.shape, jnp.bfloat16)


def _slot_group_split():
    n_groups = SLOT_WINDOW // SLOT_GROUP
    n_main = min(SLOT_GROUPS_ALWAYS, n_groups - 1)
    n_tail = n_groups - n_main
    assert n_main % n_tail == 0, "the trailing groups must form one block of the group axis"
    return n_main, n_tail


def _experts(layer, xs, sel, gate, fill, w_gate, w_up, w_down):
    n_e, _, n_tiles_all, _, _ = xs.shape
    n_chunks = EXPERT_ROW_CHUNKS
    n_tiles = n_tiles_all // n_chunks
    tokens = sel.shape[1] // n_chunks
    n_main, n_tail = _slot_group_split()

    need = (jnp.max(fill.reshape(n_e, n_chunks, n_tiles), axis=-1) >= n_main * SLOT_GROUP).astype(jnp.int32)
    steps = jnp.arange(n_e * n_chunks, dtype=jnp.int32)
    last = jnp.maximum(lax.cummax(jnp.where(need.reshape(-1) > 0, steps, -1)), 0)
    last_e, last_c = last // n_chunks, last % n_chunks

    w_in_spec = pl.BlockSpec((1, 1, D_MODEL, EXPERT_FF), lambda e, c, *_: (layer, e, 0, 0))
    route_spec = pl.BlockSpec((V7X_SUBLANES, tokens), lambda e, c, *_: (e // V7X_SUBLANES, c))
    main_spec = pl.BlockSpec((1, n_main, n_tiles, SLOT_GROUP, D_MODEL), lambda e, c, *_: (e, 0, c, 0, 0))

    def tail_map(group_block):
        return lambda e, c, nd, le, lc: (le[e * n_chunks + c], group_block, lc[e * n_chunks + c], 0, 0)

    tail_block = (1, n_tail, n_tiles, SLOT_GROUP, D_MODEL)
    ym, yt = pl.pallas_call(
        _expert_body,
        grid_spec=pltpu.PrefetchScalarGridSpec(
            num_scalar_prefetch=3,
            grid=(n_e, n_chunks),
            in_specs=[main_spec, pl.BlockSpec(tail_block, tail_map(n_main // n_tail)),
                      route_spec, route_spec, w_in_spec, w_in_spec,
                      pl.BlockSpec((1, 1, EXPERT_FF, D_MODEL), lambda e, c, *_: (layer, e, 0, 0))],
            out_specs=[main_spec, pl.BlockSpec(tail_block, lambda e, c, *_: (e, 0, c, 0, 0))],
            scratch_shapes=[pltpu.VMEM((D_MODEL, EXPERT_FF), jnp.bfloat16),
                            pltpu.VMEM((D_MODEL, EXPERT_FF), jnp.bfloat16),
                            pltpu.VMEM((EXPERT_FF, D_MODEL), jnp.bfloat16)],
        ),
        out_shape=[jax.ShapeDtypeStruct((n_e, n_main, n_tiles_all, SLOT_GROUP, D_MODEL), jnp.bfloat16),
                   jax.ShapeDtypeStruct((n_e, n_tail, n_tiles_all, SLOT_GROUP, D_MODEL), jnp.bfloat16)],
        compiler_params=pltpu.CompilerParams(dimension_semantics=("arbitrary", "arbitrary"),
                                             vmem_limit_bytes=_vmem_limit(56 * 1024 * 1024)),
        name="experts",
    )(need.reshape(-1), last_e, last_c, xs, xs, sel, gate, w_gate, w_up, w_down)
    return ym, yt


def _combine_body(has_extra, n_prompt_tiles, tile_need_ref, last_ref,
                  x_ref, sel_ref, ym_ref, yt_ref, wsg_ref, wsu_ref, wsd_ref, g2_ref, b2_ref, *rest):
    del last_ref
    rest = list(rest)
    extra_ref = rest.pop(0) if has_extra else None
    out_refs = [rest.pop(0)] if n_prompt_tiles is None else [rest.pop(0), rest.pop(0)]
    wsg_bf, wsu_bf, wsd_bf, acc_ref = rest
    step = pl.program_id(0)
    tm = TOKEN_TILE
    n_sub = x_ref.shape[0] // tm
    n_main, n_tail = ym_ref.shape[1], yt_ref.shape[1]

    @pl.when(step == 0)
    def _():
        wsg_bf[...] = _bf(wsg_ref[0])
        wsu_bf[...] = _bf(wsu_ref[0])
        wsd_bf[...] = _bf(wsd_ref[0])

    for t in range(n_sub):
        rows_t = slice(t * tm, (t + 1) * tm)
        tile = step * n_sub + t

        def onehot(first_expert, first_group, n_g):
            rows = [_slot_matches(sel_ref[e:e + 1, rows_t], first_group * SLOT_GROUP, n_g * SLOT_GROUP)
                    for e in range(first_expert, first_expert + COMBINE_CHUNK)]
            return jnp.where(jnp.concatenate(rows, axis=0), 1.0, 0.0).astype(jnp.bfloat16)

        xb = _bf(x_ref[rows_t, :])
        ahead = onehot(0, 0, n_main)
        y = _dot(_bf(_silu(_dot(xb, wsg_bf[...])) * _dot(xb, wsu_bf[...])), wsd_bf[...])
        for first in range(0, N_EXPERTS, COMBINE_CHUNK):
            current = ahead
            if first + COMBINE_CHUNK < N_EXPERTS:
                ahead = onehot(first + COMBINE_CHUNK, 0, n_main)
            ys = ym_ref[first:first + COMBINE_CHUNK, :, t].reshape(COMBINE_CHUNK * n_main * SLOT_GROUP, D_MODEL)
            y = y + _dot_tn(current, ys)
        if has_extra:
            y = y + extra_ref[rows_t, :]
        acc_ref[rows_t, :] = y

        @pl.when(tile_need_ref[tile] > 0)
        def _():
            part = jnp.zeros((tm, D_MODEL), jnp.float32)
            for first in range(0, N_EXPERTS, COMBINE_CHUNK):
                ye = yt_ref[first:first + COMBINE_CHUNK, :, t].reshape(COMBINE_CHUNK * n_tail * SLOT_GROUP, D_MODEL)
                part = part + _dot_tn(onehot(first, n_main, n_tail), ye)
            acc_ref[rows_t, :] += part

    y = _layer_norm(DEEPNORM_ALPHA * x_ref[...] + acc_ref[...], g2_ref[...], b2_ref[...])

    if n_prompt_tiles is None:
        out_refs[0][...] = y
    else:
        @pl.when(step < n_prompt_tiles)
        def _():
            out_refs[0][...] = y

        @pl.when(step >= n_prompt_tiles)
        def _():
            out_refs[1][...] = y


def _combine(layer, x, sel, fill, ym, yt, ws_gate, ws_up, ws_down, g2, b2, extra=None, prompt_rows=None):
    rows = x.shape[0]
    n_tiles = rows // TOKEN_TILE
    n_sub = COMBINE_TILES_PER_STEP
    tm = TOKEN_TILE * n_sub
    n_steps = n_tiles // n_sub
    has_extra = extra is not None
    n_main, n_tail = ym.shape[1], yt.shape[1]
    tile_need = (jnp.max(fill, axis=0) >= n_main * SLOT_GROUP).astype(jnp.int32)
    step_need = jnp.max(tile_need.reshape(n_steps, n_sub), axis=1)
    steps = jnp.arange(n_steps, dtype=jnp.int32)
    last = jnp.maximum(lax.cummax(jnp.where(step_need > 0, steps, -1)), 0)

    in_specs = [pl.BlockSpec((tm, D_MODEL), lambda i, *_: (i, 0)),
                pl.BlockSpec((N_EXPERTS, tm), lambda i, *_: (0, i)),
                pl.BlockSpec((N_EXPERTS, n_main, n_sub, SLOT_GROUP, D_MODEL), lambda i, *_: (0, 0, i, 0, 0)),
                pl.BlockSpec((N_EXPERTS, n_tail, n_sub, SLOT_GROUP, D_MODEL),
                             lambda i, tn, la: (0, 0, la[i], 0, 0)),
                pl.BlockSpec((1, D_MODEL, EXPERT_FF), lambda i, *_: (layer, 0, 0)),
                pl.BlockSpec((1, D_MODEL, EXPERT_FF), lambda i, *_: (layer, 0, 0)),
                pl.BlockSpec((1, EXPERT_FF, D_MODEL), lambda i, *_: (layer, 0, 0)),
                pl.BlockSpec((1, D_MODEL), lambda i, *_: (0, 0)), pl.BlockSpec((1, D_MODEL), lambda i, *_: (0, 0))]
    args = [x, sel, ym, yt, ws_gate, ws_up, ws_down, g2, b2]
    if has_extra:
        in_specs.append(pl.BlockSpec((tm, D_MODEL), lambda i, *_: (i, 0)))
        args.append(extra)
    if prompt_rows is None:
        n_prompt_tiles = None
        out_specs = pl.BlockSpec((tm, D_MODEL), lambda i, *_: (i, 0))
        out_shape = jax.ShapeDtypeStruct((rows, D_MODEL), jnp.float32)
    else:
        n_prompt_tiles = prompt_rows // tm
        out_specs = [pl.BlockSpec((tm, D_MODEL), lambda i, *_: (jnp.minimum(i, n_prompt_tiles - 1), 0)),
                     pl.BlockSpec((tm, D_MODEL), lambda i, *_: (jnp.maximum(i - n_prompt_tiles, 0), 0))]
        out_shape = [jax.ShapeDtypeStruct((prompt_rows, D_MODEL), jnp.float32),
                     jax.ShapeDtypeStruct((rows - prompt_rows, D_MODEL), jnp.float32)]
    return pl.pallas_call(
        functools.partial(_combine_body, has_extra, n_prompt_tiles),
        grid_spec=pltpu.PrefetchScalarGridSpec(
            num_scalar_prefetch=2,
            grid=(n_steps,),
            in_specs=in_specs,
            out_specs=out_specs,
            scratch_shapes=[pltpu.VMEM((D_MODEL, EXPERT_FF), jnp.bfloat16),
                            pltpu.VMEM((D_MODEL, EXPERT_FF), jnp.bfloat16),
                            pltpu.VMEM((EXPERT_FF, D_MODEL), jnp.bfloat16),
                            pltpu.VMEM((tm, D_MODEL), jnp.float32)],
        ),
        out_shape=out_shape,
        compiler_params=pltpu.CompilerParams(dimension_semantics=("arbitrary",),
                                             vmem_limit_bytes=_vmem_limit(56 * 1024 * 1024)),
        name="combine_extra" if has_extra else "combine",
    )(tile_need, last, *args)


def _dense_body(x_ref, gate_ref, wg_ref, wu_ref, wd_ref, y_ref, xb_ref):
    e = pl.program_id(1)

    @pl.when(e == 0)
    def _():
        xb_ref[...] = _bf(x_ref[...])
        y_ref[...] = jnp.zeros(y_ref.shape, jnp.float32)

    xb = xb_ref[...]
    h = _silu(_dot(xb, _bf(wg_ref[0, 0]))) * _dot(xb, _bf(wu_ref[0, 0]))
    down = _dot(_bf(h), _bf(wd_ref[0, 0]))
    lane = lax.broadcasted_iota(jnp.int32, gate_ref.shape, 1)
    gate = jnp.sum(jnp.where(lane == e, gate_ref[...], 0.0), axis=-1, keepdims=True)
    y_ref[...] += down * gate


def _dense_experts(layer, x, gates, w_gate, w_up, w_down):
    rows = x.shape[0]
    tm = DENSE_TILE
    w_in_spec = pl.BlockSpec((1, 1, D_MODEL, EXPERT_FF), lambda i, e: (layer, e, 0, 0))
    return pl.pallas_call(
        _dense_body,
        grid=(rows // tm, N_EXPERTS),
        in_specs=[pl.BlockSpec((tm, D_MODEL), lambda i, e: (i, 0)),
                  pl.BlockSpec((tm, GATE_LANES), lambda i, e: (i, 0)),
                  w_in_spec, w_in_spec,
                  pl.BlockSpec((1, 1, EXPERT_FF, D_MODEL), lambda i, e: (layer, e, 0, 0))],
        out_specs=pl.BlockSpec((tm, D_MODEL), lambda i, e: (i, 0)),
        out_shape=jax.ShapeDtypeStruct((rows, D_MODEL), jnp.float32),
        scratch_shapes=[pltpu.VMEM((tm, D_MODEL), jnp.bfloat16)],
        compiler_params=pltpu.CompilerParams(dimension_semantics=("arbitrary", "arbitrary"),
                                             vmem_limit_bytes=_vmem_limit(40 * 1024 * 1024)),
        name="dense_overflow",
    )(x, gates, w_gate, w_up, w_down)


def _channel_sublayer(layer, x, w_router_t, bias_col, w_gate, w_up, w_down, ws_gate, ws_up, ws_down, g2, b2,
                      prompt_rows=None):
    xs, sel, gate, over, flags, fill = _route_dispatch(x, w_router_t, bias_col)
    fill = fill[:, :, :ROUTE_TILES_PER_STEP].transpose(1, 0, 2).reshape(N_EXPERTS, -1)
    ym, yt = _experts(layer, xs, sel, gate, fill, w_gate, w_up, w_down)
    rest = (x, sel, fill, ym, yt, ws_gate, ws_up, ws_down, g2, b2)

    def with_overflow():
        extra = _dense_experts(layer, x, over, w_gate, w_up, w_down)
        return _combine(layer, *rest, extra=extra, prompt_rows=prompt_rows)

    def without_overflow():
        return _combine(layer, *rest, prompt_rows=prompt_rows)

    return lax.cond(jnp.max(flags) > 0.0, with_overflow, without_overflow)


def kernel(x_prompt, x_sample, mem_prompt, cache_mem_k, cache_mem_v, state_conv_a, state_conv_b, w_in_a, conv_a_w, conv_a_b, norm_a_g, norm_a_b, w_in_b, conv_b_w, w_kv, w_out, ln1_g, ln1_b, w_router, router_bias, w_gate, w_up, w_down, ws_gate, ws_up, ws_down, ln2_g, ln2_b):
    batch, seq, d = x_prompt.shape
    n_seq, n_pos, _ = x_sample.shape
    c = MIX_WIDTH
    p_rows, s_rows = batch * seq, n_pos * n_seq
    s_block = p_rows // s_rows
    row = lambda a: a.reshape(1, -1)

    x_p = x_prompt.reshape(p_rows, d)
    x_s, x_s_block = x_sample.transpose(1, 0, 2).reshape(s_rows, d), 0
    k_all, v_all = _kv_projection(mem_prompt.reshape(batch * N_MEM, d), w_kv)
    k_p = k_all.reshape(DEPTH, batch, N_MEM, XATTN_WIDTH)
    v_p = v_all.reshape(DEPTH, batch, N_MEM, XATTN_WIDTH)

    conv_a_p, conv_b_p, conv_a_s, conv_b_s = [], [], [], []
    for i in range(DEPTH):
        j = i // N_MIXERS
        is_a = i % N_MIXERS == 0
        if is_a:
            w_in, cw = _bf(w_in_a[j]), conv_a_w[j]
            cb, ng, nb = row(conv_a_b[j]), row(norm_a_g[j]), row(norm_a_b[j])
            hist_s = state_conv_a[j]
        else:
            w_in, cw = _bf(w_in_b[j]), conv_b_w[j]
            cb = ng = nb = jnp.zeros((1, c), jnp.float32)
            hist_s = state_conv_b[j]
        w_o = _bf(w_out[i])
        g1, b1 = row(ln1_g[i]), row(ln1_b[i])

        mix, q, hist_s_new = _sample_mix(is_a, n_seq, s_rows, x_s, x_s_block, w_in, hist_s.transpose(1, 0, 2),
                                         cw, cb, ng, nb)
        attn = _sample_attention(i, n_seq, q, cache_mem_k, cache_mem_v)
        h_s = _sample_out(s_rows, x_s, x_s_block, mix, attn, w_o, g1, b1)
        h, hist_p_new = _prompt_token_sublayer(is_a, i, x_p, h_s, batch, seq, w_in, cw, cb, ng, nb,
                                               k_p, v_p, w_o, g1, b1)
        hist_s_new = hist_s_new.transpose(1, 0, 2)
        if is_a:
            conv_a_p.append(hist_p_new)
            conv_a_s.append(hist_s_new)
        else:
            conv_b_p.append(hist_p_new)
            conv_b_s.append(hist_s_new)

        last = i == DEPTH - 1
        h = _channel_sublayer(i, h, _bf(w_router[i].T), router_bias[i].reshape(N_EXPERTS, 1),
                              w_gate, w_up, w_down, ws_gate, ws_up, ws_down, row(ln2_g[i]), row(ln2_b[i]),
                              prompt_rows=p_rows if last else None)
        if not last:
            x_p = h
            x_s, x_s_block = h, s_block

    y_p, y_s = h
    new_k = k_all.reshape(DEPTH, batch, N_MEM, N_XHEADS, XHEAD_DIM)
    new_v = v_all.reshape(DEPTH, batch, N_MEM, N_XHEADS, XHEAD_DIM)
    return (y_p.reshape(batch, seq, d), y_s.reshape(n_pos, n_seq, d).transpose(1, 0, 2), new_k, new_v,
            jnp.stack(conv_a_p), jnp.stack(conv_b_p), jnp.stack(conv_a_s), jnp.stack(conv_b_s))
```
